```python
import math
import jax, jax.numpy as jnp
from jax import lax
import numpy as np

D_MODEL = 1024
BATCH = 2
SEQ = 8192
DEPTH = 4

N_MIXERS = 2
N_HY_LAYERS = (DEPTH + 1) // 2
N_AT_LAYERS = DEPTH // 2
HEAD_DIM = 64
D_MIX = D_MODEL
MEM_LEN = 256
MEM_HEADS = 4
D_MEM = MEM_HEADS * HEAD_DIM
D_MAIN = D_MIX - D_MEM
N_Q_HEADS = D_MAIN // HEAD_DIM
N_KV_HEADS = 4
GQA_GROUP = N_Q_HEADS // N_KV_HEADS
Q_BLOCK = 128
GRID_W = 64
ROPE_THETA = 10000.0
ROPE_AXIS_DIM = HEAD_DIM // 2
HY_ORDER = 2
HY_SHORT = 3
HY_BANDS = 16
HY_EMB = 1 + 2 * HY_BANDS
HY_FILT = 64
HY_DECAY_TARGET = 1e-2
HY_FAST_PCT = 0.3
HY_SLOW_PCT = 1.5
N_EXPERTS = 16
EC_CAPACITY_FACTOR = 2
EXPERT_FF = 768
NORM_EPS = 1e-6
D_IN_HY = 3 * D_MAIN + D_MEM
D_IN_AT = D_MAIN + 2 * N_KV_HEADS * HEAD_DIM + D_MEM

kernel_name = 'hybrid_hyena_gqa_ec_moe_encoder'


def rms_norm(x, g):
    xf = x.astype(jnp.float32)
    y = xf * lax.rsqrt(jnp.mean(xf * xf, axis=-1, keepdims=True) + NORM_EPS)
    return (y * g.astype(jnp.float32)).astype(x.dtype)


def hyena_filters(L, w1, b1, w2, b2, w3, freq):
    f32 = jnp.float32
    t = jnp.linspace(0.0, 1.0, L, dtype=f32)[:, None]
    w = (2.0 * math.pi) * jnp.arange(L, dtype=f32)[:, None] / L
    bands = jnp.linspace(1e-4, HY_BANDS - 1, HY_BANDS, dtype=f32)[None, :]
    feats = jnp.concatenate([t, jnp.cos(bands * w), -jnp.sin(bands * w)], axis=-1)
    fr = freq.astype(f32)
    h = jnp.sin(fr * (feats @ w1.astype(f32) + b1.astype(f32)))
    h = jnp.sin(fr * (h @ w2.astype(f32) + b2.astype(f32)))
    h = (h @ w3.astype(f32)).reshape(L, HY_ORDER, 2, D_MAIN)
    max_decay = math.log(HY_DECAY_TARGET) / HY_FAST_PCT
    min_decay = math.log(HY_DECAY_TARGET) / HY_SLOW_PCT
    deltas = jnp.linspace(min_decay, max_decay, D_MAIN, dtype=f32)
    h = h * jnp.exp(-t[:, :, None, None] * jnp.abs(deltas))
    h = h / (jnp.sum(jnp.abs(h), axis=(0, 2), keepdims=True) + 1e-6)
    hf, hb = h[:, :, 0], h[:, :, 1]
    zero = jnp.zeros_like(hf[:1])
    return jnp.concatenate([hf[:1] + hb[:1], hf[1:], zero, hb[1:][::-1]], axis=0)


def long_conv(z, k_fft):
    T = z.shape[1]
    Z = jnp.fft.rfft(z, n=2 * T, axis=1)
    return jnp.fft.irfft(Z * k_fft[None], n=2 * T, axis=1)[:, :T]


def hyena_mixer(u, short_w, k_fft, skip):
    T = u.shape[1]
    pad = HY_SHORT // 2
    up = jnp.pad(u, ((0, 0), (pad, pad), (0, 0)))
    uc = sum(up[:, j:j + T] * short_w[j] for j in range(HY_SHORT))
    x1, x2, v = jnp.split(uc, 3, axis=-1)
    z = v.astype(jnp.float32)
    for o, gate in enumerate((x1, x2)):
        z = gate.astype(jnp.float32) * (long_conv(z, k_fft[:, o]) + skip[o].astype(jnp.float32) * z)
    return z.astype(u.dtype)


def axial_rope_tables(T):
    rows = T // GRID_W
    pos_row = jnp.repeat(jnp.arange(rows), GRID_W).astype(jnp.float32)
    pos_col = jnp.tile(jnp.arange(GRID_W), rows).astype(jnp.float32)
    inv = 1.0 / (ROPE_THETA ** (jnp.arange(0, ROPE_AXIS_DIM, 2, dtype=jnp.float32) / ROPE_AXIS_DIM))
    ang = jnp.stack([pos_row[:, None] * inv, pos_col[:, None] * inv], axis=1)
    return jnp.cos(ang), jnp.sin(ang)


def apply_axial_rope(x, cos, sin):
    B, T, H, Dh = x.shape
    nf = ROPE_AXIS_DIM // 2
    xr = x.astype(jnp.float32).reshape(B, T, H, 2, 2, nf)
    c = cos[None, :, None]
    s = sin[None, :, None]
    xa, xb = xr[..., 0, :], xr[..., 1, :]
    out = jnp.stack([xa * c - xb * s, xa * s + xb * c], axis=-2)
    return out.reshape(B, T, H, Dh).astype(x.dtype)


def blocked_gqa(q, k, v):
    B, T, Hq, Dh = q.shape
    nb = T // Q_BLOCK
    qb = q.reshape(B, nb, Q_BLOCK, N_KV_HEADS, GQA_GROUP, Dh).transpose(1, 0, 2, 3, 4, 5)
    scale = Dh ** -0.5

    def one_block(qblk):
        s = jnp.einsum('bqhgd,bkhd->bhgqk', qblk, k).astype(jnp.float32) * scale
        p = jax.nn.softmax(s, axis=-1).astype(v.dtype)
        return jnp.einsum('bhgqk,bkhd->bqhgd', p, v)

    o = lax.map(one_block, qb)
    return o.transpose(1, 0, 2, 3, 4, 5).reshape(B, T, Hq * Dh)


def memory_cross_attn(cq, mem_n, w_kv):
    B, T, _ = cq.shape
    M = mem_n.shape[1]
    q = cq.reshape(B, T, MEM_HEADS, HEAD_DIM)
    mk, mv = jnp.split(mem_n @ w_kv, 2, axis=-1)
    mk = mk.reshape(B, M, MEM_HEADS, HEAD_DIM)
    mv = mv.reshape(B, M, MEM_HEADS, HEAD_DIM)
    s = jnp.einsum('bqhd,bmhd->bhqm', q, mk).astype(jnp.float32) * (HEAD_DIM ** -0.5)
    p = jax.nn.softmax(s, axis=-1).astype(mv.dtype)
    return jnp.einsum('bhqm,bmhd->bqhd', p, mv).reshape(B, T, D_MEM)


def expert_choice_ffn(xn, w_router, w_gate, w_up, w_down):
    B, T, D = xn.shape
    cap = EC_CAPACITY_FACTOR * T // N_EXPERTS
    aff = jax.nn.softmax((xn @ w_router).astype(jnp.float32), axis=-1)
    g, idx = lax.top_k(jnp.swapaxes(aff, 1, 2), cap)
    bidx = jnp.arange(B)[:, None, None]
    xg = xn[bidx, idx]
    hid = jax.nn.silu(jnp.einsum('becd,edf->becf', xg, w_gate)) * jnp.einsum('becd,edf->becf', xg, w_up)
    y = jnp.einsum('becf,efd->becd', hid, w_down) * g[..., None].astype(xn.dtype)
    return jnp.zeros_like(xn).at[bidx, idx].add(y)


def setup_inputs(seed: int = 0) -> dict:
    key = jax.random.key(seed)
    ks = iter(jax.random.split(key, 32))

    def nrm(shape, scale):
        return jax.random.normal(next(ks), shape, jnp.float32) * scale

    def gain(shape):
        return 1.0 + nrm(shape, 0.02)

    D = D_MODEL
    return {
        'x': nrm((BATCH, SEQ, D), 1.0),
        'mem': nrm((BATCH, MEM_LEN, D), 1.0),
        'mix_norm_g': gain((DEPTH, D)),
        'ffn_norm_g': gain((DEPTH, D)),
        'mem_norm_g': gain((D,)),
        'final_norm_g': gain((D,)),
        'w_mem_kv': nrm((DEPTH, D, 2 * D_MEM), D ** -0.5),
        'w_out': nrm((DEPTH, D_MIX, D), D_MIX ** -0.5),
        'hy_w_in': nrm((N_HY_LAYERS, D, D_IN_HY), D ** -0.5),
        'hy_short_w': nrm((N_HY_LAYERS, HY_SHORT, 3 * D_MAIN), HY_SHORT ** -0.5),
        'hy_filt_w1': nrm((N_HY_LAYERS, HY_EMB, HY_FILT), HY_EMB ** -0.5),
        'hy_filt_b1': nrm((N_HY_LAYERS, HY_FILT), 0.02),
        'hy_filt_w2': nrm((N_HY_LAYERS, HY_FILT, HY_FILT), HY_FILT ** -0.5),
        'hy_filt_b2': nrm((N_HY_LAYERS, HY_FILT), 0.02),
        'hy_filt_w3': nrm((N_HY_LAYERS, HY_FILT, HY_ORDER * 2 * D_MAIN), HY_FILT ** -0.5),
        'hy_filt_freq': gain((N_HY_LAYERS, HY_FILT)),
        'hy_skip': nrm((N_HY_LAYERS, HY_ORDER, D_MAIN), 1.0),
        'at_w_in': nrm((N_AT_LAYERS, D, D_IN_AT), D ** -0.5),
        'at_q_norm_g': gain((N_AT_LAYERS, HEAD_DIM)),
        'at_k_norm_g': gain((N_AT_LAYERS, HEAD_DIM)),
        'router_w': nrm((DEPTH, D, N_EXPERTS), D ** -0.5),
        'exp_w_gate': nrm((DEPTH, N_EXPERTS, D, EXPERT_FF), D ** -0.5),
        'exp_w_up': nrm((DEPTH, N_EXPERTS, D, EXPERT_FF), D ** -0.5),
        'exp_w_down': nrm((DEPTH, N_EXPERTS, EXPERT_FF, D), EXPERT_FF ** -0.5),
    }


def reference(x, mem, mix_norm_g, ffn_norm_g, mem_norm_g, final_norm_g, w_mem_kv, w_out,
              hy_w_in, hy_short_w, hy_filt_w1, hy_filt_b1, hy_filt_w2, hy_filt_b2, hy_filt_w3,
              hy_filt_freq, hy_skip, at_w_in, at_q_norm_g, at_k_norm_g,
              router_w, exp_w_gate, exp_w_up, exp_w_down):
    B, T, _ = x.shape
    cos, sin = axial_rope_tables(T)
    mem_n = rms_norm(mem, mem_norm_g)
    h = x
    for i in range(DEPTH):
        j = i // N_MIXERS
        u = rms_norm(h, mix_norm_g[i])
        if i % N_MIXERS == 0:
            proj = u @ hy_w_in[j]
            main_in, cq = proj[..., :3 * D_MAIN], proj[..., 3 * D_MAIN:]
            k_taps = hyena_filters(T, hy_filt_w1[j], hy_filt_b1[j], hy_filt_w2[j], hy_filt_b2[j],
                                   hy_filt_w3[j], hy_filt_freq[j])
            k_fft = jnp.fft.rfft(k_taps, axis=0)
            main = hyena_mixer(main_in, hy_short_w[j], k_fft, hy_skip[j])
        else:
            proj = u @ at_w_in[j]
            o1 = D_MAIN
            o2 = o1 + N_KV_HEADS * HEAD_DIM
            o3 = o2 + N_KV_HEADS * HEAD_DIM
            q = proj[..., :o1].reshape(B, T, N_Q_HEADS, HEAD_DIM)
            kk = proj[..., o1:o2].reshape(B, T, N_KV_HEADS, HEAD_DIM)
            vv = proj[..., o2:o3].reshape(B, T, N_KV_HEADS, HEAD_DIM)
            cq = proj[..., o3:]
            q = apply_axial_rope(rms_norm(q, at_q_norm_g[j]), cos, sin)
            kk = apply_axial_rope(rms_norm(kk, at_k_norm_g[j]), cos, sin)
            main = blocked_gqa(q, kk, vv)
        cross = memory_cross_attn(cq, mem_n, w_mem_kv[i])
        h = h + jnp.concatenate([main, cross], axis=-1) @ w_out[i]
        h = h + expert_choice_ffn(rms_norm(h, ffn_norm_g[i]), router_w[i],
                                  exp_w_gate[i], exp_w_up[i], exp_w_down[i])
    return rms_norm(h, final_norm_g)
```

```python
import functools
import math

import numpy as np
import jax
import jax.numpy as jnp
from jax import lax
from jax.experimental import pallas as pl
from jax.experimental.pallas import tpu as pltpu

F32 = jnp.float32
BF16 = jnp.bfloat16
I32 = jnp.int32

HEAD_DIM = 64
MEM_HEADS = 4
D_MEM = MEM_HEADS * HEAD_DIM
N_KV_HEADS = 4
GQA_GROUP = 3
GRID_W = 64
ROPE_THETA = 10000.0
ROPE_AXIS_DIM = HEAD_DIM // 2
HY_BANDS = 16
HY_FILT = 64
HY_DECAY_TARGET = 1e-2
HY_FAST_PCT = 0.3
HY_SLOW_PCT = 1.5
N_EXPERTS = 16
EC_CAPACITY_FACTOR = 2
NORM_EPS = 1e-6

LANES = 128
BF16_SUBLANES = 16
VMEM_LIMIT = 56 * 1024 * 1024
HI = lax.Precision.HIGHEST
LOG2E = 1.4426950408889634


def _cp(*sem):
    return pltpu.CompilerParams(dimension_semantics=sem, vmem_limit_bytes=VMEM_LIMIT)


def _rms(x, g):
    ms = jnp.mean(x * x, axis=-1, keepdims=True)
    return x * lax.rsqrt(ms + NORM_EPS) * g


def _dot(a, b):
    return jnp.dot(a, b, preferred_element_type=F32)


def _dot_nt(a, b):
    return lax.dot_general(a, b, (((1,), (1,)), ((), ())), preferred_element_type=F32)


@functools.lru_cache(maxsize=None)
def _dft_tables(T, cb):
    nf = 2 * T
    n1k = nf // LANES
    n1 = T // LANES
    k1 = np.arange(n1k)[:, None]
    a1 = 2 * np.pi * k1 * np.arange(n1)[None, :] / n1k
    f1s = np.concatenate([np.cos(a1), -np.sin(a1)], axis=0)
    tw = 2 * np.pi * k1 * np.arange(LANES)[None, :] / nf
    twr, twi = np.cos(tw), -np.sin(tw)
    a2 = 2 * np.pi * np.arange(LANES)[:, None] * np.arange(LANES)[None, :] / LANES
    cr, ci = np.cos(a2), -np.sin(a2)
    wr = np.concatenate([cr, ci], axis=1)
    wi = np.concatenate([-ci, cr], axis=1)
    minv = np.block([[cr, -ci], [ci, cr]])
    a3 = 2 * np.pi * np.arange(n1)[:, None] * np.arange(n1k)[None, :] / n1k
    g1 = np.concatenate([np.cos(a3), -np.sin(a3)], axis=1) / nf
    return dict(
        f1s=f1s, wr=wr, wi=wi, minv=minv, g1=g1,
        twr_l=np.tile(twr, (1, cb)), twi_l=np.tile(twi, (1, cb)),
        twr_r=np.tile(twr, (cb, 1)), twi_r=np.tile(twi, (cb, 1)),
    )


@functools.lru_cache(maxsize=None)
def _filter_feats(T):
    t = np.linspace(0.0, 1.0, T)[None, :]
    w = (2.0 * np.pi) * np.arange(T)[None, :] / T
    bands = np.linspace(1e-4, HY_BANDS - 1, HY_BANDS)[:, None]
    feats = np.concatenate([t, np.cos(bands * w), -np.sin(bands * w)], axis=0)
    pad = np.zeros((40 - feats.shape[0], T))
    return np.concatenate([feats, pad], axis=0).astype(np.float32), t.astype(np.float32)


@functools.lru_cache(maxsize=None)
def _rope_tables(T):
    rows = T // GRID_W
    pos_row = np.repeat(np.arange(rows), GRID_W).astype(np.float64)
    pos_col = np.tile(np.arange(GRID_W), rows).astype(np.float64)
    inv = 1.0 / (ROPE_THETA ** (np.arange(0, ROPE_AXIS_DIM, 2, dtype=np.float64) / ROPE_AXIS_DIM))
    lane = np.arange(LANES)
    d = lane % HEAD_DIM
    axis = d // ROPE_AXIS_DIM
    half = (d // (ROPE_AXIS_DIM // 2)) % 2
    f = d % (ROPE_AXIS_DIM // 2)
    pos = np.where(axis[None, :] == 0, pos_row[:, None], pos_col[:, None])
    ang = pos * inv[f][None, :]
    cos2 = np.cos(ang)
    sin2 = np.where(half[None, :] == 0, -np.sin(ang), np.sin(ang))
    return cos2.astype(np.float32), sin2.astype(np.float32)


@functools.lru_cache(maxsize=None)
def _topk_tables(T, tb):
    nj = T // LANES
    per = tb // LANES
    r = np.arange(nj * N_EXPERTS)
    j, e = r // N_EXPERTS, r % N_EXPERTS
    same_e = e[:, None] == e[None, :]
    blk = j // per
    same_blk = blk[:, None] == blk[None, :]
    m_all = same_e & (j[None, :] < j[:, None])
    m_w = same_e & same_blk & (j[None, :] < j[:, None])
    m_b = same_e & same_blk
    m_a = same_e & (blk[None, :] < blk[:, None]) & ((j % per) == 0)[None, :]
    tri = np.triu(np.ones((LANES, LANES)))
    return tuple(np.asarray(m, np.float32) for m in (m_all, m_w, m_b, m_a, tri))


def _hy_inproj_kernel(h_ref, g_ref, wmt_ref, wcq_ref, ut_ref, cq_ref):
    xn = _rms(h_ref[...], g_ref[...]).astype(BF16)
    ut_ref[0] = _dot_nt(wmt_ref[...], xn)
    cq_ref[...] = _dot(xn, wcq_ref[...]).astype(BF16)


def _hy_inproj(h, g, wmt, wcq, B, T, tm):
    N, D = h.shape
    c3 = wmt.shape[0]
    nt = T // tm
    return pl.pallas_call(
        _hy_inproj_kernel,
        grid=(B, nt),
        in_specs=[
            pl.BlockSpec((tm, D), lambda b, i: (b * nt + i, 0)),
            pl.BlockSpec((1, D), lambda b, i: (0, 0)),
            pl.BlockSpec((c3, D), lambda b, i: (0, 0)),
            pl.BlockSpec((D, D_MEM), lambda b, i: (0, 0)),
        ],
        out_specs=[
            pl.BlockSpec((1, c3, tm), lambda b, i: (b, 0, i)),
            pl.BlockSpec((tm, D_MEM), lambda b, i: (b * nt + i, 0)),
        ],
        out_shape=[
            jax.ShapeDtypeStruct((B, c3, T), F32),
            jax.ShapeDtypeStruct((N, D_MEM), BF16),
        ],
        compiler_params=_cp("parallel", "parallel"),
        name="hy_inproj",
    )(h, g, wmt, wcq)


def _filt_mlp_kernel(feats_ref, w1t_ref, b1_ref, w2t_ref, b2_ref, fr_ref, h2_ref):
    fr = fr_ref[...]
    a = jnp.dot(w1t_ref[...], feats_ref[...], precision=HI, preferred_element_type=F32)
    h1 = jnp.sin(fr * (a + b1_ref[...]))
    a = jnp.dot(w2t_ref[...], h1, precision=HI, preferred_element_type=F32)
    h2_ref[...] = jnp.sin(fr * (a + b2_ref[...]))


def _filt_taps_kernel(h2_ref, w3t_ref, t_ref, dl_ref, out_ref):
    dec = jnp.exp(-t_ref[...] * dl_ref[...])
    hf = jnp.dot(w3t_ref[0, 0], h2_ref[...], precision=HI, preferred_element_type=F32) * dec
    hb = jnp.dot(w3t_ref[0, 1], h2_ref[...], precision=HI, preferred_element_type=F32) * dec
    nrm = jnp.sum(jnp.abs(hf) + jnp.abs(hb), axis=-1, keepdims=True) + 1e-6
    inv = 1.0 / nrm
    out_ref[0, 0] = hf * inv
    out_ref[0, 1] = hb * inv


def _fwd_fft(x3, f1s, twr, twi, wr, wi, cb, n1k):
    rhs = jnp.concatenate([x3[c].astype(BF16) for c in range(cb)], axis=1)
    a = _dot(f1s, rhs)
    ar, ai = a[:n1k], a[n1k:]
    tr = (ar * twr - ai * twi).astype(BF16)
    ti = (ar * twi + ai * twr).astype(BF16)
    lr = jnp.concatenate([tr[:, c * LANES:(c + 1) * LANES] for c in range(cb)], axis=0)
    li = jnp.concatenate([ti[:, c * LANES:(c + 1) * LANES] for c in range(cb)], axis=0)
    return _dot(lr, wr) + _dot(li, wi)


def _inv_fft(y, minv, twr, twi, g1, cb, n1k):
    b = _dot(y.astype(BF16), minv)
    br, bi = b[:, :LANES], b[:, LANES:]
    pr = (br * twr + bi * twi).astype(BF16)
    pi = (bi * twr - br * twi).astype(BF16)
    top = jnp.concatenate([pr[c * n1k:(c + 1) * n1k] for c in range(cb)], axis=1)
    bot = jnp.concatenate([pi[c * n1k:(c + 1) * n1k] for c in range(cb)], axis=1)
    rhs = jnp.concatenate([top, bot], axis=0)
    return _dot(g1, rhs)


def _filt_fft_kernel(taps_ref, f1s_ref, twr_ref, twi_ref, wr_ref, wi_ref, k_ref, *, cb, n1k):
    args = (f1s_ref[...], twr_ref[...], twi_ref[...], wr_ref[...], wi_ref[...], cb, n1k)
    xf = _fwd_fft(taps_ref[0, 0], *args)
    xb = _fwd_fft(taps_ref[0, 1], *args)
    k_ref[0] = jnp.concatenate([xf[:, :LANES] + xb[:, :LANES], xf[:, LANES:] - xb[:, LANES:]], axis=1)


def _hyena_filters_fft(p, T, cb):
    feats, t_row = _filter_feats(T)
    C = p["w3t"].shape[2]
    tabs = _dft_tables(T, cb)
    n1k, n1 = 2 * T // LANES, T // LANES
    h2 = pl.pallas_call(
        _filt_mlp_kernel,
        out_shape=jax.ShapeDtypeStruct((HY_FILT, T), F32),
        compiler_params=pltpu.CompilerParams(vmem_limit_bytes=VMEM_LIMIT),
        name="hy_filt_mlp",
    )(jnp.asarray(feats), p["w1t"], p["b1"], p["w2t"], p["b2"], p["fr"])
    cbt = 64
    taps = pl.pallas_call(
        _filt_taps_kernel,
        grid=(2, C // cbt),
        in_specs=[
            pl.BlockSpec((HY_FILT, T), lambda o, c: (0, 0)),
            pl.BlockSpec((1, 2, cbt, HY_FILT), lambda o, c: (o, 0, c, 0)),
            pl.BlockSpec((1, T), lambda o, c: (0, 0)),
            pl.BlockSpec((cbt, 1), lambda o, c: (c, 0)),
        ],
        out_specs=pl.BlockSpec((1, 2, cbt, T), lambda o, c: (o, 0, c, 0)),
        out_shape=jax.ShapeDtypeStruct((2, 2, C, T), F32),
        compiler_params=_cp("parallel", "parallel"),
        name="hy_filt_taps",
    )(h2, p["w3t"], jnp.asarray(t_row), p["absdelta"])
    taps = taps.reshape(2, 2, C, n1, LANES)
    const = lambda shape: pl.BlockSpec(shape, lambda o, c: (0,) * len(shape))
    f32 = lambda k: jnp.asarray(tabs[k], F32)
    return pl.pallas_call(
        functools.partial(_filt_fft_kernel, cb=cb, n1k=n1k),
        grid=(2, C // cb),
        in_specs=[
            pl.BlockSpec((1, 2, cb, n1, LANES), lambda o, c: (o, 0, c, 0, 0)),
            const((2 * n1k, n1)), const((n1k, cb * LANES)), const((n1k, cb * LANES)),
            const((LANES, 2 * LANES)), const((LANES, 2 * LANES)),
        ],
        out_specs=pl.BlockSpec((1, cb * n1k, 2 * LANES), lambda o, c: (o, c, 0)),
        out_shape=jax.ShapeDtypeStruct((2, C * n1k, 2 * LANES), F32),
        compiler_params=_cp("parallel", "parallel"),
        name="hy_filt_fft",
    )(taps, f32("f1s").astype(BF16), f32("twr_l"), f32("twi_l"), f32("wr").astype(BF16),
      f32("wi").astype(BF16))


def _time_neighbours(x):
    rows = x.shape[0]
    lane = lax.broadcasted_iota(I32, x.shape, 1)
    r = pltpu.roll(x, 1, 1)
    rr = pltpu.roll(r, 1, 0)
    prev = jnp.where(lane == 0, rr, r)
    r2 = pltpu.roll(x, LANES - 1, 1)
    rr2 = pltpu.roll(r2, rows - 1, 0)
    nxt = jnp.where(lane == LANES - 1, rr2, r2)
    return prev, nxt


def _hy_conv_kernel(x1_ref, x2_ref, v_ref, sw1_ref, sw2_ref, swv_ref, skip_ref, k_ref,
                    f1s_ref, twrl_ref, twil_ref, wr_ref, wi_ref, minv_ref, twrr_ref, twir_ref,
                    g1_ref, o_ref, *, cb, n1k, n1):
    rows = cb * n1
    shape2 = (rows, LANES)
    row = lax.broadcasted_iota(I32, shape2, 0)
    lane = lax.broadcasted_iota(I32, shape2, 1)
    first = (lane == 0) & (row % n1 == 0)
    last = (lane == LANES - 1) & (row % n1 == n1 - 1)

    def sconv(x_ref, sw_ref):
        x = x_ref[0].reshape(shape2)
        prev, nxt = _time_neighbours(x)
        prev = jnp.where(first, 0.0, prev)
        nxt = jnp.where(last, 0.0, nxt)
        w = [jnp.broadcast_to(sw_ref[j], (cb, n1, LANES)).reshape(shape2) for j in range(3)]
        return prev * w[0] + x * w[1] + nxt * w[2]

    z = sconv(v_ref, swv_ref)
    gates = (sconv(x1_ref, sw1_ref), sconv(x2_ref, sw2_ref))
    f1s, wr, wi = f1s_ref[...], wr_ref[...], wi_ref[...]
    for o in range(2):
        zf = _fwd_fft(z.reshape(cb, n1, LANES), f1s, twrl_ref[...], twil_ref[...], wr, wi, cb, n1k)
        kk = k_ref[o]
        zr, zi = zf[:, :LANES], zf[:, LANES:]
        kr, ki = kk[:, :LANES], kk[:, LANES:]
        y = jnp.concatenate([zr * kr - zi * ki, zr * ki + zi * kr], axis=1)
        conv = _inv_fft(y, minv_ref[...], twrr_ref[...], twir_ref[...], g1_ref[...], cb, n1k)
        conv = jnp.concatenate([conv[:, c * LANES:(c + 1) * LANES] for c in range(cb)], axis=0)
        skip = jnp.broadcast_to(skip_ref[o], (cb, n1, LANES)).reshape(shape2)
        z = gates[o] * (conv + skip * z)
    o_ref[0] = z.reshape(cb, n1, LANES)


def _hy_conv(ut4, sw, skip, kfft, B, T, C, cb):
    n1k, n1 = 2 * T // LANES, T // LANES
    tabs = _dft_tables(T, cb)
    nct = C // cb
    bf = lambda k: jnp.asarray(tabs[k], F32).astype(BF16)
    f32 = lambda k: jnp.asarray(tabs[k], F32)
    const = lambda shape: pl.BlockSpec(shape, lambda c, b: (0,) * len(shape))
    ublk = lambda s: pl.BlockSpec((1, cb, n1, LANES), lambda c, b, s=s: (b, s * nct + c, 0, 0))
    wblk = lambda s: pl.BlockSpec((3, cb, 1, LANES), lambda c, b, s=s: (0, s * nct + c, 0, 0))
    return pl.pallas_call(
        functools.partial(_hy_conv_kernel, cb=cb, n1k=n1k, n1=n1),
        grid=(nct, B),
        in_specs=[
            ublk(0), ublk(1), ublk(2), wblk(0), wblk(1), wblk(2),
            pl.BlockSpec((2, cb, 1, LANES), lambda c, b: (0, c, 0, 0)),
            pl.BlockSpec((2, cb * n1k, 2 * LANES), lambda c, b: (0, c, 0)),
            const((2 * n1k, n1)), const((n1k, cb * LANES)), const((n1k, cb * LANES)),
            const((LANES, 2 * LANES)), const((LANES, 2 * LANES)), const((2 * LANES, 2 * LANES)),
            const((cb * n1k, LANES)), const((cb * n1k, LANES)), const((n1, 2 * n1k)),
        ],
        out_specs=pl.BlockSpec((1, cb, n1, LANES), lambda c, b: (b, c, 0, 0)),
        out_shape=jax.ShapeDtypeStruct((B, C, n1, LANES), F32),
        compiler_params=_cp("parallel", "arbitrary"),
        name="hy_conv",
    )(ut4, ut4, ut4, sw, sw, sw, skip, kfft,
      bf("f1s"), f32("twr_l"), f32("twi_l"), bf("wr"), bf("wi"), bf("minv"),
      f32("twr_r"), f32("twi_r"), bf("g1"))


def _head_norm(x, bd, g):
    x2 = x * x
    hi = x2.astype(BF16)
    lo = (x2 - hi.astype(F32)).astype(BF16)
    ss = _dot(hi, bd) + _dot(lo, bd)
    return x * lax.rsqrt(ss * (1.0 / HEAD_DIM) + NORM_EPS) * g


def _rope(x, cos2, sin2):
    lane = lax.broadcasted_iota(I32, cos2.shape, 1)
    low = (lane // (ROPE_AXIS_DIM // 2)) % 2 == 0
    out = []
    for c in range(x.shape[1] // LANES):
        xc = x[:, c * LANES:(c + 1) * LANES]
        up = pltpu.roll(xc, LANES - 16, 1)
        dn = pltpu.roll(xc, 16, 1)
        out.append(xc * cos2 + jnp.where(low, up, dn) * sin2)
    return jnp.concatenate(out, axis=1)


def _at_inproj_kernel(h_ref, g_ref, w_ref, bdq_ref, bdk_ref, gq_ref, gk_ref, cos_ref, sin_ref,
                      qt_ref, kz_ref, vt_ref, cq_ref, *, dq, dk):
    xn = _rms(h_ref[...], g_ref[...]).astype(BF16)
    proj = _dot(xn, w_ref[...])
    q, k = proj[:, :dq], proj[:, dq:dq + dk]
    v, cq = proj[:, dq + dk:dq + 2 * dk], proj[:, dq + 2 * dk:]
    cos2, sin2 = cos_ref[...], sin_ref[...]
    qr = _rope(_head_norm(q, bdq_ref[...], gq_ref[...]), cos2, sin2)
    qt = (qr * (HEAD_DIM ** -0.5 * LOG2E)).T.astype(BF16)
    for p in range(dq // LANES):
        qt_ref[0, p] = qt[p * LANES:(p + 1) * LANES]
    kr = _rope(_head_norm(k, bdk_ref[...], gk_ref[...]), cos2, sin2)
    vt = v.T.astype(BF16)
    zv = jnp.zeros((HEAD_DIM, vt.shape[1]), BF16)
    lane = lax.broadcasted_iota(I32, (k.shape[0], LANES), 1)
    for kv in range(N_KV_HEADS):
        rows = vt[kv * HEAD_DIM:(kv + 1) * HEAD_DIM]
        vt_ref[0, kv, 0] = jnp.concatenate([rows, zv], axis=0)
        vt_ref[0, kv, 1] = jnp.concatenate([zv, rows], axis=0)
        pair = kr[:, (kv // 2) * LANES:(kv // 2 + 1) * LANES]
        own = jnp.where((lane < HEAD_DIM) == (kv % 2 == 0), pair, 0.0)
        other = pltpu.roll(own, HEAD_DIM, 1)
        lo, hi = (own, other) if kv % 2 == 0 else (other, own)
        kz_ref[0, kv, 0] = lo.astype(BF16)
        kz_ref[0, kv, 1] = hi.astype(BF16)
    cq_ref[...] = cq.astype(BF16)


def _at_inproj(h, g, w, gq, gk, B, T, tm):
    N, D = h.shape
    dq, dk = GQA_GROUP * N_KV_HEADS * HEAD_DIM, N_KV_HEADS * HEAD_DIM
    nt = T // tm
    cos2, sin2 = _rope_tables(T)
    bd = lambda n: jnp.asarray(np.kron(np.eye(n // HEAD_DIM), np.ones((HEAD_DIM, HEAD_DIM))), F32).astype(BF16)
    const = lambda shape: pl.BlockSpec(shape, lambda b, i: (0,) * len(shape))
    return pl.pallas_call(
        functools.partial(_at_inproj_kernel, dq=dq, dk=dk),
        grid=(B, nt),
        in_specs=[
            pl.BlockSpec((tm, D), lambda b, i: (b * nt + i, 0)),
            const((1, D)), const(w.shape), const((dq, dq)), const((dk, dk)),
            const((1, dq)), const((1, dk)),
            pl.BlockSpec((tm, LANES), lambda b, i: (i, 0)),
            pl.BlockSpec((tm, LANES), lambda b, i: (i, 0)),
        ],
        out_specs=[
            pl.BlockSpec((1, dq // LANES, LANES, tm), lambda b, i: (b, 0, 0, i)),
            pl.BlockSpec((1, N_KV_HEADS, 2, tm, LANES), lambda b, i: (b, 0, 0, i, 0)),
            pl.BlockSpec((1, N_KV_HEADS, 2, LANES, tm), lambda b, i: (b, 0, 0, 0, i)),
            pl.BlockSpec((tm, D_MEM), lambda b, i: (b * nt + i, 0)),
        ],
        out_shape=[
            jax.ShapeDtypeStruct((B, dq // LANES, LANES, T), BF16),
            jax.ShapeDtypeStruct((B, N_KV_HEADS, 2, T, LANES), BF16),
            jax.ShapeDtypeStruct((B, N_KV_HEADS, 2, LANES, T), BF16),
            jax.ShapeDtypeStruct((N, D_MEM), BF16),
        ],
        compiler_params=_cp("parallel", "parallel"),
        name="at_inproj",
    )(h, g, w, bd(dq), bd(dk), gq, gk, jnp.asarray(cos2), jnp.asarray(sin2))


def _flash_kernel(qt_ref, ka_ref, kb_ref, vta_ref, vtb_ref, o_ref, m_sc, l_sc, acc_sc, s_sc, p_sc):
    ki = pl.program_id(3)

    @pl.when(ki == 0)
    def _():
        m_sc[...] = jnp.full(m_sc.shape, -jnp.inf, F32)
        l_sc[...] = jnp.zeros(l_sc.shape, F32)
        acc_sc[...] = jnp.zeros(acc_sc.shape, F32)

    qt = qt_ref[0, 0]
    tk, tq = s_sc.shape[1], s_sc.shape[2]
    sub = 8
    ck = 2 * sub
    pv, alphas = [], []
    for idx, k_ref in enumerate((ka_ref, kb_ref)):
        s_sc[idx] = _dot(k_ref[0, 0, 0], qt)
    for idx, vt_ref in enumerate((vta_ref, vtb_ref)):
        mx = s_sc[idx, 0:sub, :]
        for c in range(1, tk // sub):
            mx = jnp.maximum(mx, s_sc[idx, c * sub:(c + 1) * sub, :])
        m_prev = m_sc[idx]
        m_new = jnp.maximum(m_prev, jnp.max(mx, axis=0, keepdims=True))
        alpha = jnp.exp2(m_prev - m_new)
        lsum = jnp.zeros((sub, tq), F32)
        for c in range(tk // ck):
            p = jnp.exp2(s_sc[idx, c * ck:(c + 1) * ck, :] - m_new)
            lsum = lsum + p[:sub] + p[sub:]
            p_sc[idx, c * ck:(c + 1) * ck, :] = p.astype(BF16)
        l_sc[idx] = alpha * l_sc[idx] + jnp.sum(lsum, axis=0, keepdims=True)
        m_sc[idx] = m_new
        pv.append(_dot(vt_ref[0, 0, 0], p_sc[idx]))
        alphas.append(alpha)
    row = lax.broadcasted_iota(I32, acc_sc.shape, 0)
    low = row < HEAD_DIM
    acc_sc[...] = acc_sc[...] * jnp.where(low, alphas[0], alphas[1]) + pv[0] + pv[1]

    @pl.when(ki == pl.num_programs(3) - 1)
    def _():
        o_ref[0] = acc_sc[...] / jnp.where(low, l_sc[0], l_sc[1])


def _flash(qt, kz, vtz, B, T, tq, tk):
    npair = qt.shape[1]
    nq, nk = T // tq, T // tk
    kv_of = lambda p, j: (2 * p + j) // GQA_GROUP
    return pl.pallas_call(
        _flash_kernel,
        grid=(B, npair, nq, nk),
        in_specs=[
            pl.BlockSpec((1, 1, LANES, tq), lambda b, p, i, k: (b, p, 0, i)),
            pl.BlockSpec((1, 1, 1, tk, LANES), lambda b, p, i, k: (b, kv_of(p, 0), 0, k, 0)),
            pl.BlockSpec((1, 1, 1, tk, LANES), lambda b, p, i, k: (b, kv_of(p, 1), 1, k, 0)),
            pl.BlockSpec((1, 1, 1, LANES, tk), lambda b, p, i, k: (b, kv_of(p, 0), 0, 0, k)),
            pl.BlockSpec((1, 1, 1, LANES, tk), lambda b, p, i, k: (b, kv_of(p, 1), 1, 0, k)),
        ],
        out_specs=pl.BlockSpec((1, LANES, tq), lambda b, p, i, k: (b, p, i)),
        out_shape=jax.ShapeDtypeStruct((B, npair * LANES, T), F32),
        scratch_shapes=[
            pltpu.VMEM((2, 1, tq), F32), pltpu.VMEM((2, 1, tq), F32), pltpu.VMEM((LANES, tq), F32),
            pltpu.VMEM((2, tk, tq), F32), pltpu.VMEM((2, tk, tq), BF16),
        ],
        compiler_params=_cp("parallel", "parallel", "parallel", "arbitrary"),
        name="flash_gqa",
    )(qt, kz, kz, vtz, vtz)


def _memkv_kernel(mem_ref, g_ref, wkvt_ref, wv_ref, mkt_ref, mv_ref):
    mn = _rms(mem_ref[0], g_ref[...]).astype(BF16)
    kt = _dot_nt(wkvt_ref[...], mn) * (HEAD_DIM ** -0.5)
    v = _dot(mn, wv_ref[...])
    row = lax.broadcasted_iota(I32, kt.shape, 0)
    col = lax.broadcasted_iota(I32, v.shape, 1)
    for hd in range(MEM_HEADS):
        mkt_ref[0, hd] = jnp.where(row // HEAD_DIM == hd, kt, 0.0).astype(BF16)
        mv_ref[0, hd] = jnp.where(col // HEAD_DIM == hd, v, 0.0).astype(BF16)


def _memkv(mem, g, wkt, wv):
    B, M, D = mem.shape
    return pl.pallas_call(
        _memkv_kernel,
        grid=(B,),
        in_specs=[
            pl.BlockSpec((1, M, D), lambda b: (b, 0, 0)),
            pl.BlockSpec((1, D), lambda b: (0, 0)),
            pl.BlockSpec((D_MEM, D), lambda b: (0, 0)),
            pl.BlockSpec((D, D_MEM), lambda b: (0, 0)),
        ],
        out_specs=[
            pl.BlockSpec((1, MEM_HEADS, D_MEM, M), lambda b: (b, 0, 0, 0)),
            pl.BlockSpec((1, MEM_HEADS, M, D_MEM), lambda b: (b, 0, 0, 0)),
        ],
        out_shape=[
            jax.ShapeDtypeStruct((B, MEM_HEADS, D_MEM, M), BF16),
            jax.ShapeDtypeStruct((B, MEM_HEADS, M, D_MEM), BF16),
        ],
        compiler_params=_cp("parallel"),
        name="mem_kv",
    )(mem, g, wkt, wv)


def _cross_attn(cq, mkt_ref, mv_ref):
    acc = jnp.zeros(cq.shape, F32)
    for hd in range(MEM_HEADS):
        s = _dot(cq, mkt_ref[0, hd])
        p = jnp.exp(s - jnp.max(s, axis=-1, keepdims=True))
        p = p / jnp.sum(p, axis=-1, keepdims=True)
        acc = acc + _dot(p.astype(BF16), mv_ref[0, hd])
    return acc


def _outproj_kernel(h_ref, main_ref, cq_ref, mkt_ref, mv_ref, wm_ref, wc_ref, o_ref, *, channel_major):
    if channel_major:
        main = main_ref[0].T.astype(BF16)
    else:
        main = main_ref[...]
    cross = _cross_attn(cq_ref[...], mkt_ref, mv_ref).astype(BF16)
    o_ref[...] = h_ref[...] + _dot(main, wm_ref[...]) + _dot(cross, wc_ref[...])


def _outproj(h, main, cq, mkt, mv, wm, wc, B, T, tm, channel_major):
    N, D = h.shape
    C = wm.shape[0]
    M = mkt.shape[-1]
    nt = T // tm
    if channel_major:
        mspec = pl.BlockSpec((1, C, tm), lambda b, i: (b, 0, i))
    else:
        mspec = pl.BlockSpec((tm, C), lambda b, i: (b * nt + i, 0))
    return pl.pallas_call(
        functools.partial(_outproj_kernel, channel_major=channel_major),
        grid=(B, nt),
        in_specs=[
            pl.BlockSpec((tm, D), lambda b, i: (b * nt + i, 0)),
            mspec,
            pl.BlockSpec((tm, D_MEM), lambda b, i: (b * nt + i, 0)),
            pl.BlockSpec((1, MEM_HEADS, D_MEM, M), lambda b, i: (b, 0, 0, 0)),
            pl.BlockSpec((1, MEM_HEADS, M, D_MEM), lambda b, i: (b, 0, 0, 0)),
            pl.BlockSpec((C, D), lambda b, i: (0, 0)),
            pl.BlockSpec((D_MEM, D), lambda b, i: (0, 0)),
        ],
        out_specs=pl.BlockSpec((tm, D), lambda b, i: (b * nt + i, 0)),
        out_shape=jax.ShapeDtypeStruct((N, D), F32),
        compiler_params=_cp("parallel", "parallel"),
        name="outproj_cm" if channel_major else "outproj_rm",
    )(h, main, cq, mkt, mv, wm, wc)


def _router_kernel(h_ref, g_ref, wr_ref, wrt_ref, xn_ref, gsp_ref, afft_ref):
    xn = _rms(h_ref[...], g_ref[...]).astype(BF16)
    xn_ref[...] = xn
    lg = _dot(xn, wr_ref[...])
    lane = lax.broadcasted_iota(I32, lg.shape, 1)
    lg = jnp.where(lane < N_EXPERTS, lg, -jnp.inf)
    p = jnp.exp(lg - jnp.max(lg, axis=-1, keepdims=True))
    aff = p / jnp.sum(p, axis=-1, keepdims=True)
    hi = aff.astype(BF16).astype(F32)
    mid = (aff - hi).astype(BF16).astype(F32)
    lo = (aff - hi - mid).astype(BF16).astype(F32)
    gsp_ref[...] = (hi + pltpu.roll(mid, N_EXPERTS, 1) + pltpu.roll(lo, 2 * N_EXPERTS, 1)).astype(BF16)
    lt = _dot_nt(wrt_ref[...], xn)
    pt = jnp.exp(lt - jnp.max(lt, axis=0, keepdims=True))
    pt = pt / jnp.sum(pt, axis=0, keepdims=True)
    for j in range(lt.shape[1] // LANES):
        afft_ref[0, j] = pt[:, j * LANES:(j + 1) * LANES]


def _router(h, g, wr, wrt, B, T, tm):
    N, D = h.shape
    nt = T // tm
    return pl.pallas_call(
        _router_kernel,
        grid=(B, nt),
        in_specs=[
            pl.BlockSpec((tm, D), lambda b, i: (b * nt + i, 0)),
            pl.BlockSpec((1, D), lambda b, i: (0, 0)),
            pl.BlockSpec((D, LANES), lambda b, i: (0, 0)),
            pl.BlockSpec((N_EXPERTS, D), lambda b, i: (0, 0)),
        ],
        out_specs=[
            pl.BlockSpec((tm, D), lambda b, i: (b * nt + i, 0)),
            pl.BlockSpec((tm, LANES), lambda b, i: (b * nt + i, 0)),
            pl.BlockSpec((1, tm // LANES, N_EXPERTS, LANES), lambda b, i: (b, i, 0, 0)),
        ],
        out_shape=[
            jax.ShapeDtypeStruct((N, D), BF16),
            jax.ShapeDtypeStruct((N, LANES), BF16),
            jax.ShapeDtypeStruct((B, T // LANES, N_EXPERTS, LANES), F32),
        ],
        compiler_params=_cp("parallel", "parallel"),
        name="moe_router",
    )(h, g, wr, wrt)


def _topk_kernel(aff_ref, mall_ref, mw_ref, mb_ref, ma_ref, tri_ref, pos_ref, aoff_ref, pad_ref,
                 *, cap, nj):
    E = N_EXPERTS
    aff3 = aff_ref[0]

    def count(mask3):
        per = jnp.sum(mask3.astype(F32), axis=0)
        return jnp.broadcast_to(jnp.sum(per, axis=-1, keepdims=True), (E, LANES))

    def step(i, thr):
        cand = thr | lax.shift_left(jnp.int32(1), 30 - i)
        ok = count(aff3 >= pltpu.bitcast(cand, F32)[None]) >= cap
        return jnp.where(ok, cand, thr)

    thr = lax.fori_loop(0, 31, step, jnp.zeros((E, LANES), I32))
    thr = pltpu.bitcast(thr, F32)
    gt3 = aff3 > thr[None]
    eq3 = aff3 == thr[None]
    need = cap - count(gt3)

    ones = jnp.ones((LANES, LANES), BF16)
    tri = tri_ref[...]

    def prefix(mask2):
        mb = mask2.astype(BF16)
        incl = _dot(mb, tri)
        tot = _dot(mb, ones)
        return incl - mask2, tot

    eq2 = eq3.reshape(nj * E, LANES).astype(F32)
    ex, tot = prefix(eq2)
    eq_rank = ex + _dot(mall_ref[...], tot.astype(BF16))
    need2 = jnp.broadcast_to(need[None], (nj, E, LANES)).reshape(nj * E, LANES)
    sel = jnp.where((gt3.reshape(nj * E, LANES)) | ((eq2 > 0) & (eq_rank < need2)), 1.0, 0.0)

    ex, tot = prefix(sel)
    totb = tot.astype(BF16)
    within = _dot(mw_ref[...], totb)
    cnt = _dot(mb_ref[...], totb)
    padded = (cnt.astype(I32) + (BF16_SUBLANES - 1)) & ~(BF16_SUBLANES - 1)
    aoff = _dot(ma_ref[...], padded.astype(F32).astype(BF16))
    pos = (aoff + within + ex).astype(I32)
    pos_ref[0] = jnp.where(sel > 0, pos, -1).reshape(nj, E, LANES)
    aoff_ref[0] = aoff.astype(I32).reshape(nj, E, LANES)
    pad_ref[0] = padded.reshape(nj, E, LANES)


def _topk(aff4, T, tb):
    B, nj = aff4.shape[0], aff4.shape[1]
    cap = EC_CAPACITY_FACTOR * T // N_EXPERTS
    tabs = [jnp.asarray(m, F32).astype(BF16) for m in _topk_tables(T, tb)]
    R = nj * N_EXPERTS
    blk = pl.BlockSpec((1, nj, N_EXPERTS, LANES), lambda b: (b, 0, 0, 0))
    const = lambda shape: pl.BlockSpec(shape, lambda b: (0,) * len(shape))
    out = jax.ShapeDtypeStruct((B, nj, N_EXPERTS, LANES), I32)
    return pl.pallas_call(
        functools.partial(_topk_kernel, cap=cap, nj=nj),
        grid=(B,),
        in_specs=[blk, const((R, R)), const((R, R)), const((R, R)), const((R, R)), const((LANES, LANES))],
        out_specs=[blk, blk, blk],
        out_shape=[out, out, out],
        compiler_params=_cp("parallel"),
        name="moe_topk",
    )(aff4, *tabs)


def _compress_kernel(aoff_ref, nch_ref, nmax_ref, xn_ref, gsp_ref, pos_ref, xg_ref, gs_ref, *, nblk, tb, W):
    b, d, blk = pl.program_id(0), pl.program_id(1), pl.program_id(2)

    @pl.when(blk == 0)
    def _():
        xg_ref[...] = jnp.zeros(xg_ref.shape, BF16)

    @pl.when((blk == 0) & (d == 0))
    def _():
        gs_ref[...] = jnp.zeros(gs_ref.shape, F32)

    rhs = xn_ref[...]
    base = (b * nblk + blk) * N_EXPERTS
    iota_s = lax.broadcasted_iota(I32, (W, LANES), 0)

    def chunk(c, carry):
        pieces = []
        for e in range(N_EXPERTS):
            a = aoff_ref[base + e] + c * W
            g = [pos_ref[0, jj, e:e + 1, :] - a == iota_s for jj in range(tb // LANES)]
            pieces.append(jnp.where(jnp.concatenate(g, axis=1), 1.0, 0.0).astype(BF16))
        lhs = jnp.concatenate(pieces, axis=0)
        res = _dot(lhs, rhs).astype(BF16)
        for e in range(N_EXPERTS):
            @pl.when(c < nch_ref[base + e])
            def _():
                a = pl.multiple_of(aoff_ref[base + e] + c * W, BF16_SUBLANES)
                xg_ref[0, e, pl.ds(a, W), :] = res[e * W:(e + 1) * W]

        @pl.when(d == 0)
        def _():
            resg = _dot(lhs, gsp_ref[...])
            for e in range(N_EXPERTS):
                @pl.when(c < nch_ref[base + e])
                def _():
                    a = pl.multiple_of(aoff_ref[base + e] + c * W, BF16_SUBLANES)
                    gs_ref[0, e, pl.ds(a, W), :] = resg[e * W:(e + 1) * W]
        return carry

    lax.fori_loop(0, nmax_ref[b * nblk + blk], chunk, 0)


def _compress(aoff, nch, nmax, xn, gsp, pos4, B, T, tb, dw, rows, W):
    N, D = xn.shape
    nblk = T // tb
    return pl.pallas_call(
        functools.partial(_compress_kernel, nblk=nblk, tb=tb, W=W),
        grid_spec=pltpu.PrefetchScalarGridSpec(
            num_scalar_prefetch=3,
            grid=(B, D // dw, nblk),
            in_specs=[
                pl.BlockSpec((tb, dw), lambda b, d, k, *_: (b * nblk + k, d)),
                pl.BlockSpec((tb, LANES), lambda b, d, k, *_: (b * nblk + k, 0)),
                pl.BlockSpec((1, tb // LANES, N_EXPERTS, LANES), lambda b, d, k, *_: (b, k, 0, 0)),
            ],
            out_specs=[
                pl.BlockSpec((1, N_EXPERTS, rows, dw), lambda b, d, k, *_: (b, 0, 0, d)),
                pl.BlockSpec((1, N_EXPERTS, rows, LANES), lambda b, d, k, *_: (b, 0, 0, 0)),
            ],
        ),
        out_shape=[
            jax.ShapeDtypeStruct((B, N_EXPERTS, rows, D), BF16),
            jax.ShapeDtypeStruct((B, N_EXPERTS, rows, LANES), F32),
        ],
        compiler_params=_cp("arbitrary", "arbitrary", "arbitrary"),
        name="moe_compress",
    )(aoff, nch, nmax, xn, gsp, pos4)


def _ffn_kernel(x_ref, gs_ref, wg_ref, wu_ref, wd_ref, y_ref, *, live):
    e = pl.program_id(0)
    x = x_ref[0, 0, :live]
    g = _dot(x, wg_ref[0, 0].astype(BF16))
    u = _dot(x, wu_ref[0, 0].astype(BF16))
    hid = (g * (1.0 / (1.0 + jnp.exp(-g))) * u).astype(BF16)
    gs = gs_ref[0, 0, :live]
    lane = lax.broadcasted_iota(I32, gs.shape, 1)
    mine = (lane == e) | (lane == e + N_EXPERTS) | (lane == e + 2 * N_EXPERTS)
    gate = jnp.sum(jnp.where(mine, gs, 0.0), axis=-1, keepdims=True)
    y_ref[0, 0, :live] = (_dot(hid, wd_ref[0, 0].astype(BF16)) * gate).astype(BF16)
    y_ref[0, 0, live:] = jnp.zeros((y_ref.shape[2] - live, y_ref.shape[3]), BF16)


def _ffn(xg, gs, wg, wu, wd, layer, live):
    B, E, rows, D = xg.shape
    F = wg.shape[-1]
    return pl.pallas_call(
        functools.partial(_ffn_kernel, live=live),
        grid=(E, B),
        in_specs=[
            pl.BlockSpec((1, 1, rows, D), lambda e, b: (b, e, 0, 0)),
            pl.BlockSpec((1, 1, rows, LANES), lambda e, b: (b, e, 0, 0)),
            pl.BlockSpec((1, 1, D, F), lambda e, b: (layer, e, 0, 0)),
            pl.BlockSpec((1, 1, D, F), lambda e, b: (layer, e, 0, 0)),
            pl.BlockSpec((1, 1, F, D), lambda e, b: (layer, e, 0, 0)),
        ],
        out_specs=pl.BlockSpec((1, 1, rows, D), lambda e, b: (b, e, 0, 0)),
        out_shape=jax.ShapeDtypeStruct((B, E, rows, D), BF16),
        compiler_params=_cp("parallel", "arbitrary"),
        name="moe_ffn",
    )(xg, gs, wg, wu, wd)


def _expand_kernel(aoff_ref, nch_ref, nmax_ref, h_ref, post_ref, y_ref, o_ref, *, nblk, tb, W, rows):
    b, blk = pl.program_id(0), pl.program_id(2)
    base = (b * nblk + blk) * N_EXPERTS
    pt = post_ref[...]
    iota_l = lax.broadcasted_iota(I32, (tb, W), 1)
    rel = [jnp.broadcast_to(pt[:, e:e + 1], (tb, W)) - iota_l for e in range(N_EXPERTS)]

    def chunk(c, acc):
        cols, wins = [], []
        for e in range(N_EXPERTS):
            a = aoff_ref[base + e] + c * W
            hit = (rel[e] == a) & (c < nch_ref[base + e])
            cols.append(jnp.where(hit, 1.0, 0.0).astype(BF16))
            a_in = pl.multiple_of(jnp.minimum(a, rows - W), BF16_SUBLANES)
            wins.append(y_ref[0, e, pl.ds(a_in, W), :])
        return acc + _dot(jnp.concatenate(cols, axis=1), jnp.concatenate(wins, axis=0))

    acc = lax.fori_loop(0, nmax_ref[b * nblk + blk], chunk, jnp.zeros(o_ref.shape, F32))
    o_ref[...] = h_ref[...] + acc


def _expand(aoff, nch, nmax, h, post, y, B, T, tb, dw, W):
    N, D = h.shape
    rows = y.shape[2]
    nblk = T // tb
    return pl.pallas_call(
        functools.partial(_expand_kernel, nblk=nblk, tb=tb, W=W, rows=rows),
        grid_spec=pltpu.PrefetchScalarGridSpec(
            num_scalar_prefetch=3,
            grid=(B, D // dw, nblk),
            in_specs=[
                pl.BlockSpec((tb, dw), lambda b, d, k, *_: (b * nblk + k, d)),
                pl.BlockSpec((tb, N_EXPERTS), lambda b, d, k, *_: (b * nblk + k, 0)),
                pl.BlockSpec((1, N_EXPERTS, rows, dw), lambda b, d, k, *_: (b, 0, 0, d),
                             pipeline_mode=pl.Buffered(1)),
            ],
            out_specs=pl.BlockSpec((tb, dw), lambda b, d, k, *_: (b * nblk + k, d)),
        ),
        out_shape=jax.ShapeDtypeStruct((N, D), F32),
        compiler_params=_cp("parallel", "parallel", "arbitrary"),
        name="moe_expand",
    )(aoff, nch, nmax, h, post, y)


def _moe(h, g, wr, wg, wu, wd, layer, B, T, tm):
    N, D = h.shape
    tb = min(512, T)
    W = LANES
    nblk = T // tb
    cap = EC_CAPACITY_FACTOR * T // N_EXPERTS
    rows = -(-(cap + nblk * (BF16_SUBLANES - 1)) // W) * W + W
    dw = 256
    wr_pad = jnp.zeros((D, LANES), BF16).at[:, :N_EXPERTS].set(wr.astype(BF16))
    xn, gsp, aff4 = _router(h, g, wr_pad, wr.T.astype(BF16), B, T, tm)
    pos4, aoff4, pad4 = _topk(aff4, T, tb)
    per = tb // LANES
    aoff = aoff4[:, ::per, :, 0].reshape(-1)
    nch4 = (pad4[:, ::per, :, 0] + (W - 1)) // W
    nch = nch4.reshape(-1)
    nmax = jnp.max(nch4, axis=-1).reshape(-1)
    post = jnp.transpose(pos4, (0, 1, 3, 2)).reshape(N, N_EXPERTS)
    xg, gs = _compress(aoff, nch, nmax, xn, gsp, pos4, B, T, tb, dw, rows, W)
    y = _ffn(xg, gs, wg, wu, wd, layer, rows - W)
    return _expand(aoff, nch, nmax, h, post, y, B, T, tb, 2 * dw, W)


def _final_norm_kernel(h_ref, g_ref, o_ref):
    o_ref[...] = _rms(h_ref[...], g_ref[...])


def _final_norm(h, g, tm):
    N, D = h.shape
    return pl.pallas_call(
        _final_norm_kernel,
        grid=(N // tm,),
        in_specs=[pl.BlockSpec((tm, D), lambda i: (i, 0)), pl.BlockSpec((1, D), lambda i: (0, 0))],
        out_specs=pl.BlockSpec((tm, D), lambda i: (i, 0)),
        out_shape=jax.ShapeDtypeStruct((N, D), F32),
        compiler_params=_cp("parallel"),
        name="final_norm",
    )(h, g)


def kernel(x, mem, mix_norm_g, ffn_norm_g, mem_norm_g, final_norm_g, w_mem_kv, w_out, hy_w_in, hy_short_w, hy_filt_w1, hy_filt_b1, hy_filt_w2, hy_filt_b2, hy_filt_w3, hy_filt_freq, hy_skip, at_w_in, at_q_norm_g, at_k_norm_g, router_w, exp_w_gate, exp_w_up, exp_w_down):
    B, T, D = x.shape
    depth = mix_norm_g.shape[0]
    C = hy_skip.shape[-1]
    tm = min(512, T)
    cb = 16
    n1 = T // LANES
    h = x.reshape(B * T, D)
    row = lambda v: v.reshape(1, -1).astype(F32)
    lanes = lambda v: jnp.broadcast_to(v[..., None, None], v.shape + (1, LANES)).astype(F32)
    max_decay = math.log(HY_DECAY_TARGET) / HY_FAST_PCT
    min_decay = math.log(HY_DECAY_TARGET) / HY_SLOW_PCT
    absdelta = jnp.asarray(np.abs(np.linspace(min_decay, max_decay, C)).astype(np.float32)).reshape(C, 1)

    for i in range(depth):
        j = i // 2
        wo = w_out[i].astype(BF16)
        wkv = w_mem_kv[i].astype(BF16)
        mkt, mv = _memkv(mem, row(mem_norm_g), wkv[:, :D_MEM].T, wkv[:, D_MEM:])
        if i % 2 == 0:
            w_in = hy_w_in[j].astype(BF16)
            ut, cq = _hy_inproj(h, row(mix_norm_g[i]), w_in[:, :3 * C].T, w_in[:, 3 * C:], B, T, tm)
            filt = dict(
                w1t=jnp.zeros((HY_FILT, 40), F32).at[:, :hy_filt_w1.shape[1]].set(hy_filt_w1[j].T),
                b1=hy_filt_b1[j].reshape(-1, 1), w2t=hy_filt_w2[j].T, b2=hy_filt_b2[j].reshape(-1, 1),
                fr=hy_filt_freq[j].reshape(-1, 1),
                w3t=hy_filt_w3[j].T.reshape(2, 2, C, HY_FILT), absdelta=absdelta,
            )
            kfft = _hyena_filters_fft(filt, T, cb)
            zt = _hy_conv(ut.reshape(B, 3 * C, n1, LANES), lanes(hy_short_w[j]), lanes(hy_skip[j]),
                          kfft, B, T, C, cb)
            h = _outproj(h, zt.reshape(B, C, T), cq, mkt, mv, wo[:C], wo[C:], B, T, tm, True)
        else:
            rep = lambda v, n: jnp.tile(v.astype(F32), n).reshape(1, -1)
            q, ktz, vz, cq = _at_inproj(h, row(mix_norm_g[i]), at_w_in[j].astype(BF16),
                                        rep(at_q_norm_g[j], GQA_GROUP * N_KV_HEADS),
                                        rep(at_k_norm_g[j], N_KV_HEADS), B, T, tm)
            main_t = _flash(q, ktz, vz, B, T, min(2048, T), min(1024, T))
            h = _outproj(h, main_t, cq, mkt, mv, wo[:C], wo[C:], B, T, tm, True)
        h = _moe(h, row(ffn_norm_g[i]), router_w[i], exp_w_gate, exp_w_up, exp_w_down, i, B, T, tm)
    return _final_norm(h, row(final_norm_g), tm).reshape(B, T, D)
```

```python
import functools
import math

import numpy as np
import jax
import jax.numpy as jnp
from jax import lax
from jax.experimental import pallas as pl
from jax.experimental.pallas import tpu as pltpu

F32 = jnp.float32
BF16 = jnp.bfloat16
I32 = jnp.int32

HEAD_DIM = 64
MEM_HEADS = 4
D_MEM = MEM_HEADS * HEAD_DIM
N_KV_HEADS = 4
GQA_GROUP = 3
GRID_W = 64
ROPE_THETA = 10000.0
ROPE_AXIS_DIM = HEAD_DIM // 2
HY_BANDS = 16
HY_FILT = 64
HY_DECAY_TARGET = 1e-2
HY_FAST_PCT = 0.3
HY_SLOW_PCT = 1.5
N_EXPERTS = 16
EC_CAPACITY_FACTOR = 2
NORM_EPS = 1e-6

LANES = 128
BF16_SUBLANES = 16
VMEM_LIMIT = 56 * 1024 * 1024
HI = lax.Precision.HIGHEST
LOG2E = 1.4426950408889634


def _cp(*sem):
    return pltpu.CompilerParams(dimension_semantics=sem, vmem_limit_bytes=VMEM_LIMIT)


def _rms(x, g):
    ms = jnp.mean(x * x, axis=-1, keepdims=True)
    return x * lax.rsqrt(ms + NORM_EPS) * g


def _dot(a, b):
    return jnp.dot(a, b, preferred_element_type=F32)


def _dot_nt(a, b):
    return lax.dot_general(a, b, (((1,), (1,)), ((), ())), preferred_element_type=F32)


@functools.lru_cache(maxsize=None)
def _dft_tables(T, cb):
    nf = 2 * T
    n1k = nf // LANES
    n1 = T // LANES
    k1 = np.arange(n1k)[:, None]
    a1 = 2 * np.pi * k1 * np.arange(n1)[None, :] / n1k
    f1s = np.concatenate([np.cos(a1), -np.sin(a1)], axis=0)
    a1f = 2 * np.pi * k1 * np.arange(n1k)[None, :] / n1k
    f1full = np.concatenate([np.cos(a1f), -np.sin(a1f)], axis=0)
    tw = 2 * np.pi * k1 * np.arange(LANES)[None, :] / nf
    twr, twi = np.cos(tw), -np.sin(tw)
    a2 = 2 * np.pi * np.arange(LANES)[:, None] * np.arange(LANES)[None, :] / LANES
    cr, ci = np.cos(a2), -np.sin(a2)
    w2 = np.block([[cr, ci], [-ci, cr]])
    minv = np.block([[cr, -ci], [ci, cr]])
    a3 = 2 * np.pi * np.arange(n1)[:, None] * np.arange(n1k)[None, :] / n1k
    g1 = np.concatenate([np.cos(a3), -np.sin(a3)], axis=1) / nf
    return dict(
        f1s=f1s, f1full=f1full, w2=w2, minv=minv, g1=g1,
        twr_l=np.tile(twr, (1, cb)), twi_l=np.tile(twi, (1, cb)),
        twr_r=np.tile(twr, (cb, 1)), twi_r=np.tile(twi, (cb, 1)),
    )


@functools.lru_cache(maxsize=None)
def _filter_feats(T):
    t = np.linspace(0.0, 1.0, T)[None, :]
    w = (2.0 * np.pi) * np.arange(T)[None, :] / T
    bands = np.linspace(1e-4, HY_BANDS - 1, HY_BANDS)[:, None]
    feats = np.concatenate([t, np.cos(bands * w), -np.sin(bands * w)], axis=0)
    pad = np.zeros((40 - feats.shape[0], T))
    feats = np.concatenate([feats, pad], axis=0)
    rev = (T - np.arange(T)) % T
    return (np.stack([feats, feats[:, rev]]).astype(np.float32),
            np.stack([t, t[:, rev]]).astype(np.float32))


@functools.lru_cache(maxsize=None)
def _rope_tables(T):
    rows = T // GRID_W
    pos_row = np.repeat(np.arange(rows), GRID_W).astype(np.float64)
    pos_col = np.tile(np.arange(GRID_W), rows).astype(np.float64)
    inv = 1.0 / (ROPE_THETA ** (np.arange(0, ROPE_AXIS_DIM, 2, dtype=np.float64) / ROPE_AXIS_DIM))
    lane = np.arange(LANES)
    d = lane % HEAD_DIM
    axis = d // ROPE_AXIS_DIM
    half = (d // (ROPE_AXIS_DIM // 2)) % 2
    f = d % (ROPE_AXIS_DIM // 2)
    pos = np.where(axis[None, :] == 0, pos_row[:, None], pos_col[:, None])
    ang = pos * inv[f][None, :]
    cos2 = np.cos(ang)
    sin2 = np.where(half[None, :] == 0, -np.sin(ang), np.sin(ang))
    return cos2.astype(np.float32), sin2.astype(np.float32)


@functools.lru_cache(maxsize=None)
def _spread_table(W):
    owner = np.arange(N_EXPERTS * W) // W
    hit = (np.arange(N_EXPERTS)[:, None] == owner[None, :]).astype(np.float32)
    return np.concatenate([32.0 * hit, hit], axis=0)


@functools.lru_cache(maxsize=None)
def _topk_tables(T, tb):
    nj = T // LANES
    per = tb // LANES
    r = np.arange(nj * N_EXPERTS)
    j, e = r // N_EXPERTS, r % N_EXPERTS
    same_e = e[:, None] == e[None, :]
    blk = j // per
    same_blk = blk[:, None] == blk[None, :]
    m_all = same_e & (j[None, :] < j[:, None])
    m_w = same_e & same_blk & (j[None, :] < j[:, None])
    m_b = same_e & same_blk
    m_a = same_e & (blk[None, :] < blk[:, None]) & ((j % per) == 0)[None, :]
    tri = np.triu(np.ones((LANES, LANES)))
    return tuple(np.asarray(m, np.float32) for m in (m_all, m_w, m_b, m_a, tri))


def _hy_inproj_kernel(h_ref, g_ref, wmt_ref, wcq_ref, ut_ref, cq_ref):
    xn = _rms(h_ref[...], g_ref[...]).astype(BF16)
    ut_ref[0] = _dot_nt(wmt_ref[...], xn)
    cq_ref[...] = _dot(xn, wcq_ref[...]).astype(BF16)


def _hy_inproj(h, g, wmt, wcq, B, T, tm):
    N, D = h.shape
    c3 = wmt.shape[0]
    nt = T // tm
    return pl.pallas_call(
        _hy_inproj_kernel,
        grid=(B, nt),
        in_specs=[
            pl.BlockSpec((tm, D), lambda b, i: (b * nt + i, 0)),
            pl.BlockSpec((1, D), lambda b, i: (0, 0)),
            pl.BlockSpec((c3, D), lambda b, i: (0, 0)),
            pl.BlockSpec((D, D_MEM), lambda b, i: (0, 0)),
        ],
        out_specs=[
            pl.BlockSpec((1, c3, tm), lambda b, i: (b, 0, i)),
            pl.BlockSpec((tm, D_MEM), lambda b, i: (b * nt + i, 0)),
        ],
        out_shape=[
            jax.ShapeDtypeStruct((B, c3, T), F32),
            jax.ShapeDtypeStruct((N, D_MEM), BF16),
        ],
        compiler_params=_cp("parallel", "parallel"),
        name="hy_inproj",
    )(h, g, wmt, wcq)


def _filt_mlp_kernel(feats_ref, w1t_ref, b1_ref, w2t_ref, b2_ref, fr_ref, h2_ref):
    fr = fr_ref[...]
    for d in range(2):
        a = jnp.dot(w1t_ref[...], feats_ref[d], precision=HI, preferred_element_type=F32)
        h1 = jnp.sin(fr * (a + b1_ref[...]))
        a = jnp.dot(w2t_ref[...], h1, precision=HI, preferred_element_type=F32)
        h2_ref[d] = jnp.sin(fr * (a + b2_ref[...]))


def _filt_taps_kernel(h2_ref, w3t_ref, t_ref, dl_ref, out_ref):
    T = h2_ref.shape[2]
    dl = dl_ref[...]
    hf = jnp.dot(w3t_ref[0, 0], h2_ref[0], precision=HI, preferred_element_type=F32) * jnp.exp(-t_ref[0] * dl)
    hb = jnp.dot(w3t_ref[0, 1], h2_ref[1], precision=HI, preferred_element_type=F32) * jnp.exp(-t_ref[1] * dl)
    nrm = jnp.sum(jnp.abs(hf) + jnp.abs(hb), axis=-1, keepdims=True) + 1e-6
    inv = 1.0 / nrm
    tap0 = lax.broadcasted_iota(I32, hf.shape, 1) == 0
    out_ref[0, :, :T] = (hf + jnp.where(tap0, hb, 0.0)) * inv
    out_ref[0, :, T:] = jnp.where(tap0, 0.0, hb) * inv


def _fwd_fft(x3, f1s, twr, twi, w2, cb, n1k):
    rhs = jnp.concatenate([x3[c].astype(BF16) for c in range(cb)], axis=1)
    a = _dot(f1s, rhs)
    ar, ai = a[:n1k], a[n1k:]
    tr = (ar * twr - ai * twi).astype(BF16)
    ti = (ar * twi + ai * twr).astype(BF16)
    lr = jnp.concatenate([tr[:, c * LANES:(c + 1) * LANES] for c in range(cb)], axis=0)
    li = jnp.concatenate([ti[:, c * LANES:(c + 1) * LANES] for c in range(cb)], axis=0)
    return _dot(jnp.concatenate([lr, li], axis=1), w2)


def _inv_fft(y, minv, twr, twi, g1, cb, n1k):
    b = _dot(y.astype(BF16), minv)
    br, bi = b[:, :LANES], b[:, LANES:]
    pr = (br * twr + bi * twi).astype(BF16)
    pi = (bi * twr - br * twi).astype(BF16)
    top = jnp.concatenate([pr[c * n1k:(c + 1) * n1k] for c in range(cb)], axis=1)
    bot = jnp.concatenate([pi[c * n1k:(c + 1) * n1k] for c in range(cb)], axis=1)
    rhs = jnp.concatenate([top, bot], axis=0)
    return _dot(g1, rhs)


def _filt_fft_kernel(taps_ref, f1_ref, twr_ref, twi_ref, w2_ref, k_ref, *, cb, n1k):
    k_ref[0] = _fwd_fft(taps_ref[0], f1_ref[...], twr_ref[...], twi_ref[...], w2_ref[...], cb, n1k)


def _hyena_filters_fft(p, T, cb):
    feats, t_rows = _filter_feats(T)
    C = p["w3t"].shape[2]
    tabs = _dft_tables(T, cb)
    n1k = 2 * T // LANES
    h2 = pl.pallas_call(
        _filt_mlp_kernel,
        out_shape=jax.ShapeDtypeStruct((2, HY_FILT, T), F32),
        compiler_params=pltpu.CompilerParams(vmem_limit_bytes=VMEM_LIMIT),
        name="hy_filt_mlp",
    )(jnp.asarray(feats), p["w1t"], p["b1"], p["w2t"], p["b2"], p["fr"])
    cbt = 64
    taps = pl.pallas_call(
        _filt_taps_kernel,
        grid=(2, C // cbt),
        in_specs=[
            pl.BlockSpec((2, HY_FILT, T), lambda o, c: (0, 0, 0)),
            pl.BlockSpec((1, 2, cbt, HY_FILT), lambda o, c: (o, 0, c, 0)),
            pl.BlockSpec((2, 1, T), lambda o, c: (0, 0, 0)),
            pl.BlockSpec((cbt, 1), lambda o, c: (c, 0)),
        ],
        out_specs=pl.BlockSpec((1, cbt, 2 * T), lambda o, c: (o, c, 0)),
        out_shape=jax.ShapeDtypeStruct((2, C, 2 * T), F32),
        compiler_params=_cp("parallel", "parallel"),
        name="hy_filt_taps",
    )(h2, p["w3t"], jnp.asarray(t_rows), p["absdelta"])
    taps = taps.reshape(2, C, n1k, LANES)
    const = lambda shape: pl.BlockSpec(shape, lambda o, c: (0,) * len(shape))
    f32 = lambda k: jnp.asarray(tabs[k], F32)
    return pl.pallas_call(
        functools.partial(_filt_fft_kernel, cb=cb, n1k=n1k),
        grid=(2, C // cb),
        in_specs=[
            pl.BlockSpec((1, cb, n1k, LANES), lambda o, c: (o, c, 0, 0)),
            const((2 * n1k, n1k)), const((n1k, cb * LANES)), const((n1k, cb * LANES)),
            const((2 * LANES, 2 * LANES)),
        ],
        out_specs=pl.BlockSpec((1, cb * n1k, 2 * LANES), lambda o, c: (o, c, 0)),
        out_shape=jax.ShapeDtypeStruct((2, C * n1k, 2 * LANES), F32),
        compiler_params=_cp("parallel", "parallel"),
        name="hy_filt_fft",
    )(taps, f32("f1full").astype(BF16), f32("twr_l"), f32("twi_l"), f32("w2").astype(BF16))


def _time_neighbours(x):
    rows = x.shape[0]
    lane = lax.broadcasted_iota(I32, x.shape, 1)
    r = pltpu.roll(x, 1, 1)
    rr = pltpu.roll(r, 1, 0)
    prev = jnp.where(lane == 0, rr, r)
    r2 = pltpu.roll(x, LANES - 1, 1)
    rr2 = pltpu.roll(r2, rows - 1, 0)
    nxt = jnp.where(lane == LANES - 1, rr2, r2)
    return prev, nxt


def _hy_conv_kernel(x1_ref, x2_ref, v_ref, sw1_ref, sw2_ref, swv_ref, skip_ref, k_ref,
                    f1s_ref, twrl_ref, twil_ref, w2_ref, minv_ref, twrr_ref, twir_ref,
                    g1_ref, o_ref, *, cb, n1k, n1):
    rows = cb * n1
    shape2 = (rows, LANES)
    row = lax.broadcasted_iota(I32, shape2, 0)
    lane = lax.broadcasted_iota(I32, shape2, 1)
    first = (lane == 0) & (row % n1 == 0)
    last = (lane == LANES - 1) & (row % n1 == n1 - 1)

    def sconv(x_ref, sw_ref):
        x = x_ref[0].reshape(shape2)
        prev, nxt = _time_neighbours(x)
        prev = jnp.where(first, 0.0, prev)
        nxt = jnp.where(last, 0.0, nxt)
        w = [jnp.broadcast_to(sw_ref[j], (cb, n1, LANES)).reshape(shape2) for j in range(3)]
        return prev * w[0] + x * w[1] + nxt * w[2]

    z = sconv(v_ref, swv_ref)
    gates = (sconv(x1_ref, sw1_ref), sconv(x2_ref, sw2_ref))
    f1s, w2 = f1s_ref[...], w2_ref[...]
    for o in range(2):
        zf = _fwd_fft(z.reshape(cb, n1, LANES), f1s, twrl_ref[...], twil_ref[...], w2, cb, n1k)
        kk = k_ref[o]
        zr, zi = zf[:, :LANES], zf[:, LANES:]
        kr, ki = kk[:, :LANES], kk[:, LANES:]
        y = jnp.concatenate([zr * kr - zi * ki, zr * ki + zi * kr], axis=1)
        conv = _inv_fft(y, minv_ref[...], twrr_ref[...], twir_ref[...], g1_ref[...], cb, n1k)
        conv = jnp.concatenate([conv[:, c * LANES:(c + 1) * LANES] for c in range(cb)], axis=0)
        skip = jnp.broadcast_to(skip_ref[o], (cb, n1, LANES)).reshape(shape2)
        z = gates[o] * (conv + skip * z)
    o_ref[0] = z.reshape(cb, n1, LANES)


def _hy_conv(ut4, sw, skip, kfft, B, T, C, cb):
    n1k, n1 = 2 * T // LANES, T // LANES
    tabs = _dft_tables(T, cb)
    nct = C // cb
    bf = lambda k: jnp.asarray(tabs[k], F32).astype(BF16)
    f32 = lambda k: jnp.asarray(tabs[k], F32)
    const = lambda shape: pl.BlockSpec(shape, lambda c, b: (0,) * len(shape))
    ublk = lambda s: pl.BlockSpec((1, cb, n1, LANES), lambda c, b, s=s: (b, s * nct + c, 0, 0))
    wblk = lambda s: pl.BlockSpec((3, cb, 1, LANES), lambda c, b, s=s: (0, s * nct + c, 0, 0))
    return pl.pallas_call(
        functools.partial(_hy_conv_kernel, cb=cb, n1k=n1k, n1=n1),
        grid=(nct, B),
        in_specs=[
            ublk(0), ublk(1), ublk(2), wblk(0), wblk(1), wblk(2),
            pl.BlockSpec((2, cb, 1, LANES), lambda c, b: (0, c, 0, 0)),
            pl.BlockSpec((2, cb * n1k, 2 * LANES), lambda c, b: (0, c, 0)),
            const((2 * n1k, n1)), const((n1k, cb * LANES)), const((n1k, cb * LANES)),
            const((2 * LANES, 2 * LANES)), const((2 * LANES, 2 * LANES)),
            const((cb * n1k, LANES)), const((cb * n1k, LANES)), const((n1, 2 * n1k)),
        ],
        out_specs=pl.BlockSpec((1, cb, n1, LANES), lambda c, b: (b, c, 0, 0)),
        out_shape=jax.ShapeDtypeStruct((B, C, n1, LANES), F32),
        compiler_params=_cp("parallel", "arbitrary"),
        name="hy_conv",
    )(ut4, ut4, ut4, sw, sw, sw, skip, kfft,
      bf("f1s"), f32("twr_l"), f32("twi_l"), bf("w2"), bf("minv"),
      f32("twr_r"), f32("twi_r"), bf("g1"))


def _head_norm(x, bd, g):
    x2 = x * x
    hi = x2.astype(BF16)
    lo = (x2 - hi.astype(F32)).astype(BF16)
    ss = _dot(hi, bd) + _dot(lo, bd)
    return x * lax.rsqrt(ss * (1.0 / HEAD_DIM) + NORM_EPS) * g


def _rope(x, cos2, sin2):
    lane = lax.broadcasted_iota(I32, cos2.shape, 1)
    low = (lane // (ROPE_AXIS_DIM // 2)) % 2 == 0
    out = []
    for c in range(x.shape[1] // LANES):
        xc = x[:, c * LANES:(c + 1) * LANES]
        up = pltpu.roll(xc, LANES - 16, 1)
        dn = pltpu.roll(xc, 16, 1)
        out.append(xc * cos2 + jnp.where(low, up, dn) * sin2)
    return jnp.concatenate(out, axis=1)


def _at_inproj_kernel(h_ref, g_ref, w_ref, bdq_ref, bdk_ref, gq_ref, gk_ref, cos_ref, sin_ref,
                      qt_ref, kz_ref, vt_ref, cq_ref, *, dq, dk):
    xn = _rms(h_ref[...], g_ref[...]).astype(BF16)
    proj = _dot(xn, w_ref[...])
    q, k = proj[:, :dq], proj[:, dq:dq + dk]
    v, cq = proj[:, dq + dk:dq + 2 * dk], proj[:, dq + 2 * dk:]
    cos2, sin2 = cos_ref[...], sin_ref[...]
    qr = _rope(_head_norm(q, bdq_ref[...], gq_ref[...]), cos2, sin2)
    qt = (qr * (HEAD_DIM ** -0.5 * LOG2E)).T.astype(BF16)
    for p in range(dq // LANES):
        qt_ref[0, p] = qt[p * LANES:(p + 1) * LANES]
    kr = _rope(_head_norm(k, bdk_ref[...], gk_ref[...]), cos2, sin2)
    vt = v.T.astype(BF16)
    zv = jnp.zeros((HEAD_DIM, vt.shape[1]), BF16)
    lane = lax.broadcasted_iota(I32, (k.shape[0], LANES), 1)
    for kv in range(N_KV_HEADS):
        rows = vt[kv * HEAD_DIM:(kv + 1) * HEAD_DIM]
        vt_ref[0, kv, 0] = jnp.concatenate([rows, zv], axis=0)
        vt_ref[0, kv, 1] = jnp.concatenate([zv, rows], axis=0)
        pair = kr[:, (kv // 2) * LANES:(kv // 2 + 1) * LANES]
        own = jnp.where((lane < HEAD_DIM) == (kv % 2 == 0), pair, 0.0)
        other = pltpu.roll(own, HEAD_DIM, 1)
        lo, hi = (own, other) if kv % 2 == 0 else (other, own)
        kz_ref[0, kv, 0] = lo.astype(BF16)
        kz_ref[0, kv, 1] = hi.astype(BF16)
    cq_ref[...] = cq.astype(BF16)


def _at_inproj(h, g, w, gq, gk, B, T, tm):
    N, D = h.shape
    dq, dk = GQA_GROUP * N_KV_HEADS * HEAD_DIM, N_KV_HEADS * HEAD_DIM
    nt = T // tm
    cos2, sin2 = _rope_tables(T)
    bd = lambda n: jnp.asarray(np.kron(np.eye(n // HEAD_DIM), np.ones((HEAD_DIM, HEAD_DIM))), F32).astype(BF16)
    const = lambda shape: pl.BlockSpec(shape, lambda b, i: (0,) * len(shape))
    return pl.pallas_call(
        functools.partial(_at_inproj_kernel, dq=dq, dk=dk),
        grid=(B, nt),
        in_specs=[
            pl.BlockSpec((tm, D), lambda b, i: (b * nt + i, 0)),
            const((1, D)), const(w.shape), const((dq, dq)), const((dk, dk)),
            const((1, dq)), const((1, dk)),
            pl.BlockSpec((tm, LANES), lambda b, i: (i, 0)),
            pl.BlockSpec((tm, LANES), lambda b, i: (i, 0)),
        ],
        out_specs=[
            pl.BlockSpec((1, dq // LANES, LANES, tm), lambda b, i: (b, 0, 0, i)),
            pl.BlockSpec((1, N_KV_HEADS, 2, tm, LANES), lambda b, i: (b, 0, 0, i, 0)),
            pl.BlockSpec((1, N_KV_HEADS, 2, LANES, tm), lambda b, i: (b, 0, 0, 0, i)),
            pl.BlockSpec((tm, D_MEM), lambda b, i: (b * nt + i, 0)),
        ],
        out_shape=[
            jax.ShapeDtypeStruct((B, dq // LANES, LANES, T), BF16),
            jax.ShapeDtypeStruct((B, N_KV_HEADS, 2, T, LANES), BF16),
            jax.ShapeDtypeStruct((B, N_KV_HEADS, 2, LANES, T), BF16),
            jax.ShapeDtypeStruct((N, D_MEM), BF16),
        ],
        compiler_params=_cp("parallel", "parallel"),
        name="at_inproj",
    )(h, g, w, bd(dq), bd(dk), gq, gk, jnp.asarray(cos2), jnp.asarray(sin2))


def _flash_kernel(qt_ref, ka_ref, kb_ref, vta_ref, vtb_ref, o_ref, m_sc, l_sc, acc_sc, s_sc, p_sc):
    ki = pl.program_id(3)

    @pl.when(ki == 0)
    def _():
        m_sc[...] = jnp.full(m_sc.shape, -jnp.inf, F32)
        l_sc[...] = jnp.zeros(l_sc.shape, F32)
        acc_sc[...] = jnp.zeros(acc_sc.shape, F32)

    qt = qt_ref[0, 0]
    tk, tq = s_sc.shape[1], s_sc.shape[2]
    sub = 8
    ck = 2 * sub
    pv, alphas = [], []
    for idx, k_ref in enumerate((ka_ref, kb_ref)):
        s_sc[idx] = _dot(k_ref[0, 0, 0], qt)
    for idx, vt_ref in enumerate((vta_ref, vtb_ref)):
        mx = s_sc[idx, 0:sub, :]
        for c in range(1, tk // sub):
            mx = jnp.maximum(mx, s_sc[idx, c * sub:(c + 1) * sub, :])
        m_prev = m_sc[idx]
        m_new = jnp.maximum(m_prev, jnp.max(mx, axis=0, keepdims=True))
        alpha = jnp.exp2(m_prev - m_new)
        lsum = jnp.zeros((sub, tq), F32)
        for c in range(tk // ck):
            p = jnp.exp2(s_sc[idx, c * ck:(c + 1) * ck, :] - m_new)
            lsum = lsum + p[:sub] + p[sub:]
            p_sc[idx, c * ck:(c + 1) * ck, :] = p.astype(BF16)
        l_sc[idx] = alpha * l_sc[idx] + jnp.sum(lsum, axis=0, keepdims=True)
        m_sc[idx] = m_new
        pv.append(_dot(vt_ref[0, 0, 0], p_sc[idx]))
        alphas.append(alpha)
    row = lax.broadcasted_iota(I32, acc_sc.shape, 0)
    low = row < HEAD_DIM
    acc_sc[...] = acc_sc[...] * jnp.where(low, alphas[0], alphas[1]) + pv[0] + pv[1]

    @pl.when(ki == pl.num_programs(3) - 1)
    def _():
        o_ref[0] = acc_sc[...] / jnp.where(low, l_sc[0], l_sc[1])


def _flash(qt, kz, vtz, B, T, tq, tk):
    npair = qt.shape[1]
    nq, nk = T // tq, T // tk
    kv_of = lambda p, j: (2 * p + j) // GQA_GROUP
    return pl.pallas_call(
        _flash_kernel,
        grid=(B, npair, nq, nk),
        in_specs=[
            pl.BlockSpec((1, 1, LANES, tq), lambda b, p, i, k: (b, p, 0, i)),
            pl.BlockSpec((1, 1, 1, tk, LANES), lambda b, p, i, k: (b, kv_of(p, 0), 0, k, 0)),
            pl.BlockSpec((1, 1, 1, tk, LANES), lambda b, p, i, k: (b, kv_of(p, 1), 1, k, 0)),
            pl.BlockSpec((1, 1, 1, LANES, tk), lambda b, p, i, k: (b, kv_of(p, 0), 0, 0, k)),
            pl.BlockSpec((1, 1, 1, LANES, tk), lambda b, p, i, k: (b, kv_of(p, 1), 1, 0, k)),
        ],
        out_specs=pl.BlockSpec((1, LANES, tq), lambda b, p, i, k: (b, p, i)),
        out_shape=jax.ShapeDtypeStruct((B, npair * LANES, T), F32),
        scratch_shapes=[
            pltpu.VMEM((2, 1, tq), F32), pltpu.VMEM((2, 1, tq), F32), pltpu.VMEM((LANES, tq), F32),
            pltpu.VMEM((2, tk, tq), F32), pltpu.VMEM((2, tk, tq), BF16),
        ],
        compiler_params=_cp("parallel", "parallel", "parallel", "arbitrary"),
        name="flash_gqa",
    )(qt, kz, kz, vtz, vtz)


def _memkv_kernel(mem_ref, g_ref, wkvt_ref, wv_ref, mkt_ref, mv_ref):
    mn = _rms(mem_ref[0], g_ref[...]).astype(BF16)
    kt = _dot_nt(wkvt_ref[...], mn) * (HEAD_DIM ** -0.5)
    v = _dot(mn, wv_ref[...])
    row = lax.broadcasted_iota(I32, kt.shape, 0)
    col = lax.broadcasted_iota(I32, v.shape, 1)
    for hd in range(MEM_HEADS):
        mkt_ref[0, hd] = jnp.where(row // HEAD_DIM == hd, kt, 0.0).astype(BF16)
        mv_ref[0, hd] = jnp.where(col // HEAD_DIM == hd, v, 0.0).astype(BF16)


def _memkv(mem, g, wkt, wv):
    B, M, D = mem.shape
    return pl.pallas_call(
        _memkv_kernel,
        grid=(B,),
        in_specs=[
            pl.BlockSpec((1, M, D), lambda b: (b, 0, 0)),
            pl.BlockSpec((1, D), lambda b: (0, 0)),
            pl.BlockSpec((D_MEM, D), lambda b: (0, 0)),
            pl.BlockSpec((D, D_MEM), lambda b: (0, 0)),
        ],
        out_specs=[
            pl.BlockSpec((1, MEM_HEADS, D_MEM, M), lambda b: (b, 0, 0, 0)),
            pl.BlockSpec((1, MEM_HEADS, M, D_MEM), lambda b: (b, 0, 0, 0)),
        ],
        out_shape=[
            jax.ShapeDtypeStruct((B, MEM_HEADS, D_MEM, M), BF16),
            jax.ShapeDtypeStruct((B, MEM_HEADS, M, D_MEM), BF16),
        ],
        compiler_params=_cp("parallel"),
        name="mem_kv",
    )(mem, g, wkt, wv)


def _cross_attn(cq, mkt_ref, mv_ref):
    acc = jnp.zeros(cq.shape, F32)
    for hd in range(MEM_HEADS):
        s = _dot(cq, mkt_ref[0, hd])
        p = jnp.exp(s - jnp.max(s, axis=-1, keepdims=True))
        p = p / jnp.sum(p, axis=-1, keepdims=True)
        acc = acc + _dot(p.astype(BF16), mv_ref[0, hd])
    return acc


def _outproj_kernel(h_ref, main_ref, cq_ref, mkt_ref, mv_ref, wm_ref, wc_ref, o_ref):
    if len(main_ref.shape) == 3:
        main = main_ref[0].T
    else:
        main = jnp.concatenate([main_ref[0, :, j, :].T for j in range(main_ref.shape[2])], axis=0)
    main = main.astype(BF16)
    cross = _cross_attn(cq_ref[...], mkt_ref, mv_ref).astype(BF16)
    o_ref[...] = h_ref[...] + _dot(main, wm_ref[...]) + _dot(cross, wc_ref[...])


def _outproj(h, main, cq, mkt, mv, wm, wc, B, T, tm):
    N, D = h.shape
    C = wm.shape[0]
    M = mkt.shape[-1]
    nt = T // tm
    if main.ndim == 3:
        mspec = pl.BlockSpec((1, C, tm), lambda b, i: (b, 0, i))
    else:
        mspec = pl.BlockSpec((1, C, tm // LANES, LANES), lambda b, i: (b, 0, i, 0))
    return pl.pallas_call(
        _outproj_kernel,
        grid=(B, nt),
        in_specs=[
            pl.BlockSpec((tm, D), lambda b, i: (b * nt + i, 0)),
            mspec,
            pl.BlockSpec((tm, D_MEM), lambda b, i: (b * nt + i, 0)),
            pl.BlockSpec((1, MEM_HEADS, D_MEM, M), lambda b, i: (b, 0, 0, 0)),
            pl.BlockSpec((1, MEM_HEADS, M, D_MEM), lambda b, i: (b, 0, 0, 0)),
            pl.BlockSpec((C, D), lambda b, i: (0, 0)),
            pl.BlockSpec((D_MEM, D), lambda b, i: (0, 0)),
        ],
        out_specs=pl.BlockSpec((tm, D), lambda b, i: (b * nt + i, 0)),
        out_shape=jax.ShapeDtypeStruct((N, D), F32),
        compiler_params=_cp("parallel", "parallel"),
        name="outproj",
    )(h, main, cq, mkt, mv, wm, wc)


def _router_kernel(h_ref, g_ref, wr_ref, wrt_ref, xn_ref, gsp_ref, afft_ref):
    xn = _rms(h_ref[...], g_ref[...]).astype(BF16)
    xn_ref[...] = xn
    lg = _dot(xn, wr_ref[...])
    lane = lax.broadcasted_iota(I32, lg.shape, 1)
    lg = jnp.where(lane < N_EXPERTS, lg, -jnp.inf)
    p = jnp.exp(lg - jnp.max(lg, axis=-1, keepdims=True))
    aff = p / jnp.sum(p, axis=-1, keepdims=True)
    hi = aff.astype(BF16).astype(F32)
    mid = (aff - hi).astype(BF16).astype(F32)
    lo = (aff - hi - mid).astype(BF16).astype(F32)
    gsp_ref[...] = (hi + pltpu.roll(mid, N_EXPERTS, 1) + pltpu.roll(lo, 2 * N_EXPERTS, 1)).astype(BF16)
    lt = _dot_nt(wrt_ref[...], xn)
    pt = jnp.exp(lt - jnp.max(lt, axis=0, keepdims=True))
    pt = pt / jnp.sum(pt, axis=0, keepdims=True)
    for j in range(lt.shape[1] // LANES):
        afft_ref[0, j] = pt[:, j * LANES:(j + 1) * LANES]


def _router(h, g, wr, wrt, B, T, tm):
    N, D = h.shape
    nt = T // tm
    return pl.pallas_call(
        _router_kernel,
        grid=(B, nt),
        in_specs=[
            pl.BlockSpec((tm, D), lambda b, i: (b * nt + i, 0)),
            pl.BlockSpec((1, D), lambda b, i: (0, 0)),
            pl.BlockSpec((D, LANES), lambda b, i: (0, 0)),
            pl.BlockSpec((N_EXPERTS, D), lambda b, i: (0, 0)),
        ],
        out_specs=[
            pl.BlockSpec((tm, D), lambda b, i: (b * nt + i, 0)),
            pl.BlockSpec((tm, LANES), lambda b, i: (b * nt + i, 0)),
            pl.BlockSpec((1, tm // LANES, N_EXPERTS, LANES), lambda b, i: (b, i, 0, 0)),
        ],
        out_shape=[
            jax.ShapeDtypeStruct((N, D), BF16),
            jax.ShapeDtypeStruct((N, LANES), BF16),
            jax.ShapeDtypeStruct((B, T // LANES, N_EXPERTS, LANES), F32),
        ],
        compiler_params=_cp("parallel", "parallel"),
        name="moe_router",
    )(h, g, wr, wrt)


def _topk_kernel(aff_ref, mall_ref, mw_ref, mb_ref, ma_ref, tri_ref, pos_ref, aoff_ref, pad_ref,
                 *, cap, nj):
    E = N_EXPERTS
    aff3 = aff_ref[0]

    def count(mask3):
        per = jnp.sum(mask3.astype(F32), axis=0)
        return jnp.broadcast_to(jnp.sum(per, axis=-1, keepdims=True), (E, LANES))

    def step(i, thr):
        cand = thr | lax.shift_left(jnp.int32(1), 30 - i)
        ok = count(aff3 >= pltpu.bitcast(cand, F32)[None]) >= cap
        return jnp.where(ok, cand, thr)

    thr = lax.fori_loop(0, 31, step, jnp.zeros((E, LANES), I32))
    thr = pltpu.bitcast(thr, F32)
    gt3 = aff3 > thr[None]
    eq3 = aff3 == thr[None]
    need = cap - count(gt3)

    ones = jnp.ones((LANES, LANES), BF16)
    tri = tri_ref[...]

    def prefix(mask2):
        mb = mask2.astype(BF16)
        incl = _dot(mb, tri)
        tot = _dot(mb, ones)
        return incl - mask2, tot

    eq2 = eq3.reshape(nj * E, LANES).astype(F32)
    ex, tot = prefix(eq2)
    eq_rank = ex + _dot(mall_ref[...], tot.astype(BF16))
    need2 = jnp.broadcast_to(need[None], (nj, E, LANES)).reshape(nj * E, LANES)
    sel = jnp.where((gt3.reshape(nj * E, LANES)) | ((eq2 > 0) & (eq_rank < need2)), 1.0, 0.0)

    ex, tot = prefix(sel)
    totb = tot.astype(BF16)
    within = _dot(mw_ref[...], totb)
    cnt = _dot(mb_ref[...], totb)
    padded = (cnt.astype(I32) + (BF16_SUBLANES - 1)) & ~(BF16_SUBLANES - 1)
    aoff = _dot(ma_ref[...], padded.astype(F32).astype(BF16))
    pos = (aoff + within + ex).astype(I32)
    pos_ref[0] = jnp.where(sel > 0, pos, -1).reshape(nj, E, LANES)
    aoff_ref[0] = aoff.astype(I32).reshape(nj, E, LANES)
    pad_ref[0] = padded.reshape(nj, E, LANES)


def _topk(aff4, T, tb):
    B, nj = aff4.shape[0], aff4.shape[1]
    cap = EC_CAPACITY_FACTOR * T // N_EXPERTS
    tabs = [jnp.asarray(m, F32).astype(BF16) for m in _topk_tables(T, tb)]
    R = nj * N_EXPERTS
    blk = pl.BlockSpec((1, nj, N_EXPERTS, LANES), lambda b: (b, 0, 0, 0))
    const = lambda shape: pl.BlockSpec(shape, lambda b: (0,) * len(shape))
    out = jax.ShapeDtypeStruct((B, nj, N_EXPERTS, LANES), I32)
    return pl.pallas_call(
        functools.partial(_topk_kernel, cap=cap, nj=nj),
        grid=(B,),
        in_specs=[blk, const((R, R)), const((R, R)), const((R, R)), const((R, R)), const((LANES, LANES))],
        out_specs=[blk, blk, blk],
        out_shape=[out, out, out],
        compiler_params=_cp("parallel"),
        name="moe_topk",
    )(aff4, *tabs)


def _compress_kernel(aoff_ref, nch_ref, nmax_ref, xn_ref, gsp_ref, pos_ref, xg_ref, gs_ref, *, nblk, tb, W):
    b, d, blk = pl.program_id(0), pl.program_id(1), pl.program_id(2)

    @pl.when(blk == 0)
    def _():
        xg_ref[...] = jnp.zeros(xg_ref.shape, BF16)

    @pl.when((blk == 0) & (d == 0))
    def _():
        gs_ref[...] = jnp.zeros(gs_ref.shape, F32)

    rhs = xn_ref[...]
    base = (b * nblk + blk) * N_EXPERTS
    iota_s = lax.broadcasted_iota(I32, (W, LANES), 0)

    def chunk(c, carry):
        pieces = []
        for e in range(N_EXPERTS):
            a = aoff_ref[base + e] + c * W
            g = [pos_ref[0, jj, e:e + 1, :] - a == iota_s for jj in range(tb // LANES)]
            pieces.append(jnp.where(jnp.concatenate(g, axis=1), 1.0, 0.0).astype(BF16))
        lhs = jnp.concatenate(pieces, axis=0)
        res = _dot(lhs, rhs).astype(BF16)
        for e in range(N_EXPERTS):
            @pl.when(c < nch_ref[base + e])
            def _():
                a = pl.multiple_of(aoff_ref[base + e] + c * W, BF16_SUBLANES)
                xg_ref[0, e, pl.ds(a, W), :] = res[e * W:(e + 1) * W]

        @pl.when(d == 0)
        def _():
            resg = _dot(lhs, gsp_ref[...])
            for e in range(N_EXPERTS):
                @pl.when(c < nch_ref[base + e])
                def _():
                    a = pl.multiple_of(aoff_ref[base + e] + c * W, BF16_SUBLANES)
                    gs_ref[0, e, pl.ds(a, W), :] = resg[e * W:(e + 1) * W]
        return carry

    lax.fori_loop(0, nmax_ref[b * nblk + blk], chunk, 0)


def _compress(aoff, nch, nmax, xn, gsp, pos4, B, T, tb, dw, rows, W):
    N, D = xn.shape
    nblk = T // tb
    return pl.pallas_call(
        functools.partial(_compress_kernel, nblk=nblk, tb=tb, W=W),
        grid_spec=pltpu.PrefetchScalarGridSpec(
            num_scalar_prefetch=3,
            grid=(B, D // dw, nblk),
            in_specs=[
                pl.BlockSpec((tb, dw), lambda b, d, k, *_: (b * nblk + k, d)),
                pl.BlockSpec((tb, LANES), lambda b, d, k, *_: (b * nblk + k, 0)),
                pl.BlockSpec((1, tb // LANES, N_EXPERTS, LANES), lambda b, d, k, *_: (b, k, 0, 0)),
            ],
            out_specs=[
                pl.BlockSpec((1, N_EXPERTS, rows, dw), lambda b, d, k, *_: (b, 0, 0, d)),
                pl.BlockSpec((1, N_EXPERTS, rows, LANES), lambda b, d, k, *_: (b, 0, 0, 0)),
            ],
        ),
        out_shape=[
            jax.ShapeDtypeStruct((B, N_EXPERTS, rows, D), BF16),
            jax.ShapeDtypeStruct((B, N_EXPERTS, rows, LANES), F32),
        ],
        compiler_params=_cp("arbitrary", "arbitrary", "arbitrary"),
        name="moe_compress",
    )(aoff, nch, nmax, xn, gsp, pos4)


def _ffn_kernel(x_ref, gs_ref, wg_ref, wu_ref, wd_ref, y_ref, *, live):
    e = pl.program_id(0)
    x = x_ref[0, 0, :live]
    g = _dot(x, wg_ref[0, 0].astype(BF16))
    u = _dot(x, wu_ref[0, 0].astype(BF16))
    hid = (g * (1.0 / (1.0 + jnp.exp(-g))) * u).astype(BF16)
    gs = gs_ref[0, 0, :live]
    lane = lax.broadcasted_iota(I32, gs.shape, 1)
    mine = (lane == e) | (lane == e + N_EXPERTS) | (lane == e + 2 * N_EXPERTS)
    gate = jnp.sum(jnp.where(mine, gs, 0.0), axis=-1, keepdims=True)
    y_ref[0, 0, :live] = (_dot(hid, wd_ref[0, 0].astype(BF16)) * gate).astype(BF16)
    y_ref[0, 0, live:] = jnp.zeros((y_ref.shape[2] - live, y_ref.shape[3]), BF16)


def _ffn(xg, gs, wg, wu, wd, layer, live):
    B, E, rows, D = xg.shape
    F = wg.shape[-1]
    return pl.pallas_call(
        functools.partial(_ffn_kernel, live=live),
        grid=(E, B),
        in_specs=[
            pl.BlockSpec((1, 1, rows, D), lambda e, b: (b, e, 0, 0)),
            pl.BlockSpec((1, 1, rows, LANES), lambda e, b: (b, e, 0, 0)),
            pl.BlockSpec((1, 1, D, F), lambda e, b: (layer, e, 0, 0)),
            pl.BlockSpec((1, 1, D, F), lambda e, b: (layer, e, 0, 0)),
            pl.BlockSpec((1, 1, F, D), lambda e, b: (layer, e, 0, 0)),
        ],
        out_specs=pl.BlockSpec((1, 1, rows, D), lambda e, b: (b, e, 0, 0)),
        out_shape=jax.ShapeDtypeStruct((B, E, rows, D), BF16),
        compiler_params=_cp("parallel", "arbitrary"),
        name="moe_ffn",
    )(xg, gs, wg, wu, wd)


def _expand_kernel(aoff_ref, nch_ref, nmax_ref, h_ref, post_ref, spread_ref, y_ref, o_ref,
                   *, nblk, tb, W, rows):
    b, blk = pl.program_id(0), pl.program_id(2)
    base = (b * nblk + blk) * N_EXPERTS
    lane = lax.broadcasted_iota(I32, (tb, LANES), 1)
    low = lane < W
    slot1 = ((lane & (W - 1)) + 1).astype(F32)
    spread = _dot(post_ref[...], spread_ref[...])
    rel = [spread[:, i * LANES:(i + 1) * LANES] - slot1 for i in range(N_EXPERTS // 2)]
    never = jnp.float32(-2.0 ** 20)

    def chunk(c, acc):
        cols, wins = [], []
        for i in range(N_EXPERTS // 2):
            tgt = []
            for e in (2 * i, 2 * i + 1):
                a = aoff_ref[base + e] + c * W
                tgt.append(jnp.where(c < nch_ref[base + e], a.astype(F32), never))
                a_in = pl.multiple_of(jnp.minimum(a, rows - W), BF16_SUBLANES)
                wins.append(y_ref[0, e, pl.ds(a_in, W), :])
            hit = rel[i] == jnp.where(low, tgt[0], tgt[1])
            cols.append(jnp.where(hit, 1.0, 0.0).astype(BF16))
        return acc + _dot(jnp.concatenate(cols, axis=1), jnp.concatenate(wins, axis=0))

    acc = lax.fori_loop(0, nmax_ref[b * nblk + blk], chunk, jnp.zeros(o_ref.shape, F32))
    o_ref[...] = h_ref[...] + acc


def _expand(aoff, nch, nmax, h, post, y, B, T, tb, dw, W):
    N, D = h.shape
    rows = y.shape[2]
    nblk = T // tb
    return pl.pallas_call(
        functools.partial(_expand_kernel, nblk=nblk, tb=tb, W=W, rows=rows),
        grid_spec=pltpu.PrefetchScalarGridSpec(
            num_scalar_prefetch=3,
            grid=(B, D // dw, nblk),
            in_specs=[
                pl.BlockSpec((tb, dw), lambda b, d, k, *_: (b * nblk + k, d)),
                pl.BlockSpec((tb, 2 * N_EXPERTS), lambda b, d, k, *_: (b * nblk + k, 0)),
                pl.BlockSpec((2 * N_EXPERTS, N_EXPERTS * W), lambda b, d, k, *_: (0, 0)),
                pl.BlockSpec((1, N_EXPERTS, rows, dw), lambda b, d, k, *_: (b, 0, 0, d),
                             pipeline_mode=pl.Buffered(1)),
            ],
            out_specs=pl.BlockSpec((tb, dw), lambda b, d, k, *_: (b * nblk + k, d)),
        ),
        out_shape=jax.ShapeDtypeStruct((N, D), F32),
        compiler_params=_cp("parallel", "parallel", "arbitrary"),
        name="moe_expand",
    )(aoff, nch, nmax, h, post, jnp.asarray(_spread_table(W), F32).astype(BF16), y)


def _moe(h, g, wr, wg, wu, wd, layer, B, T, tm):
    N, D = h.shape
    tb = min(512, T)
    W = LANES // 2
    nblk = T // tb
    cap = EC_CAPACITY_FACTOR * T // N_EXPERTS
    rows = -(-(cap + nblk * (BF16_SUBLANES - 1)) // LANES) * LANES + LANES
    dw = 256
    wr_pad = jnp.zeros((D, LANES), BF16).at[:, :N_EXPERTS].set(wr.astype(BF16))
    xn, gsp, aff4 = _router(h, g, wr_pad, wr.T.astype(BF16), B, T, tm)
    pos4, aoff4, pad4 = _topk(aff4, T, tb)
    per = tb // LANES
    aoff = aoff4[:, ::per, :, 0].reshape(-1)
    nch4 = (pad4[:, ::per, :, 0] + (W - 1)) // W
    nch = nch4.reshape(-1)
    nmax = jnp.max(nch4, axis=-1).reshape(-1)
    post = jnp.transpose(pos4, (0, 1, 3, 2)).reshape(N, N_EXPERTS) + 1
    post = jnp.concatenate([post >> 5, post & 31], axis=1).astype(BF16)
    xg, gs = _compress(aoff, nch, nmax, xn, gsp, pos4, B, T, tb, dw, rows, W)
    y = _ffn(xg, gs, wg, wu, wd, layer, rows - LANES)
    return _expand(aoff, nch, nmax, h, post, y, B, T, tb, 2 * dw, W)


def _final_norm_kernel(h_ref, g_ref, o_ref):
    o_ref[...] = _rms(h_ref[...], g_ref[...])


def _final_norm(h, g, tm):
    N, D = h.shape
    return pl.pallas_call(
        _final_norm_kernel,
        grid=(N // tm,),
        in_specs=[pl.BlockSpec((tm, D), lambda i: (i, 0)), pl.BlockSpec((1, D), lambda i: (0, 0))],
        out_specs=pl.BlockSpec((tm, D), lambda i: (i, 0)),
        out_shape=jax.ShapeDtypeStruct((N, D), F32),
        compiler_params=_cp("parallel"),
        name="final_norm",
    )(h, g)


def kernel(x, mem, mix_norm_g, ffn_norm_g, mem_norm_g, final_norm_g, w_mem_kv, w_out, hy_w_in, hy_short_w, hy_filt_w1, hy_filt_b1, hy_filt_w2, hy_filt_b2, hy_filt_w3, hy_filt_freq, hy_skip, at_w_in, at_q_norm_g, at_k_norm_g, router_w, exp_w_gate, exp_w_up, exp_w_down):
    B, T, D = x.shape
    depth = mix_norm_g.shape[0]
    C = hy_skip.shape[-1]
    tm = min(512, T)
    cb = 16
    n1 = T // LANES
    h = x.reshape(B * T, D)
    row = lambda v: v.reshape(1, -1).astype(F32)
    lanes = lambda v: jnp.broadcast_to(v[..., None, None], v.shape + (1, LANES)).astype(F32)
    max_decay = math.log(HY_DECAY_TARGET) / HY_FAST_PCT
    min_decay = math.log(HY_DECAY_TARGET) / HY_SLOW_PCT
    absdelta = jnp.asarray(np.abs(np.linspace(min_decay, max_decay, C)).astype(np.float32)).reshape(C, 1)

    for i in range(depth):
        j = i // 2
        wo = w_out[i].astype(BF16)
        wkv = w_mem_kv[i].astype(BF16)
        mkt, mv = _memkv(mem, row(mem_norm_g), wkv[:, :D_MEM].T, wkv[:, D_MEM:])
        if i % 2 == 0:
            w_in = hy_w_in[j].astype(BF16)
            ut, cq = _hy_inproj(h, row(mix_norm_g[i]), w_in[:, :3 * C].T, w_in[:, 3 * C:], B, T, tm)
            filt = dict(
                w1t=jnp.zeros((HY_FILT, 40), F32).at[:, :hy_filt_w1.shape[1]].set(hy_filt_w1[j].T),
                b1=hy_filt_b1[j].reshape(-1, 1), w2t=hy_filt_w2[j].T, b2=hy_filt_b2[j].reshape(-1, 1),
                fr=hy_filt_freq[j].reshape(-1, 1),
                w3t=hy_filt_w3[j].T.reshape(2, 2, C, HY_FILT), absdelta=absdelta,
            )
            kfft = _hyena_filters_fft(filt, T, cb)
            zt = _hy_conv(ut.reshape(B, 3 * C, n1, LANES), lanes(hy_short_w[j]), lanes(hy_skip[j]),
                          kfft, B, T, C, cb)
            h = _outproj(h, zt, cq, mkt, mv, wo[:C], wo[C:], B, T, min(1024, T))
        else:
            rep = lambda v, n: jnp.tile(v.astype(F32), n).reshape(1, -1)
            qt, kz, vtz, cq = _at_inproj(h, row(mix_norm_g[i]), at_w_in[j].astype(BF16),
                                         rep(at_q_norm_g[j], GQA_GROUP * N_KV_HEADS),
                                         rep(at_k_norm_g[j], N_KV_HEADS), B, T, tm)
            main_t = _flash(qt, kz, vtz, B, T, min(2048, T), min(1024, T))
            h = _outproj(h, main_t, cq, mkt, mv, wo[:C], wo[C:], B, T, tm)
        h = _moe(h, row(ffn_norm_g[i]), router_w[i], exp_w_gate, exp_w_up, exp_w_down, i, B, T, tm)
    return _final_norm(h, row(final_norm_g), tm).reshape(B, T, D)
```

```python
import functools
import math

import numpy as np
import jax
import jax.numpy as jnp
from jax import lax
from jax.experimental import pallas as pl
from jax.experimental.pallas import tpu as pltpu

F32 = jnp.float32
BF16 = jnp.bfloat16
I32 = jnp.int32

HEAD_DIM = 64
MEM_HEADS = 4
D_MEM = MEM_HEADS * HEAD_DIM
N_KV_HEADS = 4
GQA_GROUP = 3
GRID_W = 64
ROPE_THETA = 10000.0
ROPE_AXIS_DIM = HEAD_DIM // 2
HY_BANDS = 16
HY_FILT = 64
HY_DECAY_TARGET = 1e-2
HY_FAST_PCT = 0.3
HY_SLOW_PCT = 1.5
N_EXPERTS = 16
EC_CAPACITY_FACTOR = 2
NORM_EPS = 1e-6

LANES = 128
BF16_SUBLANES = 16
VMEM_LIMIT = 56 * 1024 * 1024
HI = lax.Precision.HIGHEST
LOG2E = 1.4426950408889634


def _cp(*sem):
    return pltpu.CompilerParams(dimension_semantics=sem, vmem_limit_bytes=VMEM_LIMIT)


def _rms(x, g):
    ms = jnp.mean(x * x, axis=-1, keepdims=True)
    return x * lax.rsqrt(ms + NORM_EPS) * g


def _dot(a, b):
    return jnp.dot(a, b, preferred_element_type=F32)


def _dot_nt(a, b):
    return lax.dot_general(a, b, (((1,), (1,)), ((), ())), preferred_element_type=F32)


@functools.lru_cache(maxsize=None)
def _dft_tables(T, cb):
    nf = 2 * T
    n1k = nf // LANES
    n1 = T // LANES
    k1 = np.arange(n1k)[:, None]
    a1 = 2 * np.pi * k1 * np.arange(n1)[None, :] / n1k
    f1s = np.concatenate([np.cos(a1), -np.sin(a1)], axis=0)
    a1f = 2 * np.pi * k1 * np.arange(n1k)[None, :] / n1k
    f1full = np.concatenate([np.cos(a1f), -np.sin(a1f)], axis=0)
    tw = 2 * np.pi * k1 * np.arange(LANES)[None, :] / nf
    twr, twi = np.cos(tw), -np.sin(tw)
    a2 = 2 * np.pi * np.arange(LANES)[:, None] * np.arange(LANES)[None, :] / LANES
    cr, ci = np.cos(a2), -np.sin(a2)
    w2 = np.block([[cr, ci], [-ci, cr]])
    minv = np.block([[cr, -ci], [ci, cr]])
    a3 = 2 * np.pi * np.arange(n1)[:, None] * np.arange(n1k)[None, :] / n1k
    g1 = np.concatenate([np.cos(a3), -np.sin(a3)], axis=1) / nf
    return dict(
        f1s=f1s, f1full=f1full, w2=w2, minv=minv, g1=g1,
        twr_l=np.tile(twr, (1, cb)), twi_l=np.tile(twi, (1, cb)),
        twr_r=np.tile(twr, (cb, 1)), twi_r=np.tile(twi, (cb, 1)),
    )


@functools.lru_cache(maxsize=None)
def _filter_feats(T):
    t = np.linspace(0.0, 1.0, T)[None, :]
    w = (2.0 * np.pi) * np.arange(T)[None, :] / T
    bands = np.linspace(1e-4, HY_BANDS - 1, HY_BANDS)[:, None]
    feats = np.concatenate([t, np.cos(bands * w), -np.sin(bands * w)], axis=0)
    pad = np.zeros((40 - feats.shape[0], T))
    feats = np.concatenate([feats, pad], axis=0)
    rev = (T - np.arange(T)) % T
    return (np.stack([feats, feats[:, rev]]).astype(np.float32),
            np.stack([t, t[:, rev]]).astype(np.float32))


@functools.lru_cache(maxsize=None)
def _rope_tables(T):
    rows = T // GRID_W
    pos_row = np.repeat(np.arange(rows), GRID_W).astype(np.float64)
    pos_col = np.tile(np.arange(GRID_W), rows).astype(np.float64)
    inv = 1.0 / (ROPE_THETA ** (np.arange(0, ROPE_AXIS_DIM, 2, dtype=np.float64) / ROPE_AXIS_DIM))
    lane = np.arange(LANES)
    d = lane % HEAD_DIM
    axis = d // ROPE_AXIS_DIM
    half = (d // (ROPE_AXIS_DIM // 2)) % 2
    f = d % (ROPE_AXIS_DIM // 2)
    pos = np.where(axis[None, :] == 0, pos_row[:, None], pos_col[:, None])
    ang = pos * inv[f][None, :]
    cos2 = np.cos(ang)
    sin2 = np.where(half[None, :] == 0, -np.sin(ang), np.sin(ang))
    return cos2.astype(np.float32), sin2.astype(np.float32)


@functools.lru_cache(maxsize=None)
def _spread_table(W):
    owner = np.arange(N_EXPERTS * W) // W
    hit = (np.arange(N_EXPERTS)[:, None] == owner[None, :]).astype(np.float32)
    return np.concatenate([32.0 * hit, hit], axis=0)


@functools.lru_cache(maxsize=None)
def _topk_tables(T, tb):
    nj = T // LANES
    per = tb // LANES
    r = np.arange(nj * N_EXPERTS)
    j, e = r // N_EXPERTS, r % N_EXPERTS
    same_e = e[:, None] == e[None, :]
    blk = j // per
    same_blk = blk[:, None] == blk[None, :]
    m_all = same_e & (j[None, :] < j[:, None])
    m_w = same_e & same_blk & (j[None, :] < j[:, None])
    m_b = same_e & same_blk
    m_a = same_e & (blk[None, :] < blk[:, None]) & ((j % per) == 0)[None, :]
    tri = np.triu(np.ones((LANES, LANES)))
    return tuple(np.asarray(m, np.float32) for m in (m_all, m_w, m_b, m_a, tri))


def _hy_inproj_kernel(h_ref, g_ref, wmt_ref, wcq_ref, ut_ref, cq_ref):
    xn = _rms(h_ref[...], g_ref[...]).astype(BF16)
    ut_ref[0] = _dot_nt(wmt_ref[...], xn)
    cq_ref[...] = _dot(xn, wcq_ref[...]).astype(BF16)


def _hy_inproj(h, g, wmt, wcq, B, T, tm):
    N, D = h.shape
    c3 = wmt.shape[0]
    nt = T // tm
    return pl.pallas_call(
        _hy_inproj_kernel,
        grid=(B, nt),
        in_specs=[
            pl.BlockSpec((tm, D), lambda b, i: (b * nt + i, 0)),
            pl.BlockSpec((1, D), lambda b, i: (0, 0)),
            pl.BlockSpec((c3, D), lambda b, i: (0, 0)),
            pl.BlockSpec((D, D_MEM), lambda b, i: (0, 0)),
        ],
        out_specs=[
            pl.BlockSpec((1, c3, tm), lambda b, i: (b, 0, i)),
            pl.BlockSpec((tm, D_MEM), lambda b, i: (b * nt + i, 0)),
        ],
        out_shape=[
            jax.ShapeDtypeStruct((B, c3, T), F32),
            jax.ShapeDtypeStruct((N, D_MEM), BF16),
        ],
        compiler_params=_cp("parallel", "parallel"),
        name="hy_inproj",
    )(h, g, wmt, wcq)


def _filt_mlp_kernel(feats_ref, w1t_ref, b1_ref, w2t_ref, b2_ref, fr_ref, h2_ref):
    fr = fr_ref[...]
    for d in range(2):
        a = jnp.dot(w1t_ref[...], feats_ref[d], precision=HI, preferred_element_type=F32)
        h1 = jnp.sin(fr * (a + b1_ref[...]))
        a = jnp.dot(w2t_ref[...], h1, precision=HI, preferred_element_type=F32)
        h2_ref[d] = jnp.sin(fr * (a + b2_ref[...]))


def _filt_taps_kernel(h2_ref, w3t_ref, t_ref, dl_ref, out_ref):
    T = h2_ref.shape[2]
    dl = dl_ref[...]
    hf = jnp.dot(w3t_ref[0, 0], h2_ref[0], precision=HI, preferred_element_type=F32) * jnp.exp(-t_ref[0] * dl)
    hb = jnp.dot(w3t_ref[0, 1], h2_ref[1], precision=HI, preferred_element_type=F32) * jnp.exp(-t_ref[1] * dl)
    nrm = jnp.sum(jnp.abs(hf) + jnp.abs(hb), axis=-1, keepdims=True) + 1e-6
    inv = 1.0 / nrm
    tap0 = lax.broadcasted_iota(I32, hf.shape, 1) == 0
    out_ref[0, :, :T] = ((hf + jnp.where(tap0, hb, 0.0)) * inv).astype(BF16)
    out_ref[0, :, T:] = (jnp.where(tap0, 0.0, hb) * inv).astype(BF16)


def _fwd_fft(x3, f1s, twr, twi, w2, cb, n1k):
    rhs = jnp.concatenate([x3[c].astype(BF16) for c in range(cb)], axis=1)
    a = _dot(f1s, rhs)
    ar, ai = a[:n1k], a[n1k:]
    tr = (ar * twr - ai * twi).astype(BF16)
    ti = (ar * twi + ai * twr).astype(BF16)
    lr = jnp.concatenate([tr[:, c * LANES:(c + 1) * LANES] for c in range(cb)], axis=0)
    li = jnp.concatenate([ti[:, c * LANES:(c + 1) * LANES] for c in range(cb)], axis=0)
    return _dot(jnp.concatenate([lr, li], axis=1), w2)


def _inv_fft(y, minv, twr, twi, g1, cb, n1k):
    b = _dot(y.astype(BF16), minv)
    br, bi = b[:, :LANES], b[:, LANES:]
    pr = (br * twr + bi * twi).astype(BF16)
    pi = (bi * twr - br * twi).astype(BF16)
    top = jnp.concatenate([pr[c * n1k:(c + 1) * n1k] for c in range(cb)], axis=1)
    bot = jnp.concatenate([pi[c * n1k:(c + 1) * n1k] for c in range(cb)], axis=1)
    rhs = jnp.concatenate([top, bot], axis=0)
    return _dot(g1, rhs)


def _filt_fft_kernel(taps_ref, f1_ref, twr_ref, twi_ref, w2_ref, k_ref, *, cb, n1k):
    k_ref[0] = _fwd_fft(taps_ref[0], f1_ref[...], twr_ref[...], twi_ref[...], w2_ref[...], cb, n1k)


def _hyena_filters_fft(p, T, cb):
    feats, t_rows = _filter_feats(T)
    C = p["w3t"].shape[2]
    tabs = _dft_tables(T, cb)
    n1k = 2 * T // LANES
    h2 = pl.pallas_call(
        _filt_mlp_kernel,
        out_shape=jax.ShapeDtypeStruct((2, HY_FILT, T), F32),
        compiler_params=pltpu.CompilerParams(vmem_limit_bytes=VMEM_LIMIT),
        name="hy_filt_mlp",
    )(jnp.asarray(feats), p["w1t"], p["b1"], p["w2t"], p["b2"], p["fr"])
    cbt = 64
    taps = pl.pallas_call(
        _filt_taps_kernel,
        grid=(2, C // cbt),
        in_specs=[
            pl.BlockSpec((2, HY_FILT, T), lambda o, c: (0, 0, 0)),
            pl.BlockSpec((1, 2, cbt, HY_FILT), lambda o, c: (o, 0, c, 0)),
            pl.BlockSpec((2, 1, T), lambda o, c: (0, 0, 0)),
            pl.BlockSpec((cbt, 1), lambda o, c: (c, 0)),
        ],
        out_specs=pl.BlockSpec((1, cbt, 2 * T), lambda o, c: (o, c, 0)),
        out_shape=jax.ShapeDtypeStruct((2, C, 2 * T), BF16),
        compiler_params=_cp("parallel", "parallel"),
        name="hy_filt_taps",
    )(h2, p["w3t"], jnp.asarray(t_rows), p["absdelta"])
    taps = taps.reshape(2, C, n1k, LANES)
    const = lambda shape: pl.BlockSpec(shape, lambda o, c: (0,) * len(shape))
    f32 = lambda k: jnp.asarray(tabs[k], F32)
    return pl.pallas_call(
        functools.partial(_filt_fft_kernel, cb=cb, n1k=n1k),
        grid=(2, C // cb),
        in_specs=[
            pl.BlockSpec((1, cb, n1k, LANES), lambda o, c: (o, c, 0, 0)),
            const((2 * n1k, n1k)), const((n1k, cb * LANES)), const((n1k, cb * LANES)),
            const((2 * LANES, 2 * LANES)),
        ],
        out_specs=pl.BlockSpec((1, cb * n1k, 2 * LANES), lambda o, c: (o, c, 0)),
        out_shape=jax.ShapeDtypeStruct((2, C * n1k, 2 * LANES), F32),
        compiler_params=_cp("parallel", "parallel"),
        name="hy_filt_fft",
    )(taps, f32("f1full").astype(BF16), f32("twr_l"), f32("twi_l"), f32("w2").astype(BF16))


def _time_neighbours(x):
    rows = x.shape[0]
    lane = lax.broadcasted_iota(I32, x.shape, 1)
    r = pltpu.roll(x, 1, 1)
    rr = pltpu.roll(r, 1, 0)
    prev = jnp.where(lane == 0, rr, r)
    r2 = pltpu.roll(x, LANES - 1, 1)
    rr2 = pltpu.roll(r2, rows - 1, 0)
    nxt = jnp.where(lane == LANES - 1, rr2, r2)
    return prev, nxt


def _hy_conv_kernel(x1_ref, x2_ref, v_ref, sw1_ref, sw2_ref, swv_ref, skip_ref, k_ref,
                    f1s_ref, twrl_ref, twil_ref, w2_ref, minv_ref, twrr_ref, twir_ref,
                    g1_ref, o_ref, *, cb, n1k, n1):
    rows = cb * n1
    shape2 = (rows, LANES)
    row = lax.broadcasted_iota(I32, shape2, 0)
    lane = lax.broadcasted_iota(I32, shape2, 1)
    first = (lane == 0) & (row % n1 == 0)
    last = (lane == LANES - 1) & (row % n1 == n1 - 1)

    def sconv(x_ref, sw_ref):
        x = x_ref[0].reshape(shape2)
        prev, nxt = _time_neighbours(x)
        prev = jnp.where(first, 0.0, prev)
        nxt = jnp.where(last, 0.0, nxt)
        w = [jnp.broadcast_to(sw_ref[j], (cb, n1, LANES)).reshape(shape2) for j in range(3)]
        return prev * w[0] + x * w[1] + nxt * w[2]

    z = sconv(v_ref, swv_ref)
    gates = (sconv(x1_ref, sw1_ref), sconv(x2_ref, sw2_ref))
    f1s, w2 = f1s_ref[...], w2_ref[...]
    for o in range(2):
        zf = _fwd_fft(z.reshape(cb, n1, LANES), f1s, twrl_ref[...], twil_ref[...], w2, cb, n1k)
        kk = k_ref[o]
        zr, zi = zf[:, :LANES], zf[:, LANES:]
        kr, ki = kk[:, :LANES], kk[:, LANES:]
        y = jnp.concatenate([zr * kr - zi * ki, zr * ki + zi * kr], axis=1)
        conv = _inv_fft(y, minv_ref[...], twrr_ref[...], twir_ref[...], g1_ref[...], cb, n1k)
        conv = jnp.concatenate([conv[:, c * LANES:(c + 1) * LANES] for c in range(cb)], axis=0)
        skip = jnp.broadcast_to(skip_ref[o], (cb, n1, LANES)).reshape(shape2)
        z = gates[o] * (conv + skip * z)
    o_ref[0] = z.reshape(cb, n1, LANES)


def _hy_conv(ut4, sw, skip, kfft, B, T, C, cb):
    n1k, n1 = 2 * T // LANES, T // LANES
    tabs = _dft_tables(T, cb)
    nct = C // cb
    bf = lambda k: jnp.asarray(tabs[k], F32).astype(BF16)
    f32 = lambda k: jnp.asarray(tabs[k], F32)
    const = lambda shape: pl.BlockSpec(shape, lambda c, b: (0,) * len(shape))
    ublk = lambda s: pl.BlockSpec((1, cb, n1, LANES), lambda c, b, s=s: (b, s * nct + c, 0, 0))
    wblk = lambda s: pl.BlockSpec((3, cb, 1, LANES), lambda c, b, s=s: (0, s * nct + c, 0, 0))
    return pl.pallas_call(
        functools.partial(_hy_conv_kernel, cb=cb, n1k=n1k, n1=n1),
        grid=(nct, B),
        in_specs=[
            ublk(0), ublk(1), ublk(2), wblk(0), wblk(1), wblk(2),
            pl.BlockSpec((2, cb, 1, LANES), lambda c, b: (0, c, 0, 0)),
            pl.BlockSpec((2, cb * n1k, 2 * LANES), lambda c, b: (0, c, 0)),
            const((2 * n1k, n1)), const((n1k, cb * LANES)), const((n1k, cb * LANES)),
            const((2 * LANES, 2 * LANES)), const((2 * LANES, 2 * LANES)),
            const((cb * n1k, LANES)), const((cb * n1k, LANES)), const((n1, 2 * n1k)),
        ],
        out_specs=pl.BlockSpec((1, cb, n1, LANES), lambda c, b: (b, c, 0, 0)),
        out_shape=jax.ShapeDtypeStruct((B, C, n1, LANES), F32),
        compiler_params=_cp("parallel", "arbitrary"),
        name="hy_conv",
    )(ut4, ut4, ut4, sw, sw, sw, skip, kfft,
      bf("f1s"), f32("twr_l"), f32("twi_l"), bf("w2"), bf("minv"),
      f32("twr_r"), f32("twi_r"), bf("g1"))


def _head_norm(x, bd, g):
    ss = _dot((x * x).astype(BF16), bd)
    return x * lax.rsqrt(ss * (1.0 / HEAD_DIM) + NORM_EPS) * g


def _rope(x, cos2, sin2):
    lane = lax.broadcasted_iota(I32, cos2.shape, 1)
    low = (lane // (ROPE_AXIS_DIM // 2)) % 2 == 0
    out = []
    for c in range(x.shape[1] // LANES):
        xc = x[:, c * LANES:(c + 1) * LANES]
        up = pltpu.roll(xc, LANES - 16, 1)
        dn = pltpu.roll(xc, 16, 1)
        out.append(xc * cos2 + jnp.where(low, up, dn) * sin2)
    return jnp.concatenate(out, axis=1)


def _at_inproj_kernel(h_ref, g_ref, w_ref, bdq_ref, bdk_ref, gq_ref, gk_ref, cos_ref, sin_ref,
                      qt_ref, kz_ref, vt_ref, cq_ref, *, dq, dk):
    xn = _rms(h_ref[...], g_ref[...]).astype(BF16)
    proj = _dot(xn, w_ref[...])
    q, k = proj[:, :dq], proj[:, dq:dq + dk]
    v, cq = proj[:, dq + dk:dq + 2 * dk], proj[:, dq + 2 * dk:]
    cos2, sin2 = cos_ref[...], sin_ref[...]
    qr = _rope(_head_norm(q, bdq_ref[...], gq_ref[...]), cos2, sin2)
    qt = (qr * (HEAD_DIM ** -0.5 * LOG2E)).T.astype(BF16)
    for p in range(dq // LANES):
        qt_ref[0, p] = qt[p * LANES:(p + 1) * LANES]
    kr = _rope(_head_norm(k, bdk_ref[...], gk_ref[...]), cos2, sin2)
    vt = v.T.astype(BF16)
    zv = jnp.zeros((HEAD_DIM, vt.shape[1]), BF16)
    lane = lax.broadcasted_iota(I32, (k.shape[0], LANES), 1)
    for kv in range(N_KV_HEADS):
        rows = vt[kv * HEAD_DIM:(kv + 1) * HEAD_DIM]
        vt_ref[0, kv, 0] = jnp.concatenate([rows, zv], axis=0)
        vt_ref[0, kv, 1] = jnp.concatenate([zv, rows], axis=0)
        pair = kr[:, (kv // 2) * LANES:(kv // 2 + 1) * LANES]
        own = jnp.where((lane < HEAD_DIM) == (kv % 2 == 0), pair, 0.0)
        other = pltpu.roll(own, HEAD_DIM, 1)
        lo, hi = (own, other) if kv % 2 == 0 else (other, own)
        kz_ref[0, kv, 0] = lo.astype(BF16)
        kz_ref[0, kv, 1] = hi.astype(BF16)
    cq_ref[...] = cq.astype(BF16)


def _at_inproj(h, g, w, gq, gk, B, T, tm):
    N, D = h.shape
    dq, dk = GQA_GROUP * N_KV_HEADS * HEAD_DIM, N_KV_HEADS * HEAD_DIM
    nt = T // tm
    cos2, sin2 = _rope_tables(T)
    bd = lambda n: jnp.asarray(np.kron(np.eye(n // HEAD_DIM), np.ones((HEAD_DIM, HEAD_DIM))), F32).astype(BF16)
    const = lambda shape: pl.BlockSpec(shape, lambda b, i: (0,) * len(shape))
    return pl.pallas_call(
        functools.partial(_at_inproj_kernel, dq=dq, dk=dk),
        grid=(B, nt),
        in_specs=[
            pl.BlockSpec((tm, D), lambda b, i: (b * nt + i, 0)),
            const((1, D)), const(w.shape), const((dq, dq)), const((dk, dk)),
            const((1, dq)), const((1, dk)),
            pl.BlockSpec((tm, LANES), lambda b, i: (i, 0)),
            pl.BlockSpec((tm, LANES), lambda b, i: (i, 0)),
        ],
        out_specs=[
            pl.BlockSpec((1, dq // LANES, LANES, tm), lambda b, i: (b, 0, 0, i)),
            pl.BlockSpec((1, N_KV_HEADS, 2, tm, LANES), lambda b, i: (b, 0, 0, i, 0)),
            pl.BlockSpec((1, N_KV_HEADS, 2, LANES, tm), lambda b, i: (b, 0, 0, 0, i)),
            pl.BlockSpec((tm, D_MEM), lambda b, i: (b * nt + i, 0)),
        ],
        out_shape=[
            jax.ShapeDtypeStruct((B, dq // LANES, LANES, T), BF16),
            jax.ShapeDtypeStruct((B, N_KV_HEADS, 2, T, LANES), BF16),
            jax.ShapeDtypeStruct((B, N_KV_HEADS, 2, LANES, T), BF16),
            jax.ShapeDtypeStruct((N, D_MEM), BF16),
        ],
        compiler_params=_cp("parallel", "parallel"),
        name="at_inproj",
    )(h, g, w, bd(dq), bd(dk), gq, gk, jnp.asarray(cos2), jnp.asarray(sin2))


def _flash_kernel(qt_ref, ka_ref, kb_ref, vta_ref, vtb_ref, o_ref, m_sc, l_sc, acc_sc, s_sc, p_sc):
    ki = pl.program_id(3)

    @pl.when(ki == 0)
    def _():
        m_sc[...] = jnp.full(m_sc.shape, -jnp.inf, F32)
        l_sc[...] = jnp.zeros(l_sc.shape, F32)
        acc_sc[...] = jnp.zeros(acc_sc.shape, F32)

    qt = qt_ref[0, 0]
    tk, tq = s_sc.shape[1], s_sc.shape[2]
    sub = 8
    ck = 2 * sub
    pv, alphas = [], []
    for idx, k_ref in enumerate((ka_ref, kb_ref)):
        s_sc[idx] = _dot(k_ref[0, 0, 0], qt)
    for idx, vt_ref in enumerate((vta_ref, vtb_ref)):
        mx = s_sc[idx, 0:sub, :]
        for c in range(1, tk // sub):
            mx = jnp.maximum(mx, s_sc[idx, c * sub:(c + 1) * sub, :])
        m_prev = m_sc[idx]
        m_new = jnp.maximum(m_prev, jnp.max(mx, axis=0, keepdims=True))
        alpha = jnp.exp2(m_prev - m_new)
        lsum = jnp.zeros((sub, tq), F32)
        for c in range(tk // ck):
            p = jnp.exp2(s_sc[idx, c * ck:(c + 1) * ck, :] - m_new)
            lsum = lsum + p[:sub] + p[sub:]
            p_sc[idx, c * ck:(c + 1) * ck, :] = p.astype(BF16)
        l_sc[idx] = alpha * l_sc[idx] + jnp.sum(lsum, axis=0, keepdims=True)
        m_sc[idx] = m_new
        pv.append(_dot(vt_ref[0, 0, 0], p_sc[idx]))
        alphas.append(alpha)
    row = lax.broadcasted_iota(I32, acc_sc.shape, 0)
    low = row < HEAD_DIM
    acc_sc[...] = acc_sc[...] * jnp.where(low, alphas[0], alphas[1]) + pv[0] + pv[1]

    @pl.when(ki == pl.num_programs(3) - 1)
    def _():
        o_ref[0] = acc_sc[...] / jnp.where(low, l_sc[0], l_sc[1])


def _flash(qt, kz, vtz, B, T, tq, tk):
    npair = qt.shape[1]
    nq, nk = T // tq, T // tk
    kv_of = lambda p, j: (2 * p + j) // GQA_GROUP
    return pl.pallas_call(
        _flash_kernel,
        grid=(B, npair, nq, nk),
        in_specs=[
            pl.BlockSpec((1, 1, LANES, tq), lambda b, p, i, k: (b, p, 0, i)),
            pl.BlockSpec((1, 1, 1, tk, LANES), lambda b, p, i, k: (b, kv_of(p, 0), 0, k, 0)),
            pl.BlockSpec((1, 1, 1, tk, LANES), lambda b, p, i, k: (b, kv_of(p, 1), 1, k, 0)),
            pl.BlockSpec((1, 1, 1, LANES, tk), lambda b, p, i, k: (b, kv_of(p, 0), 0, 0, k)),
            pl.BlockSpec((1, 1, 1, LANES, tk), lambda b, p, i, k: (b, kv_of(p, 1), 1, 0, k)),
        ],
        out_specs=pl.BlockSpec((1, LANES, tq), lambda b, p, i, k: (b, p, i)),
        out_shape=jax.ShapeDtypeStruct((B, npair * LANES, T), F32),
        scratch_shapes=[
            pltpu.VMEM((2, 1, tq), F32), pltpu.VMEM((2, 1, tq), F32), pltpu.VMEM((LANES, tq), F32),
            pltpu.VMEM((2, tk, tq), F32), pltpu.VMEM((2, tk, tq), BF16),
        ],
        compiler_params=_cp("parallel", "parallel", "parallel", "arbitrary"),
        name="flash_gqa",
    )(qt, kz, kz, vtz, vtz)


def _memkv_kernel(mem_ref, g_ref, wkvt_ref, wv_ref, mkt_ref, mv_ref):
    mn = _rms(mem_ref[0], g_ref[...]).astype(BF16)
    kt = _dot_nt(wkvt_ref[...], mn) * (HEAD_DIM ** -0.5)
    v = _dot(mn, wv_ref[...])
    row = lax.broadcasted_iota(I32, kt.shape, 0)
    col = lax.broadcasted_iota(I32, v.shape, 1)
    for hd in range(MEM_HEADS):
        mkt_ref[0, hd] = jnp.where(row // HEAD_DIM == hd, kt, 0.0).astype(BF16)
        mv_ref[0, hd] = jnp.where(col // HEAD_DIM == hd, v, 0.0).astype(BF16)


def _memkv(mem, g, wkt, wv):
    B, M, D = mem.shape
    return pl.pallas_call(
        _memkv_kernel,
        grid=(B,),
        in_specs=[
            pl.BlockSpec((1, M, D), lambda b: (b, 0, 0)),
            pl.BlockSpec((1, D), lambda b: (0, 0)),
            pl.BlockSpec((D_MEM, D), lambda b: (0, 0)),
            pl.BlockSpec((D, D_MEM), lambda b: (0, 0)),
        ],
        out_specs=[
            pl.BlockSpec((1, MEM_HEADS, D_MEM, M), lambda b: (b, 0, 0, 0)),
            pl.BlockSpec((1, MEM_HEADS, M, D_MEM), lambda b: (b, 0, 0, 0)),
        ],
        out_shape=[
            jax.ShapeDtypeStruct((B, MEM_HEADS, D_MEM, M), BF16),
            jax.ShapeDtypeStruct((B, MEM_HEADS, M, D_MEM), BF16),
        ],
        compiler_params=_cp("parallel"),
        name="mem_kv",
    )(mem, g, wkt, wv)


def _cross_attn(cq, mkt_ref, mv_ref):
    acc = jnp.zeros(cq.shape, F32)
    for hd in range(MEM_HEADS):
        s = _dot(cq, mkt_ref[0, hd])
        p = jnp.exp(s - jnp.max(s, axis=-1, keepdims=True))
        p = p / jnp.sum(p, axis=-1, keepdims=True)
        acc = acc + _dot(p.astype(BF16), mv_ref[0, hd])
    return acc


def _outproj_kernel(h_ref, main_ref, cq_ref, mkt_ref, mv_ref, wm_ref, wc_ref, o_ref):
    if len(main_ref.shape) == 3:
        main = main_ref[0].T
    else:
        main = jnp.concatenate([main_ref[0, :, j, :].T for j in range(main_ref.shape[2])], axis=0)
    main = main.astype(BF16)
    cross = _cross_attn(cq_ref[...], mkt_ref, mv_ref).astype(BF16)
    o_ref[...] = h_ref[...] + _dot(main, wm_ref[...]) + _dot(cross, wc_ref[...])


def _outproj(h, main, cq, mkt, mv, wm, wc, B, T, tm):
    N, D = h.shape
    C = wm.shape[0]
    M = mkt.shape[-1]
    nt = T // tm
    if main.ndim == 3:
        mspec = pl.BlockSpec((1, C, tm), lambda b, i: (b, 0, i))
    else:
        mspec = pl.BlockSpec((1, C, tm // LANES, LANES), lambda b, i: (b, 0, i, 0))
    return pl.pallas_call(
        _outproj_kernel,
        grid=(B, nt),
        in_specs=[
            pl.BlockSpec((tm, D), lambda b, i: (b * nt + i, 0)),
            mspec,
            pl.BlockSpec((tm, D_MEM), lambda b, i: (b * nt + i, 0)),
            pl.BlockSpec((1, MEM_HEADS, D_MEM, M), lambda b, i: (b, 0, 0, 0)),
            pl.BlockSpec((1, MEM_HEADS, M, D_MEM), lambda b, i: (b, 0, 0, 0)),
            pl.BlockSpec((C, D), lambda b, i: (0, 0)),
            pl.BlockSpec((D_MEM, D), lambda b, i: (0, 0)),
        ],
        out_specs=pl.BlockSpec((tm, D), lambda b, i: (b * nt + i, 0)),
        out_shape=jax.ShapeDtypeStruct((N, D), F32),
        compiler_params=_cp("parallel", "parallel"),
        name="outproj",
    )(h, main, cq, mkt, mv, wm, wc)


def _router_kernel(h_ref, g_ref, wr_ref, wrt_ref, xn_ref, gsp_ref, afft_ref):
    xn = _rms(h_ref[...], g_ref[...]).astype(BF16)
    xn_ref[...] = xn
    lg = _dot(xn, wr_ref[...])
    lane = lax.broadcasted_iota(I32, lg.shape, 1)
    lg = jnp.where(lane < N_EXPERTS, lg, -jnp.inf)
    p = jnp.exp(lg - jnp.max(lg, axis=-1, keepdims=True))
    aff = p / jnp.sum(p, axis=-1, keepdims=True)
    hi = aff.astype(BF16).astype(F32)
    mid = (aff - hi).astype(BF16).astype(F32)
    lo = (aff - hi - mid).astype(BF16).astype(F32)
    gsp_ref[...] = (hi + pltpu.roll(mid, N_EXPERTS, 1) + pltpu.roll(lo, 2 * N_EXPERTS, 1)).astype(BF16)
    lt = _dot_nt(wrt_ref[...], xn)
    pt = jnp.exp(lt - jnp.max(lt, axis=0, keepdims=True))
    pt = pt / jnp.sum(pt, axis=0, keepdims=True)
    for j in range(lt.shape[1] // LANES):
        afft_ref[0, j] = pt[:, j * LANES:(j + 1) * LANES]


def _router(h, g, wr, wrt, B, T, tm):
    N, D = h.shape
    nt = T // tm
    return pl.pallas_call(
        _router_kernel,
        grid=(B, nt),
        in_specs=[
            pl.BlockSpec((tm, D), lambda b, i: (b * nt + i, 0)),
            pl.BlockSpec((1, D), lambda b, i: (0, 0)),
            pl.BlockSpec((D, LANES), lambda b, i: (0, 0)),
            pl.BlockSpec((N_EXPERTS, D), lambda b, i: (0, 0)),
        ],
        out_specs=[
            pl.BlockSpec((tm, D), lambda b, i: (b * nt + i, 0)),
            pl.BlockSpec((tm, LANES), lambda b, i: (b * nt + i, 0)),
            pl.BlockSpec((1, tm // LANES, N_EXPERTS, LANES), lambda b, i: (b, i, 0, 0)),
        ],
        out_shape=[
            jax.ShapeDtypeStruct((N, D), BF16),
            jax.ShapeDtypeStruct((N, LANES), BF16),
            jax.ShapeDtypeStruct((B, T // LANES, N_EXPERTS, LANES), F32),
        ],
        compiler_params=_cp("parallel", "parallel"),
        name="moe_router",
    )(h, g, wr, wrt)


def _topk_kernel(aff_ref, mall_ref, mw_ref, mb_ref, ma_ref, tri_ref, pos_ref, aoff_ref, pad_ref,
                 *, cap, nj):
    E = N_EXPERTS
    aff3 = aff_ref[0]

    def count(mask3):
        per = jnp.sum(mask3.astype(F32), axis=0)
        return jnp.broadcast_to(jnp.sum(per, axis=-1, keepdims=True), (E, LANES))

    def step(i, thr):
        cand = thr | lax.shift_left(jnp.int32(1), 30 - i)
        ok = count(aff3 >= pltpu.bitcast(cand, F32)[None]) >= cap
        return jnp.where(ok, cand, thr)

    thr = lax.fori_loop(0, 31, step, jnp.zeros((E, LANES), I32))
    thr = pltpu.bitcast(thr, F32)
    gt3 = aff3 > thr[None]
    eq3 = aff3 == thr[None]
    need = cap - count(gt3)

    ones = jnp.ones((LANES, LANES), BF16)
    tri = tri_ref[...]

    def prefix(mask2):
        mb = mask2.astype(BF16)
        incl = _dot(mb, tri)
        tot = _dot(mb, ones)
        return incl - mask2, tot

    eq2 = eq3.reshape(nj * E, LANES).astype(F32)
    ex, tot = prefix(eq2)
    eq_rank = ex + _dot(mall_ref[...], tot.astype(BF16))
    need2 = jnp.broadcast_to(need[None], (nj, E, LANES)).reshape(nj * E, LANES)
    sel = jnp.where((gt3.reshape(nj * E, LANES)) | ((eq2 > 0) & (eq_rank < need2)), 1.0, 0.0)

    ex, tot = prefix(sel)
    totb = tot.astype(BF16)
    within = _dot(mw_ref[...], totb)
    cnt = _dot(mb_ref[...], totb)
    padded = (cnt.astype(I32) + (BF16_SUBLANES - 1)) & ~(BF16_SUBLANES - 1)
    aoff = _dot(ma_ref[...], padded.astype(F32).astype(BF16))
    pos = (aoff + within + ex).astype(I32)
    pos_ref[0] = jnp.where(sel > 0, pos, -1).reshape(nj, E, LANES)
    aoff_ref[0] = aoff.astype(I32).reshape(nj, E, LANES)
    pad_ref[0] = padded.reshape(nj, E, LANES)


def _topk(aff4, T, tb):
    B, nj = aff4.shape[0], aff4.shape[1]
    cap = EC_CAPACITY_FACTOR * T // N_EXPERTS
    tabs = [jnp.asarray(m, F32).astype(BF16) for m in _topk_tables(T, tb)]
    R = nj * N_EXPERTS
    blk = pl.BlockSpec((1, nj, N_EXPERTS, LANES), lambda b: (b, 0, 0, 0))
    const = lambda shape: pl.BlockSpec(shape, lambda b: (0,) * len(shape))
    out = jax.ShapeDtypeStruct((B, nj, N_EXPERTS, LANES), I32)
    return pl.pallas_call(
        functools.partial(_topk_kernel, cap=cap, nj=nj),
        grid=(B,),
        in_specs=[blk, const((R, R)), const((R, R)), const((R, R)), const((R, R)), const((LANES, LANES))],
        out_specs=[blk, blk, blk],
        out_shape=[out, out, out],
        compiler_params=_cp("parallel"),
        name="moe_topk",
    )(aff4, *tabs)


def _compress_kernel(aoff_ref, nch_ref, nmax_ref, xn_ref, gsp_ref, pos_ref, xg_ref, gs_ref,
                     *, nblk, tb, W, eg):
    b, grp, blk = pl.program_id(0), pl.program_id(1), pl.program_id(2)

    @pl.when(blk == 0)
    def _():
        xg_ref[...] = jnp.zeros(xg_ref.shape, BF16)
        gs_ref[...] = jnp.zeros(gs_ref.shape, F32)

    base = (b * nblk + blk) * N_EXPERTS + grp * eg
    iota_s = lax.broadcasted_iota(I32, (W, LANES), 0)

    def chunk(c, carry):
        pieces = []
        for i in range(eg):
            a = aoff_ref[base + i] + c * W
            e = grp * eg + i
            g = [pos_ref[0, jj, pl.ds(e, 1), :] - a == iota_s for jj in range(tb // LANES)]
            pieces.append(jnp.where(jnp.concatenate(g, axis=1), 1.0, 0.0).astype(BF16))
        lhs = jnp.concatenate(pieces, axis=0)
        res = _dot(lhs, xn_ref[...]).astype(BF16)
        resg = _dot(lhs, gsp_ref[...])
        for i in range(eg):
            @pl.when(c < nch_ref[base + i])
            def _():
                a = pl.multiple_of(aoff_ref[base + i] + c * W, BF16_SUBLANES)
                xg_ref[0, i, pl.ds(a, W), :] = res[i * W:(i + 1) * W]
                gs_ref[0, i, pl.ds(a, W), :] = resg[i * W:(i + 1) * W]
        return carry

    lax.fori_loop(0, nmax_ref[b * nblk + blk], chunk, 0)


def _compress(aoff, nch, nmax, xn, gsp, pos4, B, T, tb, rows, W):
    N, D = xn.shape
    nblk = T // tb
    eg = 4
    return pl.pallas_call(
        functools.partial(_compress_kernel, nblk=nblk, tb=tb, W=W, eg=eg),
        grid_spec=pltpu.PrefetchScalarGridSpec(
            num_scalar_prefetch=3,
            grid=(B, N_EXPERTS // eg, nblk),
            in_specs=[
                pl.BlockSpec((tb, D), lambda b, g, k, *_: (b * nblk + k, 0)),
                pl.BlockSpec((tb, LANES), lambda b, g, k, *_: (b * nblk + k, 0)),
                pl.BlockSpec((1, tb // LANES, N_EXPERTS, LANES), lambda b, g, k, *_: (b, k, 0, 0)),
            ],
            out_specs=[
                pl.BlockSpec((1, eg, rows, D), lambda b, g, k, *_: (b, g, 0, 0)),
                pl.BlockSpec((1, eg, rows, LANES), lambda b, g, k, *_: (b, g, 0, 0)),
            ],
        ),
        out_shape=[
            jax.ShapeDtypeStruct((B, N_EXPERTS, rows, D), BF16),
            jax.ShapeDtypeStruct((B, N_EXPERTS, rows, LANES), F32),
        ],
        compiler_params=_cp("parallel", "parallel", "arbitrary"),
        name="moe_compress",
    )(aoff, nch, nmax, xn, gsp, pos4)


def _ffn_kernel(x_ref, gs_ref, wg_ref, wu_ref, wd_ref, y_ref, *, live):
    e = pl.program_id(0)
    x = x_ref[0, 0, :live]
    g = _dot(x, wg_ref[0, 0].astype(BF16))
    u = _dot(x, wu_ref[0, 0].astype(BF16))
    hid = (g * (1.0 / (1.0 + jnp.exp(-g))) * u).astype(BF16)
    gs = gs_ref[0, 0, :live]
    lane = lax.broadcasted_iota(I32, gs.shape, 1)
    mine = (lane == e) | (lane == e + N_EXPERTS) | (lane == e + 2 * N_EXPERTS)
    gate = jnp.sum(jnp.where(mine, gs, 0.0), axis=-1, keepdims=True)
    y_ref[0, 0, :live] = (_dot(hid, wd_ref[0, 0].astype(BF16)) * gate).astype(BF16)
    y_ref[0, 0, live:] = jnp.zeros((y_ref.shape[2] - live, y_ref.shape[3]), BF16)


def _ffn(xg, gs, wg, wu, wd, layer, live):
    B, E, rows, D = xg.shape
    F = wg.shape[-1]
    return pl.pallas_call(
        functools.partial(_ffn_kernel, live=live),
        grid=(E, B),
        in_specs=[
            pl.BlockSpec((1, 1, rows, D), lambda e, b: (b, e, 0, 0)),
            pl.BlockSpec((1, 1, rows, LANES), lambda e, b: (b, e, 0, 0)),
            pl.BlockSpec((1, 1, D, F), lambda e, b: (layer, e, 0, 0)),
            pl.BlockSpec((1, 1, D, F), lambda e, b: (layer, e, 0, 0)),
            pl.BlockSpec((1, 1, F, D), lambda e, b: (layer, e, 0, 0)),
        ],
        out_specs=pl.BlockSpec((1, 1, rows, D), lambda e, b: (b, e, 0, 0)),
        out_shape=jax.ShapeDtypeStruct((B, E, rows, D), BF16),
        compiler_params=_cp("parallel", "arbitrary"),
        name="moe_ffn",
    )(xg, gs, wg, wu, wd)


def _expand_kernel(aoff_ref, nch_ref, nmax_ref, h_ref, post_ref, spread_ref, slot1_ref, y_ref, o_ref,
                   *, nblk, tb, W, rows):
    b, blk = pl.program_id(0), pl.program_id(2)
    base = (b * nblk + blk) * N_EXPERTS
    ngrp = N_EXPERTS * W // LANES
    lane = lax.broadcasted_iota(I32, (1, LANES), 1)
    rel = _dot(post_ref[...], spread_ref[...]) - slot1_ref[...]
    never = jnp.float32(-2.0 ** 20)

    def chunk(c, acc):
        tgt, wins = [], []
        for e in range(N_EXPERTS):
            a = aoff_ref[base + e] + c * W
            tgt.append(jnp.where(c < nch_ref[base + e], a.astype(F32), never))
            a_in = pl.multiple_of(jnp.minimum(a, rows - W), BF16_SUBLANES)
            wins.append(y_ref[0, e, pl.ds(a_in, W), :])
        cols = []
        for g in range(ngrp):
            first, last = (LANES * g) // W, (LANES * g + LANES - 1) // W
            t = jnp.full((1, LANES), tgt[last], F32)
            for e in range(last - 1, first - 1, -1):
                t = jnp.where(lane < (e + 1) * W - LANES * g, tgt[e], t)
            hit = rel[:, g * LANES:(g + 1) * LANES] == t
            cols.append(jnp.where(hit, 1.0, 0.0).astype(BF16))
        return acc + _dot(jnp.concatenate(cols, axis=1), jnp.concatenate(wins, axis=0))

    acc = lax.fori_loop(0, nmax_ref[b * nblk + blk], chunk, jnp.zeros(o_ref.shape, F32))
    o_ref[...] = h_ref[...] + acc


def _expand(aoff, nch, nmax, h, post, y, B, T, tb, dw, W):
    N, D = h.shape
    rows = y.shape[2]
    nblk = T // tb
    return pl.pallas_call(
        functools.partial(_expand_kernel, nblk=nblk, tb=tb, W=W, rows=rows),
        grid_spec=pltpu.PrefetchScalarGridSpec(
            num_scalar_prefetch=3,
            grid=(B, D // dw, nblk),
            in_specs=[
                pl.BlockSpec((tb, dw), lambda b, d, k, *_: (b * nblk + k, d)),
                pl.BlockSpec((tb, 2 * N_EXPERTS), lambda b, d, k, *_: (b * nblk + k, 0)),
                pl.BlockSpec((2 * N_EXPERTS, N_EXPERTS * W), lambda b, d, k, *_: (0, 0)),
                pl.BlockSpec((1, N_EXPERTS * W), lambda b, d, k, *_: (0, 0)),
                pl.BlockSpec((1, N_EXPERTS, rows, dw), lambda b, d, k, *_: (b, 0, 0, d),
                             pipeline_mode=pl.Buffered(1)),
            ],
            out_specs=pl.BlockSpec((tb, dw), lambda b, d, k, *_: (b * nblk + k, d)),
        ),
        out_shape=jax.ShapeDtypeStruct((N, D), F32),
        compiler_params=_cp("parallel", "parallel", "arbitrary"),
        name="moe_expand",
    )(aoff, nch, nmax, h, post, jnp.asarray(_spread_table(W), F32).astype(BF16),
      jnp.asarray((np.arange(N_EXPERTS * W) % W + 1).reshape(1, -1), F32), y)


def _moe(h, g, wr, wg, wu, wd, layer, B, T, tm):
    N, D = h.shape
    tb = min(512, T)
    W = 96
    nblk = T // tb
    cap = EC_CAPACITY_FACTOR * T // N_EXPERTS
    rows = -(-(cap + nblk * (BF16_SUBLANES - 1)) // LANES) * LANES + LANES
    dw = 256
    wr_pad = jnp.zeros((D, LANES), BF16).at[:, :N_EXPERTS].set(wr.astype(BF16))
    xn, gsp, aff4 = _router(h, g, wr_pad, wr.T.astype(BF16), B, T, tm)
    pos4, aoff4, pad4 = _topk(aff4, T, tb)
    per = tb // LANES
    aoff = aoff4[:, ::per, :, 0].reshape(-1)
    nch4 = (pad4[:, ::per, :, 0] + (W - 1)) // W
    nch = nch4.reshape(-1)
    nmax = jnp.max(nch4, axis=-1).reshape(-1)
    post = jnp.transpose(pos4, (0, 1, 3, 2)).reshape(N, N_EXPERTS) + 1
    post = jnp.concatenate([post >> 5, post & 31], axis=1).astype(BF16)
    xg, gs = _compress(aoff, nch, nmax, xn, gsp, pos4, B, T, tb, rows, W)
    y = _ffn(xg, gs, wg, wu, wd, layer, rows - LANES)
    return _expand(aoff, nch, nmax, h, post, y, B, T, tb, 2 * dw, W)


def _final_norm_kernel(h_ref, g_ref, o_ref):
    o_ref[...] = _rms(h_ref[...], g_ref[...])


def _final_norm(h, g, tm):
    N, D = h.shape
    return pl.pallas_call(
        _final_norm_kernel,
        grid=(N // tm,),
        in_specs=[pl.BlockSpec((tm, D), lambda i: (i, 0)), pl.BlockSpec((1, D), lambda i: (0, 0))],
        out_specs=pl.BlockSpec((tm, D), lambda i: (i, 0)),
        out_shape=jax.ShapeDtypeStruct((N, D), F32),
        compiler_params=_cp("parallel"),
        name="final_norm",
    )(h, g)


def kernel(x, mem, mix_norm_g, ffn_norm_g, mem_norm_g, final_norm_g, w_mem_kv, w_out, hy_w_in, hy_short_w, hy_filt_w1, hy_filt_b1, hy_filt_w2, hy_filt_b2, hy_filt_w3, hy_filt_freq, hy_skip, at_w_in, at_q_norm_g, at_k_norm_g, router_w, exp_w_gate, exp_w_up, exp_w_down):
    B, T, D = x.shape
    depth = mix_norm_g.shape[0]
    C = hy_skip.shape[-1]
    tm = min(512, T)
    cb = 16
    n1 = T // LANES
    h = x.reshape(B * T, D)
    row = lambda v: v.reshape(1, -1).astype(F32)
    lanes = lambda v: jnp.broadcast_to(v[..., None, None], v.shape + (1, LANES)).astype(F32)
    max_decay = math.log(HY_DECAY_TARGET) / HY_FAST_PCT
    min_decay = math.log(HY_DECAY_TARGET) / HY_SLOW_PCT
    absdelta = jnp.asarray(np.abs(np.linspace(min_decay, max_decay, C)).astype(np.float32)).reshape(C, 1)

    for i in range(depth):
        j = i // 2
        wo = w_out[i].astype(BF16)
        wkv = w_mem_kv[i].astype(BF16)
        mkt, mv = _memkv(mem, row(mem_norm_g), wkv[:, :D_MEM].T, wkv[:, D_MEM:])
        if i % 2 == 0:
            w_in = hy_w_in[j].astype(BF16)
            ut, cq = _hy_inproj(h, row(mix_norm_g[i]), w_in[:, :3 * C].T, w_in[:, 3 * C:], B, T, tm)
            filt = dict(
                w1t=jnp.zeros((HY_FILT, 40), F32).at[:, :hy_filt_w1.shape[1]].set(hy_filt_w1[j].T),
                b1=hy_filt_b1[j].reshape(-1, 1), w2t=hy_filt_w2[j].T, b2=hy_filt_b2[j].reshape(-1, 1),
                fr=hy_filt_freq[j].reshape(-1, 1),
                w3t=hy_filt_w3[j].T.reshape(2, 2, C, HY_FILT), absdelta=absdelta,
            )
            kfft = _hyena_filters_fft(filt, T, cb)
            zt = _hy_conv(ut.reshape(B, 3 * C, n1, LANES), lanes(hy_short_w[j]), lanes(hy_skip[j]),
                          kfft, B, T, C, cb)
            h = _outproj(h, zt, cq, mkt, mv, wo[:C], wo[C:], B, T, min(1024, T))
        else:
            rep = lambda v, n: jnp.tile(v.astype(F32), n).reshape(1, -1)
            qt, kz, vtz, cq = _at_inproj(h, row(mix_norm_g[i]), at_w_in[j].astype(BF16),
                                         rep(at_q_norm_g[j], GQA_GROUP * N_KV_HEADS),
                                         rep(at_k_norm_g[j], N_KV_HEADS), B, T, tm)
            main_t = _flash(qt, kz, vtz, B, T, min(2048, T), min(1024, T))
            h = _outproj(h, main_t, cq, mkt, mv, wo[:C], wo[C:], B, T, tm)
        h = _moe(h, row(ffn_norm_g[i]), router_w[i], exp_w_gate, exp_w_up, exp_w_down, i, B, T, tm)
    return _final_norm(h, row(final_norm_g), tm).reshape(B, T, D)
```

```python
import functools
import math

import numpy as np
import jax
import jax.numpy as jnp
from jax import lax
from jax.experimental import pallas as pl
from jax.experimental.pallas import tpu as pltpu

F32 = jnp.float32
BF16 = jnp.bfloat16
I32 = jnp.int32

HEAD_DIM = 64
MEM_HEADS = 4
D_MEM = MEM_HEADS * HEAD_DIM
N_KV_HEADS = 4
GQA_GROUP = 3
GRID_W = 64
ROPE_THETA = 10000.0
ROPE_AXIS_DIM = HEAD_DIM // 2
HY_BANDS = 16
HY_FILT = 64
HY_DECAY_TARGET = 1e-2
HY_FAST_PCT = 0.3
HY_SLOW_PCT = 1.5
N_EXPERTS = 16
EC_CAPACITY_FACTOR = 2
NORM_EPS = 1e-6

LANES = 128
BF16_SUBLANES = 16
VMEM_LIMIT = 56 * 1024 * 1024
HI = lax.Precision.HIGHEST
LOG2E = 1.4426950408889634


def _cp(*sem):
    return pltpu.CompilerParams(dimension_semantics=sem, vmem_limit_bytes=VMEM_LIMIT)


def _rms(x, g):
    ms = jnp.mean(x * x, axis=-1, keepdims=True)
    return x * lax.rsqrt(ms + NORM_EPS) * g


def _dot(a, b):
    return jnp.dot(a, b, preferred_element_type=F32)


def _dot_nt(a, b):
    return lax.dot_general(a, b, (((1,), (1,)), ((), ())), preferred_element_type=F32)


@functools.lru_cache(maxsize=None)
def _dft_tables(T, cb):
    nf = 2 * T
    n1k = nf // LANES
    n1 = T // LANES
    k1 = np.arange(n1k)[:, None]
    a1 = 2 * np.pi * k1 * np.arange(n1)[None, :] / n1k
    f1s = np.concatenate([np.cos(a1), -np.sin(a1)], axis=0)
    a1f = 2 * np.pi * k1 * np.arange(n1k)[None, :] / n1k
    f1full = np.concatenate([np.cos(a1f), -np.sin(a1f)], axis=0)
    tw = 2 * np.pi * k1 * np.arange(LANES)[None, :] / nf
    twr, twi = np.cos(tw), -np.sin(tw)
    a2 = 2 * np.pi * np.arange(LANES)[:, None] * np.arange(LANES)[None, :] / LANES
    cr, ci = np.cos(a2), -np.sin(a2)
    w2 = np.block([[cr, ci], [-ci, cr]])
    minv = np.block([[cr, -ci], [ci, cr]])
    a3 = 2 * np.pi * np.arange(n1)[:, None] * np.arange(n1k)[None, :] / n1k
    g1 = np.concatenate([np.cos(a3), -np.sin(a3)], axis=1) / nf
    return dict(
        f1s=f1s, f1full=f1full, w2=w2, minv=minv, g1=g1,
        twr_l=np.tile(twr, (1, cb)), twi_l=np.tile(twi, (1, cb)),
        twr_r=np.tile(twr, (cb, 1)), twi_r=np.tile(twi, (cb, 1)),
    )


@functools.lru_cache(maxsize=None)
def _filter_feats(T):
    t = np.linspace(0.0, 1.0, T)[None, :]
    w = (2.0 * np.pi) * np.arange(T)[None, :] / T
    bands = np.linspace(1e-4, HY_BANDS - 1, HY_BANDS)[:, None]
    feats = np.concatenate([t, np.cos(bands * w), -np.sin(bands * w)], axis=0)
    pad = np.zeros((40 - feats.shape[0], T))
    feats = np.concatenate([feats, pad], axis=0)
    rev = (T - np.arange(T)) % T
    return (np.stack([feats, feats[:, rev]]).astype(np.float32),
            np.stack([t, t[:, rev]]).astype(np.float32))


@functools.lru_cache(maxsize=None)
def _rope_tables(T):
    rows = T // GRID_W
    pos_row = np.repeat(np.arange(rows), GRID_W).astype(np.float64)
    pos_col = np.tile(np.arange(GRID_W), rows).astype(np.float64)
    inv = 1.0 / (ROPE_THETA ** (np.arange(0, ROPE_AXIS_DIM, 2, dtype=np.float64) / ROPE_AXIS_DIM))
    lane = np.arange(LANES)
    d = lane % HEAD_DIM
    axis = d // ROPE_AXIS_DIM
    half = (d // (ROPE_AXIS_DIM // 2)) % 2
    f = d % (ROPE_AXIS_DIM // 2)
    pos = np.where(axis[None, :] == 0, pos_row[:, None], pos_col[:, None])
    ang = pos * inv[f][None, :]
    cos2 = np.cos(ang)
    sin2 = np.where(half[None, :] == 0, -np.sin(ang), np.sin(ang))
    return cos2.astype(np.float32), sin2.astype(np.float32)


@functools.lru_cache(maxsize=None)
def _spread_table(W):
    owner = np.arange(N_EXPERTS * W) // W
    hit = (np.arange(N_EXPERTS)[:, None] == owner[None, :]).astype(np.float32)
    return np.concatenate([32.0 * hit, hit], axis=0)


@functools.lru_cache(maxsize=None)
def _topk_tables(T, tb):
    nj = T // LANES
    per = tb // LANES
    r = np.arange(nj * N_EXPERTS)
    j, e = r // N_EXPERTS, r % N_EXPERTS
    same_e = e[:, None] == e[None, :]
    blk = j // per
    same_blk = blk[:, None] == blk[None, :]
    m_all = same_e & (j[None, :] < j[:, None])
    m_w = same_e & same_blk & (j[None, :] < j[:, None])
    m_b = same_e & same_blk
    m_a = same_e & (blk[None, :] < blk[:, None]) & ((j % per) == 0)[None, :]
    tri = np.triu(np.ones((LANES, LANES)))
    return tuple(np.asarray(m, np.float32) for m in (m_all, m_w, m_b, m_a, tri))


def _hy_inproj_kernel(h_ref, g_ref, wmt_ref, wcq_ref, ut_ref, cq_ref):
    xn = _rms(h_ref[...], g_ref[...]).astype(BF16)
    ut_ref[0] = _dot_nt(wmt_ref[...], xn)
    cq_ref[...] = _dot(xn, wcq_ref[...]).astype(BF16)


def _hy_inproj(h, g, wmt, wcq, B, T, tm):
    N, D = h.shape
    c3 = wmt.shape[0]
    nt = T // tm
    return pl.pallas_call(
        _hy_inproj_kernel,
        grid=(B, nt),
        in_specs=[
            pl.BlockSpec((tm, D), lambda b, i: (b * nt + i, 0)),
            pl.BlockSpec((1, D), lambda b, i: (0, 0)),
            pl.BlockSpec((c3, D), lambda b, i: (0, 0)),
            pl.BlockSpec((D, D_MEM), lambda b, i: (0, 0)),
        ],
        out_specs=[
            pl.BlockSpec((1, c3, tm), lambda b, i: (b, 0, i)),
            pl.BlockSpec((tm, D_MEM), lambda b, i: (b * nt + i, 0)),
        ],
        out_shape=[
            jax.ShapeDtypeStruct((B, c3, T), F32),
            jax.ShapeDtypeStruct((N, D_MEM), BF16),
        ],
        compiler_params=_cp("parallel", "parallel"),
        name="hy_inproj",
    )(h, g, wmt, wcq)


def _filt_mlp_kernel(feats_ref, w1t_ref, b1_ref, w2t_ref, b2_ref, fr_ref, h2_ref):
    fr = fr_ref[...]
    for d in range(2):
        a = jnp.dot(w1t_ref[...], feats_ref[d], precision=HI, preferred_element_type=F32)
        h1 = jnp.sin(fr * (a + b1_ref[...]))
        a = jnp.dot(w2t_ref[...], h1, precision=HI, preferred_element_type=F32)
        h2_ref[d] = jnp.sin(fr * (a + b2_ref[...]))


def _filt_taps_kernel(h2_ref, w3t_ref, t_ref, dl_ref, out_ref):
    T = h2_ref.shape[2]
    dl = dl_ref[...]
    hf = jnp.dot(w3t_ref[0, 0], h2_ref[0], precision=HI, preferred_element_type=F32) * jnp.exp(-t_ref[0] * dl)
    hb = jnp.dot(w3t_ref[0, 1], h2_ref[1], precision=HI, preferred_element_type=F32) * jnp.exp(-t_ref[1] * dl)
    nrm = jnp.sum(jnp.abs(hf) + jnp.abs(hb), axis=-1, keepdims=True) + 1e-6
    inv = 1.0 / nrm
    tap0 = lax.broadcasted_iota(I32, hf.shape, 1) == 0
    out_ref[0, :, :T] = ((hf + jnp.where(tap0, hb, 0.0)) * inv).astype(BF16)
    out_ref[0, :, T:] = (jnp.where(tap0, 0.0, hb) * inv).astype(BF16)


def _fwd_fft(x3, f1s, twr, twi, w2, cb, n1k):
    rhs = jnp.concatenate([x3[c].astype(BF16) for c in range(cb)], axis=1)
    a = _dot(f1s, rhs)
    ar, ai = a[:n1k], a[n1k:]
    tr = (ar * twr - ai * twi).astype(BF16)
    ti = (ar * twi + ai * twr).astype(BF16)
    lr = jnp.concatenate([tr[:, c * LANES:(c + 1) * LANES] for c in range(cb)], axis=0)
    li = jnp.concatenate([ti[:, c * LANES:(c + 1) * LANES] for c in range(cb)], axis=0)
    return _dot(jnp.concatenate([lr, li], axis=1), w2)


def _inv_fft(y, minv, twr, twi, g1, cb, n1k):
    b = _dot(y.astype(BF16), minv)
    br, bi = b[:, :LANES], b[:, LANES:]
    pr = (br * twr + bi * twi).astype(BF16)
    pi = (bi * twr - br * twi).astype(BF16)
    top = jnp.concatenate([pr[c * n1k:(c + 1) * n1k] for c in range(cb)], axis=1)
    bot = jnp.concatenate([pi[c * n1k:(c + 1) * n1k] for c in range(cb)], axis=1)
    rhs = jnp.concatenate([top, bot], axis=0)
    return _dot(g1, rhs)


def _filt_fft_kernel(taps_ref, f1_ref, twr_ref, twi_ref, w2_ref, k_ref, *, cb, n1k):
    k_ref[0] = _fwd_fft(taps_ref[0], f1_ref[...], twr_ref[...], twi_ref[...], w2_ref[...], cb, n1k).astype(BF16)


def _hyena_filters_fft(p, T, cb):
    feats, t_rows = _filter_feats(T)
    C = p["w3t"].shape[2]
    tabs = _dft_tables(T, cb)
    n1k = 2 * T // LANES
    h2 = pl.pallas_call(
        _filt_mlp_kernel,
        out_shape=jax.ShapeDtypeStruct((2, HY_FILT, T), F32),
        compiler_params=pltpu.CompilerParams(vmem_limit_bytes=VMEM_LIMIT),
        name="hy_filt_mlp",
    )(jnp.asarray(feats), p["w1t"], p["b1"], p["w2t"], p["b2"], p["fr"])
    cbt = 64
    taps = pl.pallas_call(
        _filt_taps_kernel,
        grid=(2, C // cbt),
        in_specs=[
            pl.BlockSpec((2, HY_FILT, T), lambda o, c: (0, 0, 0)),
            pl.BlockSpec((1, 2, cbt, HY_FILT), lambda o, c: (o, 0, c, 0)),
            pl.BlockSpec((2, 1, T), lambda o, c: (0, 0, 0)),
            pl.BlockSpec((cbt, 1), lambda o, c: (c, 0)),
        ],
        out_specs=pl.BlockSpec((1, cbt, 2 * T), lambda o, c: (o, c, 0)),
        out_shape=jax.ShapeDtypeStruct((2, C, 2 * T), BF16),
        compiler_params=_cp("parallel", "parallel"),
        name="hy_filt_taps",
    )(h2, p["w3t"], jnp.asarray(t_rows), p["absdelta"])
    taps = taps.reshape(2, C, n1k, LANES)
    const = lambda shape: pl.BlockSpec(shape, lambda o, c: (0,) * len(shape))
    f32 = lambda k: jnp.asarray(tabs[k], F32)
    return pl.pallas_call(
        functools.partial(_filt_fft_kernel, cb=cb, n1k=n1k),
        grid=(2, C // cb),
        in_specs=[
            pl.BlockSpec((1, cb, n1k, LANES), lambda o, c: (o, c, 0, 0)),
            const((2 * n1k, n1k)), const((n1k, cb * LANES)), const((n1k, cb * LANES)),
            const((2 * LANES, 2 * LANES)),
        ],
        out_specs=pl.BlockSpec((1, cb * n1k, 2 * LANES), lambda o, c: (o, c, 0)),
        out_shape=jax.ShapeDtypeStruct((2, C * n1k, 2 * LANES), BF16),
        compiler_params=_cp("parallel", "parallel"),
        name="hy_filt_fft",
    )(taps, f32("f1full").astype(BF16), f32("twr_l"), f32("twi_l"), f32("w2").astype(BF16))


def _time_neighbours(x):
    rows = x.shape[0]
    lane = lax.broadcasted_iota(I32, x.shape, 1)
    r = pltpu.roll(x, 1, 1)
    rr = pltpu.roll(r, 1, 0)
    prev = jnp.where(lane == 0, rr, r)
    r2 = pltpu.roll(x, LANES - 1, 1)
    rr2 = pltpu.roll(r2, rows - 1, 0)
    nxt = jnp.where(lane == LANES - 1, rr2, r2)
    return prev, nxt


def _hy_conv_kernel(x1_ref, x2_ref, v_ref, sw1_ref, sw2_ref, swv_ref, skip_ref, k_ref,
                    f1s_ref, twrl_ref, twil_ref, w2_ref, minv_ref, twrr_ref, twir_ref,
                    g1_ref, o_ref, *, cb, n1k, n1):
    rows = cb * n1
    shape2 = (rows, LANES)
    row = lax.broadcasted_iota(I32, shape2, 0)
    lane = lax.broadcasted_iota(I32, shape2, 1)
    first = (lane == 0) & (row % n1 == 0)
    last = (lane == LANES - 1) & (row % n1 == n1 - 1)

    def sconv(x_ref, sw_ref):
        x = x_ref[0].reshape(shape2)
        prev, nxt = _time_neighbours(x)
        prev = jnp.where(first, 0.0, prev)
        nxt = jnp.where(last, 0.0, nxt)
        w = [jnp.broadcast_to(sw_ref[j], (cb, n1, LANES)).reshape(shape2) for j in range(3)]
        return prev * w[0] + x * w[1] + nxt * w[2]

    z = sconv(v_ref, swv_ref)
    gates = (sconv(x1_ref, sw1_ref), sconv(x2_ref, sw2_ref))
    f1s, w2 = f1s_ref[...], w2_ref[...]
    for o in range(2):
        zf = _fwd_fft(z.reshape(cb, n1, LANES), f1s, twrl_ref[...], twil_ref[...], w2, cb, n1k)
        kk = k_ref[o].astype(F32)
        zr, zi = zf[:, :LANES], zf[:, LANES:]
        kr, ki = kk[:, :LANES], kk[:, LANES:]
        y = jnp.concatenate([zr * kr - zi * ki, zr * ki + zi * kr], axis=1)
        conv = _inv_fft(y, minv_ref[...], twrr_ref[...], twir_ref[...], g1_ref[...], cb, n1k)
        conv = jnp.concatenate([conv[:, c * LANES:(c + 1) * LANES] for c in range(cb)], axis=0)
        skip = jnp.broadcast_to(skip_ref[o], (cb, n1, LANES)).reshape(shape2)
        z = gates[o] * (conv + skip * z)
    o_ref[0] = z.reshape(cb, n1, LANES)


def _hy_conv(ut4, sw, skip, kfft, B, T, C, cb):
    n1k, n1 = 2 * T // LANES, T // LANES
    tabs = _dft_tables(T, cb)
    nct = C // cb
    bf = lambda k: jnp.asarray(tabs[k], F32).astype(BF16)
    f32 = lambda k: jnp.asarray(tabs[k], F32)
    const = lambda shape: pl.BlockSpec(shape, lambda c, b: (0,) * len(shape))
    ublk = lambda s: pl.BlockSpec((1, cb, n1, LANES), lambda c, b, s=s: (b, s * nct + c, 0, 0))
    wblk = lambda s: pl.BlockSpec((3, cb, 1, LANES), lambda c, b, s=s: (0, s * nct + c, 0, 0))
    return pl.pallas_call(
        functools.partial(_hy_conv_kernel, cb=cb, n1k=n1k, n1=n1),
        grid=(nct, B),
        in_specs=[
            ublk(0), ublk(1), ublk(2), wblk(0), wblk(1), wblk(2),
            pl.BlockSpec((2, cb, 1, LANES), lambda c, b: (0, c, 0, 0)),
            pl.BlockSpec((2, cb * n1k, 2 * LANES), lambda c, b: (0, c, 0)),
            const((2 * n1k, n1)), const((n1k, cb * LANES)), const((n1k, cb * LANES)),
            const((2 * LANES, 2 * LANES)), const((2 * LANES, 2 * LANES)),
            const((cb * n1k, LANES)), const((cb * n1k, LANES)), const((n1, 2 * n1k)),
        ],
        out_specs=pl.BlockSpec((1, cb, n1, LANES), lambda c, b: (b, c, 0, 0)),
        out_shape=jax.ShapeDtypeStruct((B, C, n1, LANES), F32),
        compiler_params=_cp("parallel", "arbitrary"),
        name="hy_conv",
    )(ut4, ut4, ut4, sw, sw, sw, skip, kfft,
      bf("f1s"), f32("twr_l"), f32("twi_l"), bf("w2"), bf("minv"),
      f32("twr_r"), f32("twi_r"), bf("g1"))


def _head_norm(x, bd, g):
    ss = _dot((x * x).astype(BF16), bd)
    return x * lax.rsqrt(ss * (1.0 / HEAD_DIM) + NORM_EPS) * g


def _rope(x, cos2, sin2):
    lane = lax.broadcasted_iota(I32, cos2.shape, 1)
    low = (lane // (ROPE_AXIS_DIM // 2)) % 2 == 0
    out = []
    for c in range(x.shape[1] // LANES):
        xc = x[:, c * LANES:(c + 1) * LANES]
        up = pltpu.roll(xc, LANES - 16, 1)
        dn = pltpu.roll(xc, 16, 1)
        out.append(xc * cos2 + jnp.where(low, up, dn) * sin2)
    return jnp.concatenate(out, axis=1)


def _at_inproj_kernel(h_ref, g_ref, w_ref, bdq_ref, bdk_ref, gq_ref, gk_ref, cos_ref, sin_ref,
                      qt_ref, kz_ref, vt_ref, cq_ref, *, dq, dk):
    xn = _rms(h_ref[...], g_ref[...]).astype(BF16)
    proj = _dot(xn, w_ref[...])
    q, k = proj[:, :dq], proj[:, dq:dq + dk]
    v, cq = proj[:, dq + dk:dq + 2 * dk], proj[:, dq + 2 * dk:]
    cos2, sin2 = cos_ref[...], sin_ref[...]
    qr = _rope(_head_norm(q, bdq_ref[...], gq_ref[...]), cos2, sin2)
    qt = (qr * (HEAD_DIM ** -0.5 * LOG2E)).T.astype(BF16)
    for p in range(dq // LANES):
        qt_ref[0, p] = qt[p * LANES:(p + 1) * LANES]
    kr = _rope(_head_norm(k, bdk_ref[...], gk_ref[...]), cos2, sin2)
    vt = v.T.astype(BF16)
    zv = jnp.zeros((HEAD_DIM, vt.shape[1]), BF16)
    lane = lax.broadcasted_iota(I32, (k.shape[0], LANES), 1)
    for kv in range(N_KV_HEADS):
        rows = vt[kv * HEAD_DIM:(kv + 1) * HEAD_DIM]
        vt_ref[0, kv, 0] = jnp.concatenate([rows, zv], axis=0)
        vt_ref[0, kv, 1] = jnp.concatenate([zv, rows], axis=0)
        pair = kr[:, (kv // 2) * LANES:(kv // 2 + 1) * LANES]
        own = jnp.where((lane < HEAD_DIM) == (kv % 2 == 0), pair, 0.0)
        other = pltpu.roll(own, HEAD_DIM, 1)
        lo, hi = (own, other) if kv % 2 == 0 else (other, own)
        kz_ref[0, kv, 0] = lo.astype(BF16)
        kz_ref[0, kv, 1] = hi.astype(BF16)
    cq_ref[...] = cq.astype(BF16)


def _at_inproj(h, g, w, gq, gk, B, T, tm):
    N, D = h.shape
    dq, dk = GQA_GROUP * N_KV_HEADS * HEAD_DIM, N_KV_HEADS * HEAD_DIM
    nt = T // tm
    cos2, sin2 = _rope_tables(T)
    bd = lambda n: jnp.asarray(np.kron(np.eye(n // HEAD_DIM), np.ones((HEAD_DIM, HEAD_DIM))), F32).astype(BF16)
    const = lambda shape: pl.BlockSpec(shape, lambda b, i: (0,) * len(shape))
    return pl.pallas_call(
        functools.partial(_at_inproj_kernel, dq=dq, dk=dk),
        grid=(B, nt),
        in_specs=[
            pl.BlockSpec((tm, D), lambda b, i: (b * nt + i, 0)),
            const((1, D)), const(w.shape), const((dq, dq)), const((dk, dk)),
            const((1, dq)), const((1, dk)),
            pl.BlockSpec((tm, LANES), lambda b, i: (i, 0)),
            pl.BlockSpec((tm, LANES), lambda b, i: (i, 0)),
        ],
        out_specs=[
            pl.BlockSpec((1, dq // LANES, LANES, tm), lambda b, i: (b, 0, 0, i)),
            pl.BlockSpec((1, N_KV_HEADS, 2, tm, LANES), lambda b, i: (b, 0, 0, i, 0)),
            pl.BlockSpec((1, N_KV_HEADS, 2, LANES, tm), lambda b, i: (b, 0, 0, 0, i)),
            pl.BlockSpec((tm, D_MEM), lambda b, i: (b * nt + i, 0)),
        ],
        out_shape=[
            jax.ShapeDtypeStruct((B, dq // LANES, LANES, T), BF16),
            jax.ShapeDtypeStruct((B, N_KV_HEADS, 2, T, LANES), BF16),
            jax.ShapeDtypeStruct((B, N_KV_HEADS, 2, LANES, T), BF16),
            jax.ShapeDtypeStruct((N, D_MEM), BF16),
        ],
        compiler_params=_cp("parallel", "parallel"),
        name="at_inproj",
    )(h, g, w, bd(dq), bd(dk), gq, gk, jnp.asarray(cos2), jnp.asarray(sin2))


def _flash_kernel(qt_ref, ka_ref, kb_ref, vta_ref, vtb_ref, o_ref, m_sc, l_sc, acc_sc, s_sc, p_sc):
    ki = pl.program_id(3)

    @pl.when(ki == 0)
    def _():
        m_sc[...] = jnp.full(m_sc.shape, -jnp.inf, F32)
        l_sc[...] = jnp.zeros(l_sc.shape, F32)
        acc_sc[...] = jnp.zeros(acc_sc.shape, F32)

    qt = qt_ref[0, 0]
    tk, tq = s_sc.shape[1], s_sc.shape[2]
    sub = 8
    ck = 2 * sub
    pv, alphas = [], []
    for idx, k_ref in enumerate((ka_ref, kb_ref)):
        s_sc[idx] = _dot(k_ref[0, 0, 0], qt)
    for idx, vt_ref in enumerate((vta_ref, vtb_ref)):
        mx = s_sc[idx, 0:sub, :]
        for c in range(1, tk // sub):
            mx = jnp.maximum(mx, s_sc[idx, c * sub:(c + 1) * sub, :])
        m_prev = m_sc[idx]
        m_new = jnp.maximum(m_prev, jnp.max(mx, axis=0, keepdims=True))
        alpha = jnp.exp2(m_prev - m_new)
        lsum = jnp.zeros((sub, tq), F32)
        for c in range(tk // ck):
            p = jnp.exp2(s_sc[idx, c * ck:(c + 1) * ck, :] - m_new)
            lsum = lsum + p[:sub] + p[sub:]
            p_sc[idx, c * ck:(c + 1) * ck, :] = p.astype(BF16)
        l_sc[idx] = alpha * l_sc[idx] + jnp.sum(lsum, axis=0, keepdims=True)
        m_sc[idx] = m_new
        pv.append(_dot(vt_ref[0, 0, 0], p_sc[idx]))
        alphas.append(alpha)
    row = lax.broadcasted_iota(I32, acc_sc.shape, 0)
    low = row < HEAD_DIM
    acc_sc[...] = acc_sc[...] * jnp.where(low, alphas[0], alphas[1]) + pv[0] + pv[1]

    @pl.when(ki == pl.num_programs(3) - 1)
    def _():
        o_ref[0] = acc_sc[...] / jnp.where(low, l_sc[0], l_sc[1])


def _flash(qt, kz, vtz, B, T, tq, tk):
    npair = qt.shape[1]
    nq, nk = T // tq, T // tk
    kv_of = lambda p, j: (2 * p + j) // GQA_GROUP
    return pl.pallas_call(
        _flash_kernel,
        grid=(B, npair, nq, nk),
        in_specs=[
            pl.BlockSpec((1, 1, LANES, tq), lambda b, p, i, k: (b, p, 0, i)),
            pl.BlockSpec((1, 1, 1, tk, LANES), lambda b, p, i, k: (b, kv_of(p, 0), 0, k, 0)),
            pl.BlockSpec((1, 1, 1, tk, LANES), lambda b, p, i, k: (b, kv_of(p, 1), 1, k, 0)),
            pl.BlockSpec((1, 1, 1, LANES, tk), lambda b, p, i, k: (b, kv_of(p, 0), 0, 0, k)),
            pl.BlockSpec((1, 1, 1, LANES, tk), lambda b, p, i, k: (b, kv_of(p, 1), 1, 0, k)),
        ],
        out_specs=pl.BlockSpec((1, LANES, tq), lambda b, p, i, k: (b, p, i)),
        out_shape=jax.ShapeDtypeStruct((B, npair * LANES, T), F32),
        scratch_shapes=[
            pltpu.VMEM((2, 1, tq), F32), pltpu.VMEM((2, 1, tq), F32), pltpu.VMEM((LANES, tq), F32),
            pltpu.VMEM((2, tk, tq), F32), pltpu.VMEM((2, tk, tq), BF16),
        ],
        compiler_params=_cp("parallel", "parallel", "parallel", "arbitrary"),
        name="flash_gqa",
    )(qt, kz, kz, vtz, vtz)


def _memkv_kernel(mem_ref, g_ref, wkvt_ref, wv_ref, mkt_ref, mv_ref):
    mn = _rms(mem_ref[0], g_ref[...]).astype(BF16)
    kt = _dot_nt(wkvt_ref[...], mn) * (HEAD_DIM ** -0.5)
    v = _dot(mn, wv_ref[...])
    row = lax.broadcasted_iota(I32, kt.shape, 0)
    col = lax.broadcasted_iota(I32, v.shape, 1)
    for hd in range(MEM_HEADS):
        mkt_ref[0, hd] = jnp.where(row // HEAD_DIM == hd, kt, 0.0).astype(BF16)
        mv_ref[0, hd] = jnp.where(col // HEAD_DIM == hd, v, 0.0).astype(BF16)


def _memkv(mem, g, wkt, wv):
    B, M, D = mem.shape
    return pl.pallas_call(
        _memkv_kernel,
        grid=(B,),
        in_specs=[
            pl.BlockSpec((1, M, D), lambda b: (b, 0, 0)),
            pl.BlockSpec((1, D), lambda b: (0, 0)),
            pl.BlockSpec((D_MEM, D), lambda b: (0, 0)),
            pl.BlockSpec((D, D_MEM), lambda b: (0, 0)),
        ],
        out_specs=[
            pl.BlockSpec((1, MEM_HEADS, D_MEM, M), lambda b: (b, 0, 0, 0)),
            pl.BlockSpec((1, MEM_HEADS, M, D_MEM), lambda b: (b, 0, 0, 0)),
        ],
        out_shape=[
            jax.ShapeDtypeStruct((B, MEM_HEADS, D_MEM, M), BF16),
            jax.ShapeDtypeStruct((B, MEM_HEADS, M, D_MEM), BF16),
        ],
        compiler_params=_cp("parallel"),
        name="mem_kv",
    )(mem, g, wkt, wv)


def _cross_attn(cq, mkt_ref, mv_ref):
    acc = jnp.zeros(cq.shape, F32)
    for hd in range(MEM_HEADS):
        s = _dot(cq, mkt_ref[0, hd])
        p = jnp.exp(s - jnp.max(s, axis=-1, keepdims=True))
        p = p / jnp.sum(p, axis=-1, keepdims=True)
        acc = acc + _dot(p.astype(BF16), mv_ref[0, hd])
    return acc


def _route(h, g_ref, wr_ref, wrt_ref, xn_ref, gsp_ref, afft_ref):
    xn = _rms(h, g_ref[...]).astype(BF16)
    xn_ref[...] = xn
    lg = _dot(xn, wr_ref[...])
    lane = lax.broadcasted_iota(I32, lg.shape, 1)
    lg = jnp.where(lane < N_EXPERTS, lg, -jnp.inf)
    p = jnp.exp(lg - jnp.max(lg, axis=-1, keepdims=True))
    aff = p / jnp.sum(p, axis=-1, keepdims=True)
    hi = aff.astype(BF16).astype(F32)
    mid = (aff - hi).astype(BF16).astype(F32)
    lo = (aff - hi - mid).astype(BF16).astype(F32)
    gsp_ref[...] = (hi + pltpu.roll(mid, N_EXPERTS, 1) + pltpu.roll(lo, 2 * N_EXPERTS, 1)).astype(BF16)
    lt = _dot_nt(wrt_ref[...], xn)
    pt = jnp.exp(lt - jnp.max(lt, axis=0, keepdims=True))
    pt = pt / jnp.sum(pt, axis=0, keepdims=True)
    for j in range(lt.shape[1] // LANES):
        afft_ref[0, j] = pt[:, j * LANES:(j + 1) * LANES]


def _outproj_kernel(h_ref, main_ref, cq_ref, mkt_ref, mv_ref, wm_ref, wc_ref, g_ref, wr_ref, wrt_ref,
                    o_ref, xn_ref, gsp_ref, afft_ref):
    if len(main_ref.shape) == 3:
        main = main_ref[0].T
    else:
        main = jnp.concatenate([main_ref[0, :, j, :].T for j in range(main_ref.shape[2])], axis=0)
    main = main.astype(BF16)
    cross = _cross_attn(cq_ref[...], mkt_ref, mv_ref).astype(BF16)
    h_new = h_ref[...] + _dot(main, wm_ref[...]) + _dot(cross, wc_ref[...])
    o_ref[...] = h_new
    _route(h_new, g_ref, wr_ref, wrt_ref, xn_ref, gsp_ref, afft_ref)


def _outproj(h, main, cq, mkt, mv, wm, wc, g, wr, wrt, B, T, tm):
    N, D = h.shape
    C = wm.shape[0]
    M = mkt.shape[-1]
    nt = T // tm
    if main.ndim == 3:
        mspec = pl.BlockSpec((1, C, tm), lambda b, i: (b, 0, i))
    else:
        mspec = pl.BlockSpec((1, C, tm // LANES, LANES), lambda b, i: (b, 0, i, 0))
    return pl.pallas_call(
        _outproj_kernel,
        grid=(B, nt),
        in_specs=[
            pl.BlockSpec((tm, D), lambda b, i: (b * nt + i, 0)),
            mspec,
            pl.BlockSpec((tm, D_MEM), lambda b, i: (b * nt + i, 0)),
            pl.BlockSpec((1, MEM_HEADS, D_MEM, M), lambda b, i: (b, 0, 0, 0)),
            pl.BlockSpec((1, MEM_HEADS, M, D_MEM), lambda b, i: (b, 0, 0, 0)),
            pl.BlockSpec((C, D), lambda b, i: (0, 0)),
            pl.BlockSpec((D_MEM, D), lambda b, i: (0, 0)),
            pl.BlockSpec((1, D), lambda b, i: (0, 0)),
            pl.BlockSpec((D, LANES), lambda b, i: (0, 0)),
            pl.BlockSpec((N_EXPERTS, D), lambda b, i: (0, 0)),
        ],
        out_specs=[
            pl.BlockSpec((tm, D), lambda b, i: (b * nt + i, 0)),
            pl.BlockSpec((tm, D), lambda b, i: (b * nt + i, 0)),
            pl.BlockSpec((tm, LANES), lambda b, i: (b * nt + i, 0)),
            pl.BlockSpec((1, tm // LANES, N_EXPERTS, LANES), lambda b, i: (b, i, 0, 0)),
        ],
        out_shape=[
            jax.ShapeDtypeStruct((N, D), F32),
            jax.ShapeDtypeStruct((N, D), BF16),
            jax.ShapeDtypeStruct((N, LANES), BF16),
            jax.ShapeDtypeStruct((B, T // LANES, N_EXPERTS, LANES), F32),
        ],
        compiler_params=_cp("parallel", "parallel"),
        name="outproj",
    )(h, main, cq, mkt, mv, wm, wc, g, wr, wrt)


def _topk_kernel(aff_ref, mall_ref, mw_ref, mb_ref, ma_ref, tri_ref, pos_ref, aoff_ref, pad_ref,
                 *, cap, nj):
    E = N_EXPERTS
    aff3 = aff_ref[0]

    def count(mask3):
        per = jnp.sum(mask3.astype(F32), axis=0)
        return jnp.broadcast_to(jnp.sum(per, axis=-1, keepdims=True), (E, LANES))

    def step(i, thr):
        cand = thr | lax.shift_left(jnp.int32(1), 30 - i)
        ok = count(aff3 >= pltpu.bitcast(cand, F32)[None]) >= cap
        return jnp.where(ok, cand, thr)

    thr = lax.fori_loop(0, 31, step, jnp.zeros((E, LANES), I32))
    thr = pltpu.bitcast(thr, F32)
    gt3 = aff3 > thr[None]
    eq3 = aff3 == thr[None]
    need = cap - count(gt3)

    ones = jnp.ones((LANES, LANES), BF16)
    tri = tri_ref[...]

    def prefix(mask2):
        mb = mask2.astype(BF16)
        incl = _dot(mb, tri)
        tot = _dot(mb, ones)
        return incl - mask2, tot

    eq2 = eq3.reshape(nj * E, LANES).astype(F32)
    ex, tot = prefix(eq2)
    eq_rank = ex + _dot(mall_ref[...], tot.astype(BF16))
    need2 = jnp.broadcast_to(need[None], (nj, E, LANES)).reshape(nj * E, LANES)
    sel = jnp.where((gt3.reshape(nj * E, LANES)) | ((eq2 > 0) & (eq_rank < need2)), 1.0, 0.0)

    ex, tot = prefix(sel)
    totb = tot.astype(BF16)
    within = _dot(mw_ref[...], totb)
    cnt = _dot(mb_ref[...], totb)
    padded = (cnt.astype(I32) + (BF16_SUBLANES - 1)) & ~(BF16_SUBLANES - 1)
    aoff = _dot(ma_ref[...], padded.astype(F32).astype(BF16))
    pos = (aoff + within + ex).astype(I32)
    pos_ref[0] = jnp.where(sel > 0, pos, -1).reshape(nj, E, LANES)
    aoff_ref[0] = aoff.astype(I32).reshape(nj, E, LANES)
    pad_ref[0] = padded.reshape(nj, E, LANES)


def _topk(aff4, T, tb):
    B, nj = aff4.shape[0], aff4.shape[1]
    cap = EC_CAPACITY_FACTOR * T // N_EXPERTS
    tabs = [jnp.asarray(m, F32).astype(BF16) for m in _topk_tables(T, tb)]
    R = nj * N_EXPERTS
    blk = pl.BlockSpec((1, nj, N_EXPERTS, LANES), lambda b: (b, 0, 0, 0))
    const = lambda shape: pl.BlockSpec(shape, lambda b: (0,) * len(shape))
    out = jax.ShapeDtypeStruct((B, nj, N_EXPERTS, LANES), I32)
    return pl.pallas_call(
        functools.partial(_topk_kernel, cap=cap, nj=nj),
        grid=(B,),
        in_specs=[blk, const((R, R)), const((R, R)), const((R, R)), const((R, R)), const((LANES, LANES))],
        out_specs=[blk, blk, blk],
        out_shape=[out, out, out],
        compiler_params=_cp("parallel"),
        name="moe_topk",
    )(aff4, *tabs)


def _compress_kernel(aoff_ref, nch_ref, nmax_ref, xn_ref, gsp_ref, pos_ref, xg_ref, gs_ref,
                     *, nblk, tb, W, eg, nsub):
    b, grp, blk = pl.program_id(0), pl.program_id(1), pl.program_id(2)

    @pl.when(blk == 0)
    def _():
        xg_ref[...] = jnp.zeros(xg_ref.shape, BF16)
        gs_ref[...] = jnp.zeros(gs_ref.shape, F32)

    iota_s = lax.broadcasted_iota(I32, (W, LANES), 0)
    per = tb // LANES
    for sb in range(nsub):
        tblk = blk * nsub + sb
        base = (b * nblk + tblk) * N_EXPERTS + grp * eg
        tok = slice(sb * tb, (sb + 1) * tb)

        def chunk(c, carry, base=base, tok=tok, sb=sb):
            pieces = []
            for i in range(eg):
                a = aoff_ref[base + i] + c * W
                e = grp * eg + i
                g = [pos_ref[0, sb * per + jj, pl.ds(e, 1), :] - a == iota_s for jj in range(per)]
                pieces.append(jnp.where(jnp.concatenate(g, axis=1), 1.0, 0.0).astype(BF16))
            lhs = jnp.concatenate(pieces, axis=0)
            res = _dot(lhs, xn_ref[tok, :]).astype(BF16)
            resg = _dot(lhs, gsp_ref[tok, :])
            for i in range(eg):
                @pl.when(c < nch_ref[base + i])
                def _():
                    a = pl.multiple_of(aoff_ref[base + i] + c * W, BF16_SUBLANES)
                    xg_ref[0, i, pl.ds(a, W), :] = res[i * W:(i + 1) * W]
                    gs_ref[0, i, pl.ds(a, W), :] = resg[i * W:(i + 1) * W]
            return carry

        lax.fori_loop(0, nmax_ref[b * nblk + tblk], chunk, 0)


def _compress(aoff, nch, nmax, xn, gsp, pos4, B, T, tb, rows, W):
    N, D = xn.shape
    nblk = T // tb
    eg = 4
    nsub = 2 if nblk % 2 == 0 else 1
    ns, ts = nblk // nsub, nsub * tb
    return pl.pallas_call(
        functools.partial(_compress_kernel, nblk=nblk, tb=tb, W=W, eg=eg, nsub=nsub),
        grid_spec=pltpu.PrefetchScalarGridSpec(
            num_scalar_prefetch=3,
            grid=(B, N_EXPERTS // eg, ns),
            in_specs=[
                pl.BlockSpec((ts, D), lambda b, g, k, *_: (b * ns + k, 0)),
                pl.BlockSpec((ts, LANES), lambda b, g, k, *_: (b * ns + k, 0)),
                pl.BlockSpec((1, ts // LANES, N_EXPERTS, LANES), lambda b, g, k, *_: (b, k, 0, 0)),
            ],
            out_specs=[
                pl.BlockSpec((1, eg, rows, D), lambda b, g, k, *_: (b, g, 0, 0)),
                pl.BlockSpec((1, eg, rows, LANES), lambda b, g, k, *_: (b, g, 0, 0)),
            ],
        ),
        out_shape=[
            jax.ShapeDtypeStruct((B, N_EXPERTS, rows, D), BF16),
            jax.ShapeDtypeStruct((B, N_EXPERTS, rows, LANES), F32),
        ],
        compiler_params=_cp("parallel", "parallel", "arbitrary"),
        name="moe_compress",
    )(aoff, nch, nmax, xn, gsp, pos4)


def _ffn_kernel(x_ref, gs_ref, wg_ref, wu_ref, wd_ref, y_ref, *, live):
    e = pl.program_id(0)
    x = x_ref[0, 0, :live]
    g = _dot(x, wg_ref[0, 0].astype(BF16))
    u = _dot(x, wu_ref[0, 0].astype(BF16))
    hid = (g * (1.0 / (1.0 + jnp.exp(-g))) * u).astype(BF16)
    gs = gs_ref[0, 0, :live]
    lane = lax.broadcasted_iota(I32, gs.shape, 1)
    mine = (lane == e) | (lane == e + N_EXPERTS) | (lane == e + 2 * N_EXPERTS)
    gate = jnp.sum(jnp.where(mine, gs, 0.0), axis=-1, keepdims=True)
    y_ref[0, 0, :live] = (_dot(hid, wd_ref[0, 0].astype(BF16)) * gate).astype(BF16)
    y_ref[0, 0, live:] = jnp.zeros((y_ref.shape[2] - live, y_ref.shape[3]), BF16)


def _ffn(xg, gs, wg, wu, wd, layer, live):
    B, E, rows, D = xg.shape
    F = wg.shape[-1]
    return pl.pallas_call(
        functools.partial(_ffn_kernel, live=live),
        grid=(E, B),
        in_specs=[
            pl.BlockSpec((1, 1, rows, D), lambda e, b: (b, e, 0, 0)),
            pl.BlockSpec((1, 1, rows, LANES), lambda e, b: (b, e, 0, 0)),
            pl.BlockSpec((1, 1, D, F), lambda e, b: (layer, e, 0, 0)),
            pl.BlockSpec((1, 1, D, F), lambda e, b: (layer, e, 0, 0)),
            pl.BlockSpec((1, 1, F, D), lambda e, b: (layer, e, 0, 0)),
        ],
        out_specs=pl.BlockSpec((1, 1, rows, D), lambda e, b: (b, e, 0, 0)),
        out_shape=jax.ShapeDtypeStruct((B, E, rows, D), BF16),
        compiler_params=_cp("parallel", "arbitrary"),
        name="moe_ffn",
    )(xg, gs, wg, wu, wd)


def _expand_kernel(aoff_ref, nch_ref, nmax_ref, h_ref, post_ref, spread_ref, slot1_ref, y_ref, o_ref,
                   *, nblk, tb, W, rows):
    b, blk = pl.program_id(0), pl.program_id(2)
    base = (b * nblk + blk) * N_EXPERTS
    ngrp = N_EXPERTS * W // LANES
    lane = lax.broadcasted_iota(I32, (1, LANES), 1)
    rel = _dot(post_ref[...], spread_ref[...]) - slot1_ref[...]
    never = jnp.float32(-2.0 ** 20)

    def chunk(c, acc):
        tgt, wins = [], []
        for e in range(N_EXPERTS):
            a = aoff_ref[base + e] + c * W
            tgt.append(jnp.where(c < nch_ref[base + e], a.astype(F32), never))
            a_in = pl.multiple_of(jnp.minimum(a, rows - W), BF16_SUBLANES)
            wins.append(y_ref[0, e, pl.ds(a_in, W), :])
        cols = []
        for g in range(ngrp):
            first, last = (LANES * g) // W, (LANES * g + LANES - 1) // W
            t = jnp.full((1, LANES), tgt[last], F32)
            for e in range(last - 1, first - 1, -1):
                t = jnp.where(lane < (e + 1) * W - LANES * g, tgt[e], t)
            hit = rel[:, g * LANES:(g + 1) * LANES] == t
            cols.append(jnp.where(hit, 1.0, 0.0).astype(BF16))
        return acc + _dot(jnp.concatenate(cols, axis=1), jnp.concatenate(wins, axis=0))

    acc = lax.fori_loop(0, nmax_ref[b * nblk + blk], chunk, jnp.zeros(o_ref.shape, F32))
    o_ref[...] = h_ref[...] + acc


def _expand(aoff, nch, nmax, h, post, y, B, T, tb, dw, W):
    N, D = h.shape
    rows = y.shape[2]
    nblk = T // tb
    return pl.pallas_call(
        functools.partial(_expand_kernel, nblk=nblk, tb=tb, W=W, rows=rows),
        grid_spec=pltpu.PrefetchScalarGridSpec(
            num_scalar_prefetch=3,
            grid=(B, D // dw, nblk),
            in_specs=[
                pl.BlockSpec((tb, dw), lambda b, d, k, *_: (b * nblk + k, d)),
                pl.BlockSpec((tb, 2 * N_EXPERTS), lambda b, d, k, *_: (b * nblk + k, 0)),
                pl.BlockSpec((2 * N_EXPERTS, N_EXPERTS * W), lambda b, d, k, *_: (0, 0)),
                pl.BlockSpec((1, N_EXPERTS * W), lambda b, d, k, *_: (0, 0)),
                pl.BlockSpec((1, N_EXPERTS, rows, dw), lambda b, d, k, *_: (b, 0, 0, d),
                             pipeline_mode=pl.Buffered(1)),
            ],
            out_specs=pl.BlockSpec((tb, dw), lambda b, d, k, *_: (b * nblk + k, d)),
        ),
        out_shape=jax.ShapeDtypeStruct((N, D), F32),
        compiler_params=_cp("parallel", "parallel", "arbitrary"),
        name="moe_expand",
    )(aoff, nch, nmax, h, post, jnp.asarray(_spread_table(W), F32).astype(BF16),
      jnp.asarray((np.arange(N_EXPERTS * W) % W + 1).reshape(1, -1), F32), y)


def _moe(h, xn, gsp, aff4, wg, wu, wd, layer, B, T):
    N, D = h.shape
    tb = min(512, T)
    W = 96
    nblk = T // tb
    cap = EC_CAPACITY_FACTOR * T // N_EXPERTS
    rows = -(-(cap + nblk * (BF16_SUBLANES - 1)) // LANES) * LANES + LANES
    dw = 256
    pos4, aoff4, pad4 = _topk(aff4, T, tb)
    per = tb // LANES
    aoff = aoff4[:, ::per, :, 0].reshape(-1)
    nch4 = (pad4[:, ::per, :, 0] + (W - 1)) // W
    nch = nch4.reshape(-1)
    nmax = jnp.max(nch4, axis=-1).reshape(-1)
    post = jnp.transpose(pos4, (0, 1, 3, 2)).reshape(N, N_EXPERTS) + 1
    post = jnp.concatenate([post >> 5, post & 31], axis=1).astype(BF16)
    xg, gs = _compress(aoff, nch, nmax, xn, gsp, pos4, B, T, tb, rows, W)
    y = _ffn(xg, gs, wg, wu, wd, layer, rows - LANES)
    return _expand(aoff, nch, nmax, h, post, y, B, T, tb, 2 * dw, W)


def _final_norm_kernel(h_ref, g_ref, o_ref):
    o_ref[...] = _rms(h_ref[...], g_ref[...])


def _final_norm(h, g, tm):
    N, D = h.shape
    return pl.pallas_call(
        _final_norm_kernel,
        grid=(N // tm,),
        in_specs=[pl.BlockSpec((tm, D), lambda i: (i, 0)), pl.BlockSpec((1, D), lambda i: (0, 0))],
        out_specs=pl.BlockSpec((tm, D), lambda i: (i, 0)),
        out_shape=jax.ShapeDtypeStruct((N, D), F32),
        compiler_params=_cp("parallel"),
        name="final_norm",
    )(h, g)


def kernel(x, mem, mix_norm_g, ffn_norm_g, mem_norm_g, final_norm_g, w_mem_kv, w_out, hy_w_in, hy_short_w, hy_filt_w1, hy_filt_b1, hy_filt_w2, hy_filt_b2, hy_filt_w3, hy_filt_freq, hy_skip, at_w_in, at_q_norm_g, at_k_norm_g, router_w, exp_w_gate, exp_w_up, exp_w_down):
    B, T, D = x.shape
    depth = mix_norm_g.shape[0]
    C = hy_skip.shape[-1]
    tm = min(512, T)
    cb = 16
    n1 = T // LANES
    h = x.reshape(B * T, D)
    row = lambda v: v.reshape(1, -1).astype(F32)
    lanes = lambda v: jnp.broadcast_to(v[..., None, None], v.shape + (1, LANES)).astype(F32)
    max_decay = math.log(HY_DECAY_TARGET) / HY_FAST_PCT
    min_decay = math.log(HY_DECAY_TARGET) / HY_SLOW_PCT
    absdelta = jnp.asarray(np.abs(np.linspace(min_decay, max_decay, C)).astype(np.float32)).reshape(C, 1)

    for i in range(depth):
        j = i // 2
        wo = w_out[i].astype(BF16)
        wkv = w_mem_kv[i].astype(BF16)
        mkt, mv = _memkv(mem, row(mem_norm_g), wkv[:, :D_MEM].T, wkv[:, D_MEM:])
        wr = router_w[i].astype(BF16)
        route_w = (row(ffn_norm_g[i]), jnp.zeros((D, LANES), BF16).at[:, :N_EXPERTS].set(wr), wr.T)
        if i % 2 == 0:
            w_in = hy_w_in[j].astype(BF16)
            ut, cq = _hy_inproj(h, row(mix_norm_g[i]), w_in[:, :3 * C].T, w_in[:, 3 * C:], B, T, tm)
            filt = dict(
                w1t=jnp.zeros((HY_FILT, 40), F32).at[:, :hy_filt_w1.shape[1]].set(hy_filt_w1[j].T),
                b1=hy_filt_b1[j].reshape(-1, 1), w2t=hy_filt_w2[j].T, b2=hy_filt_b2[j].reshape(-1, 1),
                fr=hy_filt_freq[j].reshape(-1, 1),
                w3t=hy_filt_w3[j].T.reshape(2, 2, C, HY_FILT), absdelta=absdelta,
            )
            kfft = _hyena_filters_fft(filt, T, cb)
            zt = _hy_conv(ut.reshape(B, 3 * C, n1, LANES), lanes(hy_short_w[j]), lanes(hy_skip[j]),
                          kfft, B, T, C, cb)
            routed = _outproj(h, zt, cq, mkt, mv, wo[:C], wo[C:], *route_w, B, T, min(1024, T))
        else:
            rep = lambda v, n: jnp.tile(v.astype(F32), n).reshape(1, -1)
            qt, kz, vtz, cq = _at_inproj(h, row(mix_norm_g[i]), at_w_in[j].astype(BF16),
                                         rep(at_q_norm_g[j], GQA_GROUP * N_KV_HEADS),
                                         rep(at_k_norm_g[j], N_KV_HEADS), B, T, tm)
            main_t = _flash(qt, kz, vtz, B, T, min(2048, T), min(1024, T))
            routed = _outproj(h, main_t, cq, mkt, mv, wo[:C], wo[C:], *route_w, B, T, tm)
        h = _moe(*routed, exp_w_gate, exp_w_up, exp_w_down, i, B, T)
    return _final_norm(h, row(final_norm_g), tm).reshape(B, T, D)
```

```python
import functools
import math

import numpy as np
import jax
import jax.numpy as jnp
from jax import lax
from jax.experimental import pallas as pl
from jax.experimental.pallas import tpu as pltpu

F32 = jnp.float32
BF16 = jnp.bfloat16
I32 = jnp.int32

HEAD_DIM = 64
MEM_HEADS = 4
D_MEM = MEM_HEADS * HEAD_DIM
N_KV_HEADS = 4
GQA_GROUP = 3
GRID_W = 64
ROPE_THETA = 10000.0
ROPE_AXIS_DIM = HEAD_DIM // 2
HY_BANDS = 16
HY_FILT = 64
HY_DECAY_TARGET = 1e-2
HY_FAST_PCT = 0.3
HY_SLOW_PCT = 1.5
N_EXPERTS = 16
EC_CAPACITY_FACTOR = 2
NORM_EPS = 1e-6

LANES = 128
BF16_SUBLANES = 16
VMEM_LIMIT = 56 * 1024 * 1024
HI = lax.Precision.HIGHEST
LOG2E = 1.4426950408889634


def _cp(*sem):
    return pltpu.CompilerParams(dimension_semantics=sem, vmem_limit_bytes=VMEM_LIMIT)


def _rms(x, g):
    ms = jnp.mean(x * x, axis=-1, keepdims=True)
    return x * lax.rsqrt(ms + NORM_EPS) * g


def _dot(a, b):
    return jnp.dot(a, b, preferred_element_type=F32)


def _dot_nt(a, b):
    return lax.dot_general(a, b, (((1,), (1,)), ((), ())), preferred_element_type=F32)


@functools.lru_cache(maxsize=None)
def _dft_tables(T, cb):
    nf = 2 * T
    n1k = nf // LANES
    n1 = T // LANES
    k1 = np.arange(n1k)[:, None]
    a1 = 2 * np.pi * k1 * np.arange(n1)[None, :] / n1k
    f1s = np.concatenate([np.cos(a1), -np.sin(a1)], axis=0)
    a1f = 2 * np.pi * k1 * np.arange(n1k)[None, :] / n1k
    f1full = np.concatenate([np.cos(a1f), -np.sin(a1f)], axis=0)
    tw = 2 * np.pi * k1 * np.arange(LANES)[None, :] / nf
    twr, twi = np.cos(tw), -np.sin(tw)
    a2 = 2 * np.pi * np.arange(LANES)[:, None] * np.arange(LANES)[None, :] / LANES
    cr, ci = np.cos(a2), -np.sin(a2)
    w2 = np.block([[cr, ci], [-ci, cr]])
    minv = np.block([[cr, -ci], [ci, cr]])
    a3 = 2 * np.pi * np.arange(n1)[:, None] * np.arange(n1k)[None, :] / n1k
    g1 = np.concatenate([np.cos(a3), -np.sin(a3)], axis=1) / nf
    return dict(
        f1s=f1s, f1full=f1full, w2=w2, minv=minv, g1=g1,
        twr_l=np.tile(twr, (1, cb)), twi_l=np.tile(twi, (1, cb)),
        twr_r=np.tile(twr, (cb, 1)), twi_r=np.tile(twi, (cb, 1)),
    )


@functools.lru_cache(maxsize=None)
def _filter_feats(T):
    t = np.linspace(0.0, 1.0, T)[None, :]
    w = (2.0 * np.pi) * np.arange(T)[None, :] / T
    bands = np.linspace(1e-4, HY_BANDS - 1, HY_BANDS)[:, None]
    feats = np.concatenate([t, np.cos(bands * w), -np.sin(bands * w)], axis=0)
    pad = np.zeros((40 - feats.shape[0], T))
    feats = np.concatenate([feats, pad], axis=0)
    rev = (T - np.arange(T)) % T
    return (np.stack([feats, feats[:, rev]]).astype(np.float32),
            np.stack([t, t[:, rev]]).astype(np.float32))


@functools.lru_cache(maxsize=None)
def _rope_tables(T):
    rows = T // GRID_W
    pos_row = np.repeat(np.arange(rows), GRID_W).astype(np.float64)
    pos_col = np.tile(np.arange(GRID_W), rows).astype(np.float64)
    inv = 1.0 / (ROPE_THETA ** (np.arange(0, ROPE_AXIS_DIM, 2, dtype=np.float64) / ROPE_AXIS_DIM))
    lane = np.arange(LANES)
    d = lane % HEAD_DIM
    axis = d // ROPE_AXIS_DIM
    half = (d // (ROPE_AXIS_DIM // 2)) % 2
    f = d % (ROPE_AXIS_DIM // 2)
    pos = np.where(axis[None, :] == 0, pos_row[:, None], pos_col[:, None])
    ang = pos * inv[f][None, :]
    cos2 = np.cos(ang)
    sin2 = np.where(half[None, :] == 0, -np.sin(ang), np.sin(ang))
    return cos2.astype(np.float32), sin2.astype(np.float32)


@functools.lru_cache(maxsize=None)
def _spread_table(W):
    owner = np.arange(N_EXPERTS * W) // W
    hit = (np.arange(N_EXPERTS)[:, None] == owner[None, :]).astype(np.float32)
    return np.concatenate([32.0 * hit, hit], axis=0)


@functools.lru_cache(maxsize=None)
def _topk_tables(T, tb):
    nj = T // LANES
    per = tb // LANES
    r = np.arange(nj * N_EXPERTS)
    j, e = r // N_EXPERTS, r % N_EXPERTS
    same_e = e[:, None] == e[None, :]
    blk = j // per
    same_blk = blk[:, None] == blk[None, :]
    m_all = same_e & (j[None, :] < j[:, None])
    m_w = same_e & same_blk & (j[None, :] < j[:, None])
    m_b = same_e & same_blk
    tri = np.triu(np.ones((LANES, LANES)))
    return tuple(np.asarray(m, np.float32) for m in (m_all, m_w, m_b, tri))


def _hy_inproj_kernel(h_ref, g_ref, wmt_ref, wcq_ref, ut_ref, cq_ref):
    xn = _rms(h_ref[...], g_ref[...]).astype(BF16)
    ut_ref[0] = _dot_nt(wmt_ref[...], xn)
    cq_ref[...] = _dot(xn, wcq_ref[...]).astype(BF16)


def _hy_inproj(h, g, wmt, wcq, B, T, tm):
    N, D = h.shape
    c3 = wmt.shape[0]
    nt = T // tm
    return pl.pallas_call(
        _hy_inproj_kernel,
        grid=(B, nt),
        in_specs=[
            pl.BlockSpec((tm, D), lambda b, i: (b * nt + i, 0)),
            pl.BlockSpec((1, D), lambda b, i: (0, 0)),
            pl.BlockSpec((c3, D), lambda b, i: (0, 0)),
            pl.BlockSpec((D, D_MEM), lambda b, i: (0, 0)),
        ],
        out_specs=[
            pl.BlockSpec((1, c3, tm), lambda b, i: (b, 0, i)),
            pl.BlockSpec((tm, D_MEM), lambda b, i: (b * nt + i, 0)),
        ],
        out_shape=[
            jax.ShapeDtypeStruct((B, c3, T), F32),
            jax.ShapeDtypeStruct((N, D_MEM), BF16),
        ],
        compiler_params=_cp("parallel", "parallel"),
        name="hy_inproj",
    )(h, g, wmt, wcq)


def _filt_mlp_kernel(feats_ref, w1t_ref, b1_ref, w2t_ref, b2_ref, fr_ref, h2_ref):
    fr = fr_ref[...]
    for d in range(2):
        a = jnp.dot(w1t_ref[...], feats_ref[d], precision=HI, preferred_element_type=F32)
        h1 = jnp.sin(fr * (a + b1_ref[...]))
        a = jnp.dot(w2t_ref[...], h1, precision=HI, preferred_element_type=F32)
        h2_ref[d] = jnp.sin(fr * (a + b2_ref[...]))


def _filt_taps_kernel(h2_ref, w3t_ref, t_ref, dl_ref, out_ref):
    T = h2_ref.shape[2]
    dl = dl_ref[...]
    hf = jnp.dot(w3t_ref[0, 0], h2_ref[0], precision=HI, preferred_element_type=F32) * jnp.exp(-t_ref[0] * dl)
    hb = jnp.dot(w3t_ref[0, 1], h2_ref[1], precision=HI, preferred_element_type=F32) * jnp.exp(-t_ref[1] * dl)
    nrm = jnp.sum(jnp.abs(hf) + jnp.abs(hb), axis=-1, keepdims=True) + 1e-6
    inv = 1.0 / nrm
    tap0 = lax.broadcasted_iota(I32, hf.shape, 1) == 0
    out_ref[0, :, :T] = ((hf + jnp.where(tap0, hb, 0.0)) * inv).astype(BF16)
    out_ref[0, :, T:] = (jnp.where(tap0, 0.0, hb) * inv).astype(BF16)


def _fwd_fft(x3, f1s, twr, twi, w2, cb, n1k):
    rhs = jnp.concatenate([x3[c].astype(BF16) for c in range(cb)], axis=1)
    a = _dot(f1s, rhs)
    ar, ai = a[:n1k], a[n1k:]
    tr = (ar * twr - ai * twi).astype(BF16)
    ti = (ar * twi + ai * twr).astype(BF16)
    lr = jnp.concatenate([tr[:, c * LANES:(c + 1) * LANES] for c in range(cb)], axis=0)
    li = jnp.concatenate([ti[:, c * LANES:(c + 1) * LANES] for c in range(cb)], axis=0)
    return _dot(jnp.concatenate([lr, li], axis=1), w2)


def _inv_fft(y, minv, twr, twi, g1, cb, n1k):
    b = _dot(y.astype(BF16), minv)
    br, bi = b[:, :LANES], b[:, LANES:]
    pr = (br * twr + bi * twi).astype(BF16)
    pi = (bi * twr - br * twi).astype(BF16)
    top = jnp.concatenate([pr[c * n1k:(c + 1) * n1k] for c in range(cb)], axis=1)
    bot = jnp.concatenate([pi[c * n1k:(c + 1) * n1k] for c in range(cb)], axis=1)
    rhs = jnp.concatenate([top, bot], axis=0)
    return _dot(g1, rhs)


def _filt_fft_kernel(taps_ref, f1_ref, twr_ref, twi_ref, w2_ref, k_ref, *, cb, n1k):
    k_ref[0] = _fwd_fft(taps_ref[0], f1_ref[...], twr_ref[...], twi_ref[...], w2_ref[...], cb, n1k).astype(BF16)


def _hyena_filters_fft(p, T, cb):
    feats, t_rows = _filter_feats(T)
    C = p["w3t"].shape[2]
    tabs = _dft_tables(T, cb)
    n1k = 2 * T // LANES
    h2 = pl.pallas_call(
        _filt_mlp_kernel,
        out_shape=jax.ShapeDtypeStruct((2, HY_FILT, T), F32),
        compiler_params=pltpu.CompilerParams(vmem_limit_bytes=VMEM_LIMIT),
        name="hy_filt_mlp",
    )(jnp.asarray(feats), p["w1t"], p["b1"], p["w2t"], p["b2"], p["fr"])
    cbt = 64
    taps = pl.pallas_call(
        _filt_taps_kernel,
        grid=(2, C // cbt),
        in_specs=[
            pl.BlockSpec((2, HY_FILT, T), lambda o, c: (0, 0, 0)),
            pl.BlockSpec((1, 2, cbt, HY_FILT), lambda o, c: (o, 0, c, 0)),
            pl.BlockSpec((2, 1, T), lambda o, c: (0, 0, 0)),
            pl.BlockSpec((cbt, 1), lambda o, c: (c, 0)),
        ],
        out_specs=pl.BlockSpec((1, cbt, 2 * T), lambda o, c: (o, c, 0)),
        out_shape=jax.ShapeDtypeStruct((2, C, 2 * T), BF16),
        compiler_params=_cp("parallel", "parallel"),
        name="hy_filt_taps",
    )(h2, p["w3t"], jnp.asarray(t_rows), p["absdelta"])
    taps = taps.reshape(2, C, n1k, LANES)
    const = lambda shape: pl.BlockSpec(shape, lambda o, c: (0,) * len(shape))
    f32 = lambda k: jnp.asarray(tabs[k], F32)
    return pl.pallas_call(
        functools.partial(_filt_fft_kernel, cb=cb, n1k=n1k),
        grid=(2, C // cb),
        in_specs=[
            pl.BlockSpec((1, cb, n1k, LANES), lambda o, c: (o, c, 0, 0)),
            const((2 * n1k, n1k)), const((n1k, cb * LANES)), const((n1k, cb * LANES)),
            const((2 * LANES, 2 * LANES)),
        ],
        out_specs=pl.BlockSpec((1, cb * n1k, 2 * LANES), lambda o, c: (o, c, 0)),
        out_shape=jax.ShapeDtypeStruct((2, C * n1k, 2 * LANES), BF16),
        compiler_params=_cp("parallel", "parallel"),
        name="hy_filt_fft",
    )(taps, f32("f1full").astype(BF16), f32("twr_l"), f32("twi_l"), f32("w2").astype(BF16))


def _time_neighbours(x):
    rows = x.shape[0]
    lane = lax.broadcasted_iota(I32, x.shape, 1)
    r = pltpu.roll(x, 1, 1)
    rr = pltpu.roll(r, 1, 0)
    prev = jnp.where(lane == 0, rr, r)
    r2 = pltpu.roll(x, LANES - 1, 1)
    rr2 = pltpu.roll(r2, rows - 1, 0)
    nxt = jnp.where(lane == LANES - 1, rr2, r2)
    return prev, nxt


def _hy_conv_kernel(x1_ref, x2_ref, v_ref, sw1_ref, sw2_ref, swv_ref, skip_ref, k_ref,
                    f1s_ref, twrl_ref, twil_ref, w2_ref, minv_ref, twrr_ref, twir_ref,
                    g1_ref, o_ref, *, cb, n1k, n1):
    rows = cb * n1
    shape2 = (rows, LANES)
    row = lax.broadcasted_iota(I32, shape2, 0)
    lane = lax.broadcasted_iota(I32, shape2, 1)
    first = (lane == 0) & (row % n1 == 0)
    last = (lane == LANES - 1) & (row % n1 == n1 - 1)

    def sconv(x_ref, sw_ref):
        x = x_ref[0].reshape(shape2)
        prev, nxt = _time_neighbours(x)
        prev = jnp.where(first, 0.0, prev)
        nxt = jnp.where(last, 0.0, nxt)
        w = [jnp.broadcast_to(sw_ref[j], (cb, n1, LANES)).reshape(shape2) for j in range(3)]
        return prev * w[0] + x * w[1] + nxt * w[2]

    z = sconv(v_ref, swv_ref)
    gates = (sconv(x1_ref, sw1_ref), sconv(x2_ref, sw2_ref))
    f1s, w2 = f1s_ref[...], w2_ref[...]
    for o in range(2):
        zf = _fwd_fft(z.reshape(cb, n1, LANES), f1s, twrl_ref[...], twil_ref[...], w2, cb, n1k)
        kk = k_ref[o].astype(F32)
        zr, zi = zf[:, :LANES], zf[:, LANES:]
        kr, ki = kk[:, :LANES], kk[:, LANES:]
        y = jnp.concatenate([zr * kr - zi * ki, zr * ki + zi * kr], axis=1)
        conv = _inv_fft(y, minv_ref[...], twrr_ref[...], twir_ref[...], g1_ref[...], cb, n1k)
        conv = jnp.concatenate([conv[:, c * LANES:(c + 1) * LANES] for c in range(cb)], axis=0)
        skip = jnp.broadcast_to(skip_ref[o], (cb, n1, LANES)).reshape(shape2)
        z = gates[o] * (conv + skip * z)
    o_ref[0] = z.reshape(cb, n1, LANES)


def _hy_conv(ut4, sw, skip, kfft, B, T, C, cb):
    n1k, n1 = 2 * T // LANES, T // LANES
    tabs = _dft_tables(T, cb)
    nct = C // cb
    bf = lambda k: jnp.asarray(tabs[k], F32).astype(BF16)
    f32 = lambda k: jnp.asarray(tabs[k], F32)
    const = lambda shape: pl.BlockSpec(shape, lambda c, b: (0,) * len(shape))
    ublk = lambda s: pl.BlockSpec((1, cb, n1, LANES), lambda c, b, s=s: (b, s * nct + c, 0, 0))
    wblk = lambda s: pl.BlockSpec((3, cb, 1, LANES), lambda c, b, s=s: (0, s * nct + c, 0, 0))
    return pl.pallas_call(
        functools.partial(_hy_conv_kernel, cb=cb, n1k=n1k, n1=n1),
        grid=(nct, B),
        in_specs=[
            ublk(0), ublk(1), ublk(2), wblk(0), wblk(1), wblk(2),
            pl.BlockSpec((2, cb, 1, LANES), lambda c, b: (0, c, 0, 0)),
            pl.BlockSpec((2, cb * n1k, 2 * LANES), lambda c, b: (0, c, 0)),
            const((2 * n1k, n1)), const((n1k, cb * LANES)), const((n1k, cb * LANES)),
            const((2 * LANES, 2 * LANES)), const((2 * LANES, 2 * LANES)),
            const((cb * n1k, LANES)), const((cb * n1k, LANES)), const((n1, 2 * n1k)),
        ],
        out_specs=pl.BlockSpec((1, cb, n1, LANES), lambda c, b: (b, c, 0, 0)),
        out_shape=jax.ShapeDtypeStruct((B, C, n1, LANES), F32),
        compiler_params=_cp("parallel", "arbitrary"),
        name="hy_conv",
    )(ut4, ut4, ut4, sw, sw, sw, skip, kfft,
      bf("f1s"), f32("twr_l"), f32("twi_l"), bf("w2"), bf("minv"),
      f32("twr_r"), f32("twi_r"), bf("g1"))


def _head_norm(x, bd, g):
    ss = _dot((x * x).astype(BF16), bd)
    return x * lax.rsqrt(ss * (1.0 / HEAD_DIM) + NORM_EPS) * g


def _rope(x, cos2, sin2):
    lane = lax.broadcasted_iota(I32, cos2.shape, 1)
    low = (lane // (ROPE_AXIS_DIM // 2)) % 2 == 0
    out = []
    for c in range(x.shape[1] // LANES):
        xc = x[:, c * LANES:(c + 1) * LANES]
        up = pltpu.roll(xc, LANES - 16, 1)
        dn = pltpu.roll(xc, 16, 1)
        out.append(xc * cos2 + jnp.where(low, up, dn) * sin2)
    return jnp.concatenate(out, axis=1)


def _at_inproj_kernel(h_ref, g_ref, w_ref, bdq_ref, bdk_ref, gq_ref, gk_ref, cos_ref, sin_ref,
                      qt_ref, kz_ref, vt_ref, cq_ref, *, dq, dk):
    xn = _rms(h_ref[...], g_ref[...]).astype(BF16)
    proj = _dot(xn, w_ref[...])
    q, k = proj[:, :dq], proj[:, dq:dq + dk]
    v, cq = proj[:, dq + dk:dq + 2 * dk], proj[:, dq + 2 * dk:]
    cos2, sin2 = cos_ref[...], sin_ref[...]
    qr = _rope(_head_norm(q, bdq_ref[...], gq_ref[...]), cos2, sin2)
    qt = (qr * (HEAD_DIM ** -0.5 * LOG2E)).T.astype(BF16)
    for p in range(dq // LANES):
        qt_ref[0, p] = qt[p * LANES:(p + 1) * LANES]
    kr = _rope(_head_norm(k, bdk_ref[...], gk_ref[...]), cos2, sin2)
    vt = v.T.astype(BF16)
    lane = lax.broadcasted_iota(I32, (k.shape[0], LANES), 1)
    for kv in range(N_KV_HEADS):
        vt_ref[0, kv] = vt[kv * HEAD_DIM:(kv + 1) * HEAD_DIM]
        pair = kr[:, (kv // 2) * LANES:(kv // 2 + 1) * LANES]
        own = jnp.where((lane < HEAD_DIM) == (kv % 2 == 0), pair, 0.0)
        other = pltpu.roll(own, HEAD_DIM, 1)
        lo, hi = (own, other) if kv % 2 == 0 else (other, own)
        kz_ref[0, kv, 0] = lo.astype(BF16)
        kz_ref[0, kv, 1] = hi.astype(BF16)
    cq_ref[...] = cq.astype(BF16)


def _at_inproj(h, g, w, gq, gk, B, T, tm):
    N, D = h.shape
    dq, dk = GQA_GROUP * N_KV_HEADS * HEAD_DIM, N_KV_HEADS * HEAD_DIM
    nt = T // tm
    cos2, sin2 = _rope_tables(T)
    bd = lambda n: jnp.asarray(np.kron(np.eye(n // HEAD_DIM), np.ones((HEAD_DIM, HEAD_DIM))), F32).astype(BF16)
    const = lambda shape: pl.BlockSpec(shape, lambda b, i: (0,) * len(shape))
    return pl.pallas_call(
        functools.partial(_at_inproj_kernel, dq=dq, dk=dk),
        grid=(B, nt),
        in_specs=[
            pl.BlockSpec((tm, D), lambda b, i: (b * nt + i, 0)),
            const((1, D)), const(w.shape), const((dq, dq)), const((dk, dk)),
            const((1, dq)), const((1, dk)),
            pl.BlockSpec((tm, LANES), lambda b, i: (i, 0)),
            pl.BlockSpec((tm, LANES), lambda b, i: (i, 0)),
        ],
        out_specs=[
            pl.BlockSpec((1, dq // LANES, LANES, tm), lambda b, i: (b, 0, 0, i)),
            pl.BlockSpec((1, N_KV_HEADS, 2, tm, LANES), lambda b, i: (b, 0, 0, i, 0)),
            pl.BlockSpec((1, N_KV_HEADS, HEAD_DIM, tm), lambda b, i: (b, 0, 0, i)),
            pl.BlockSpec((tm, D_MEM), lambda b, i: (b * nt + i, 0)),
        ],
        out_shape=[
            jax.ShapeDtypeStruct((B, dq // LANES, LANES, T), BF16),
            jax.ShapeDtypeStruct((B, N_KV_HEADS, 2, T, LANES), BF16),
            jax.ShapeDtypeStruct((B, N_KV_HEADS, HEAD_DIM, T), BF16),
            jax.ShapeDtypeStruct((N, D_MEM), BF16),
        ],
        compiler_params=_cp("parallel", "parallel"),
        name="at_inproj",
    )(h, g, w, bd(dq), bd(dk), gq, gk, jnp.asarray(cos2), jnp.asarray(sin2))


def _flash_kernel(qt_ref, ka_ref, kb_ref, vta_ref, vtb_ref, o_ref, m_sc, l_sc, acc_sc, s_sc, p_sc):
    ki = pl.program_id(3)

    @pl.when(ki == 0)
    def _():
        m_sc[...] = jnp.full(m_sc.shape, -jnp.inf, F32)
        l_sc[...] = jnp.zeros(l_sc.shape, F32)
        acc_sc[...] = jnp.zeros(acc_sc.shape, F32)

    qt = qt_ref[0, 0]
    tk, tq = s_sc.shape[1], s_sc.shape[2]
    sub = 8
    ck = 2 * sub
    pv, alphas = [], []
    for idx, k_ref in enumerate((ka_ref, kb_ref)):
        s_sc[idx] = _dot(k_ref[0, 0, 0], qt)
    for idx, vt_ref in enumerate((vta_ref, vtb_ref)):
        mx = s_sc[idx, 0:sub, :]
        for c in range(1, tk // sub):
            mx = jnp.maximum(mx, s_sc[idx, c * sub:(c + 1) * sub, :])
        m_prev = m_sc[idx]
        m_new = jnp.maximum(m_prev, jnp.max(mx, axis=0, keepdims=True))
        alpha = jnp.exp2(m_prev - m_new)
        lsum = jnp.zeros((sub, tq), F32)
        for c in range(tk // ck):
            p = jnp.exp2(s_sc[idx, c * ck:(c + 1) * ck, :] - m_new)
            lsum = lsum + p[:sub] + p[sub:]
            p_sc[idx, c * ck:(c + 1) * ck, :] = p.astype(BF16)
        l_sc[idx] = alpha * l_sc[idx] + jnp.sum(lsum, axis=0, keepdims=True)
        m_sc[idx] = m_new
        pv.append(_dot(vt_ref[0, 0], p_sc[idx]))
        alphas.append(alpha)
    row = lax.broadcasted_iota(I32, acc_sc.shape, 0)
    low = row < HEAD_DIM
    acc_sc[...] = acc_sc[...] * jnp.where(low, alphas[0], alphas[1]) + jnp.concatenate(pv, axis=0)

    @pl.when(ki == pl.num_programs(3) - 1)
    def _():
        o_ref[0] = acc_sc[...] / jnp.where(low, l_sc[0], l_sc[1])


def _flash(qt, kz, vtz, B, T, tq, tk):
    npair = qt.shape[1]
    nq, nk = T // tq, T // tk
    kv_of = lambda p, j: (2 * p + j) // GQA_GROUP
    return pl.pallas_call(
        _flash_kernel,
        grid=(B, npair, nq, nk),
        in_specs=[
            pl.BlockSpec((1, 1, LANES, tq), lambda b, p, i, k: (b, p, 0, i)),
            pl.BlockSpec((1, 1, 1, tk, LANES), lambda b, p, i, k: (b, kv_of(p, 0), 0, k, 0)),
            pl.BlockSpec((1, 1, 1, tk, LANES), lambda b, p, i, k: (b, kv_of(p, 1), 1, k, 0)),
            pl.BlockSpec((1, 1, HEAD_DIM, tk), lambda b, p, i, k: (b, kv_of(p, 0), 0, k)),
            pl.BlockSpec((1, 1, HEAD_DIM, tk), lambda b, p, i, k: (b, kv_of(p, 1), 0, k)),
        ],
        out_specs=pl.BlockSpec((1, LANES, tq), lambda b, p, i, k: (b, p, i)),
        out_shape=jax.ShapeDtypeStruct((B, npair * LANES, T), F32),
        scratch_shapes=[
            pltpu.VMEM((2, 1, tq), F32), pltpu.VMEM((2, 1, tq), F32), pltpu.VMEM((LANES, tq), F32),
            pltpu.VMEM((2, tk, tq), F32), pltpu.VMEM((2, tk, tq), BF16),
        ],
        compiler_params=_cp("parallel", "parallel", "parallel", "arbitrary"),
        name="flash_gqa",
    )(qt, kz, kz, vtz, vtz)


def _memkv_kernel(mem_ref, g_ref, wkvt_ref, wv_ref, mkt_ref, mv_ref):
    mn = _rms(mem_ref[0], g_ref[...]).astype(BF16)
    kt = _dot_nt(wkvt_ref[...], mn) * (HEAD_DIM ** -0.5)
    v = _dot(mn, wv_ref[...])
    row = lax.broadcasted_iota(I32, kt.shape, 0)
    col = lax.broadcasted_iota(I32, v.shape, 1)
    for hd in range(MEM_HEADS):
        mkt_ref[0, hd] = jnp.where(row // HEAD_DIM == hd, kt, 0.0).astype(BF16)
        mv_ref[0, hd] = jnp.where(col // HEAD_DIM == hd, v, 0.0).astype(BF16)


def _memkv(mem, g, wkt, wv):
    B, M, D = mem.shape
    return pl.pallas_call(
        _memkv_kernel,
        grid=(B,),
        in_specs=[
            pl.BlockSpec((1, M, D), lambda b: (b, 0, 0)),
            pl.BlockSpec((1, D), lambda b: (0, 0)),
            pl.BlockSpec((D_MEM, D), lambda b: (0, 0)),
            pl.BlockSpec((D, D_MEM), lambda b: (0, 0)),
        ],
        out_specs=[
            pl.BlockSpec((1, MEM_HEADS, D_MEM, M), lambda b: (b, 0, 0, 0)),
            pl.BlockSpec((1, MEM_HEADS, M, D_MEM), lambda b: (b, 0, 0, 0)),
        ],
        out_shape=[
            jax.ShapeDtypeStruct((B, MEM_HEADS, D_MEM, M), BF16),
            jax.ShapeDtypeStruct((B, MEM_HEADS, M, D_MEM), BF16),
        ],
        compiler_params=_cp("parallel"),
        name="mem_kv",
    )(mem, g, wkt, wv)


def _cross_attn(cq, mkt_ref, mv_ref):
    acc = jnp.zeros(cq.shape, F32)
    for hd in range(MEM_HEADS):
        s = _dot(cq, mkt_ref[0, hd])
        p = jnp.exp(s - jnp.max(s, axis=-1, keepdims=True))
        p = p / jnp.sum(p, axis=-1, keepdims=True)
        acc = acc + _dot(p.astype(BF16), mv_ref[0, hd])
    return acc


def _route(h, g_ref, wr_ref, wrt_ref, xn_ref, gsp_ref, afft_ref):
    xn = _rms(h, g_ref[...]).astype(BF16)
    xn_ref[...] = xn
    lg = _dot(xn, wr_ref[...])
    lane = lax.broadcasted_iota(I32, lg.shape, 1)
    lg = jnp.where(lane < N_EXPERTS, lg, -jnp.inf)
    p = jnp.exp(lg - jnp.max(lg, axis=-1, keepdims=True))
    aff = p / jnp.sum(p, axis=-1, keepdims=True)
    hi = aff.astype(BF16).astype(F32)
    mid = (aff - hi).astype(BF16).astype(F32)
    lo = (aff - hi - mid).astype(BF16).astype(F32)
    gsp_ref[...] = (hi + pltpu.roll(mid, N_EXPERTS, 1) + pltpu.roll(lo, 2 * N_EXPERTS, 1)).astype(BF16)
    lt = _dot_nt(wrt_ref[...], xn)
    pt = jnp.exp(lt - jnp.max(lt, axis=0, keepdims=True))
    pt = pt / jnp.sum(pt, axis=0, keepdims=True)
    for j in range(lt.shape[1] // LANES):
        afft_ref[0, j] = pt[:, j * LANES:(j + 1) * LANES]


def _outproj_kernel(h_ref, main_ref, cq_ref, mkt_ref, mv_ref, wm_ref, wc_ref, g_ref, wr_ref, wrt_ref,
                    o_ref, xn_ref, gsp_ref, afft_ref):
    if len(main_ref.shape) == 3:
        main = main_ref[0].T
    else:
        main = jnp.concatenate([main_ref[0, :, j, :].T for j in range(main_ref.shape[2])], axis=0)
    main = main.astype(BF16)
    cross = _cross_attn(cq_ref[...], mkt_ref, mv_ref).astype(BF16)
    h_new = h_ref[...] + _dot(main, wm_ref[...]) + _dot(cross, wc_ref[...])
    o_ref[...] = h_new
    _route(h_new, g_ref, wr_ref, wrt_ref, xn_ref, gsp_ref, afft_ref)


def _outproj(h, main, cq, mkt, mv, wm, wc, g, wr, wrt, B, T, tm):
    N, D = h.shape
    C = wm.shape[0]
    M = mkt.shape[-1]
    nt = T // tm
    if main.ndim == 3:
        mspec = pl.BlockSpec((1, C, tm), lambda b, i: (b, 0, i))
    else:
        mspec = pl.BlockSpec((1, C, tm // LANES, LANES), lambda b, i: (b, 0, i, 0))
    return pl.pallas_call(
        _outproj_kernel,
        grid=(B, nt),
        in_specs=[
            pl.BlockSpec((tm, D), lambda b, i: (b * nt + i, 0)),
            mspec,
            pl.BlockSpec((tm, D_MEM), lambda b, i: (b * nt + i, 0)),
            pl.BlockSpec((1, MEM_HEADS, D_MEM, M), lambda b, i: (b, 0, 0, 0)),
            pl.BlockSpec((1, MEM_HEADS, M, D_MEM), lambda b, i: (b, 0, 0, 0)),
            pl.BlockSpec((C, D), lambda b, i: (0, 0)),
            pl.BlockSpec((D_MEM, D), lambda b, i: (0, 0)),
            pl.BlockSpec((1, D), lambda b, i: (0, 0)),
            pl.BlockSpec((D, LANES), lambda b, i: (0, 0)),
            pl.BlockSpec((N_EXPERTS, D), lambda b, i: (0, 0)),
        ],
        out_specs=[
            pl.BlockSpec((tm, D), lambda b, i: (b * nt + i, 0)),
            pl.BlockSpec((tm, D), lambda b, i: (b * nt + i, 0)),
            pl.BlockSpec((tm, LANES), lambda b, i: (b * nt + i, 0)),
            pl.BlockSpec((1, tm // LANES, N_EXPERTS, LANES), lambda b, i: (b, i, 0, 0)),
        ],
        out_shape=[
            jax.ShapeDtypeStruct((N, D), F32),
            jax.ShapeDtypeStruct((N, D), BF16),
            jax.ShapeDtypeStruct((N, LANES), BF16),
            jax.ShapeDtypeStruct((B, T // LANES, N_EXPERTS, LANES), F32),
        ],
        compiler_params=_cp("parallel", "parallel"),
        name="outproj",
    )(h, main, cq, mkt, mv, wm, wc, g, wr, wrt)


def _topk_kernel(aff_ref, mall_ref, mw_ref, mb_ref, tri_ref, pos_ref, aoff_ref, cnt_ref, *, cap, nj):
    E = N_EXPERTS
    aff3 = aff_ref[0]

    def count(mask3):
        per = jnp.sum(mask3.astype(F32), axis=0)
        return jnp.broadcast_to(jnp.sum(per, axis=-1, keepdims=True), (E, LANES))

    def step(i, thr):
        cand = thr | lax.shift_left(jnp.int32(1), 30 - i)
        ok = count(aff3 >= pltpu.bitcast(cand, F32)[None]) >= cap
        return jnp.where(ok, cand, thr)

    thr = lax.fori_loop(0, 31, step, jnp.zeros((E, LANES), I32))
    thr = pltpu.bitcast(thr, F32)
    gt3 = aff3 > thr[None]
    eq3 = aff3 == thr[None]
    need = cap - count(gt3)

    ones = jnp.ones((LANES, LANES), BF16)
    tri = tri_ref[...]

    def prefix(mask2):
        mb = mask2.astype(BF16)
        incl = _dot(mb, tri)
        tot = _dot(mb, ones)
        return incl - mask2, tot

    eq2 = eq3.reshape(nj * E, LANES).astype(F32)
    ex, tot = prefix(eq2)
    eq_rank = ex + _dot(mall_ref[...], tot.astype(BF16))
    need2 = jnp.broadcast_to(need[None], (nj, E, LANES)).reshape(nj * E, LANES)
    sel = jnp.where((gt3.reshape(nj * E, LANES)) | ((eq2 > 0) & (eq_rank < need2)), 1.0, 0.0)

    ex, tot = prefix(sel)
    totb = tot.astype(BF16)
    before = _dot(mall_ref[...], totb)
    within = _dot(mw_ref[...], totb)
    cnt = _dot(mb_ref[...], totb)
    pos_ref[0] = jnp.where(sel > 0, (before + ex).astype(I32), -1).reshape(nj, E, LANES)
    aoff_ref[0] = (before - within).astype(I32).reshape(nj, E, LANES)
    cnt_ref[0] = cnt.astype(I32).reshape(nj, E, LANES)


def _topk(aff4, T, tb):
    B, nj = aff4.shape[0], aff4.shape[1]
    cap = EC_CAPACITY_FACTOR * T // N_EXPERTS
    tabs = [jnp.asarray(m, F32).astype(BF16) for m in _topk_tables(T, tb)]
    R = nj * N_EXPERTS
    blk = pl.BlockSpec((1, nj, N_EXPERTS, LANES), lambda b: (b, 0, 0, 0))
    const = lambda shape: pl.BlockSpec(shape, lambda b: (0,) * len(shape))
    out = jax.ShapeDtypeStruct((B, nj, N_EXPERTS, LANES), I32)
    return pl.pallas_call(
        functools.partial(_topk_kernel, cap=cap, nj=nj),
        grid=(B,),
        in_specs=[blk, const((R, R)), const((R, R)), const((R, R)), const((LANES, LANES))],
        out_specs=[blk, blk, blk],
        out_shape=[out, out, out],
        compiler_params=_cp("parallel"),
        name="moe_topk",
    )(aff4, *tabs)


def _compress_kernel(aoff_ref, nch_ref, nmax_ref, xn_ref, gsp_ref, pos_ref, xg_ref, gs_ref,
                     *, nblk, tb, W, eg, nsub):
    b, grp, blk = pl.program_id(0), pl.program_id(1), pl.program_id(2)

    @pl.when(blk == 0)
    def _():
        xg_ref[...] = jnp.zeros(xg_ref.shape, BF16)
        gs_ref[...] = jnp.zeros(gs_ref.shape, F32)

    iota_s = lax.broadcasted_iota(I32, (W, LANES), 0)
    per = tb // LANES
    for sb in range(nsub):
        tblk = blk * nsub + sb
        base = (b * nblk + tblk) * N_EXPERTS + grp * eg
        tok = slice(sb * tb, (sb + 1) * tb)

        def chunk(c, carry, base=base, tok=tok, sb=sb):
            pieces = []
            for i in range(eg):
                a = aoff_ref[base + i] + c * W
                e = grp * eg + i
                g = [pos_ref[0, sb * per + jj, pl.ds(e, 1), :] - a == iota_s for jj in range(per)]
                pieces.append(jnp.where(jnp.concatenate(g, axis=1), 1.0, 0.0).astype(BF16))
            lhs = jnp.concatenate(pieces, axis=0)
            res = _dot(lhs, xn_ref[tok, :])
            resg = _dot(lhs, gsp_ref[tok, :])
            for i in range(eg):
                @pl.when(c < nch_ref[base + i])
                def _():
                    win = pl.ds(pl.multiple_of(aoff_ref[base + i] + c * W, BF16_SUBLANES), W)
                    xg_ref[0, i, win, :] = (xg_ref[0, i, win, :] + res[i * W:(i + 1) * W]).astype(BF16)
                    gs_ref[0, i, win, :] = gs_ref[0, i, win, :] + resg[i * W:(i + 1) * W]
            return carry

        lax.fori_loop(0, nmax_ref[b * nblk + tblk], chunk, 0)


def _compress(aoff, nch, nmax, xn, gsp, pos4, B, T, tb, rows, W):
    N, D = xn.shape
    nblk = T // tb
    eg = 4
    nsub = 2 if nblk % 2 == 0 else 1
    ns, ts = nblk // nsub, nsub * tb
    return pl.pallas_call(
        functools.partial(_compress_kernel, nblk=nblk, tb=tb, W=W, eg=eg, nsub=nsub),
        grid_spec=pltpu.PrefetchScalarGridSpec(
            num_scalar_prefetch=3,
            grid=(B, N_EXPERTS // eg, ns),
            in_specs=[
                pl.BlockSpec((ts, D), lambda b, g, k, *_: (b * ns + k, 0)),
                pl.BlockSpec((ts, LANES), lambda b, g, k, *_: (b * ns + k, 0)),
                pl.BlockSpec((1, ts // LANES, N_EXPERTS, LANES), lambda b, g, k, *_: (b, k, 0, 0)),
            ],
            out_specs=[
                pl.BlockSpec((1, eg, rows, D), lambda b, g, k, *_: (b, g, 0, 0)),
                pl.BlockSpec((1, eg, rows, LANES), lambda b, g, k, *_: (b, g, 0, 0)),
            ],
        ),
        out_shape=[
            jax.ShapeDtypeStruct((B, N_EXPERTS, rows, D), BF16),
            jax.ShapeDtypeStruct((B, N_EXPERTS, rows, LANES), F32),
        ],
        compiler_params=_cp("parallel", "parallel", "arbitrary"),
        name="moe_compress",
    )(aoff, nch, nmax, xn, gsp, pos4)


def _ffn_kernel(x_ref, gs_ref, wg_ref, wu_ref, wd_ref, y_ref, *, live):
    e = pl.program_id(0)
    x = x_ref[0, 0, :live]
    g = _dot(x, wg_ref[0, 0].astype(BF16))
    u = _dot(x, wu_ref[0, 0].astype(BF16))
    hid = (g * (1.0 / (1.0 + jnp.exp(-g))) * u).astype(BF16)
    gs = gs_ref[0, 0, :live]
    lane = lax.broadcasted_iota(I32, gs.shape, 1)
    mine = (lane == e) | (lane == e + N_EXPERTS) | (lane == e + 2 * N_EXPERTS)
    gate = jnp.sum(jnp.where(mine, gs, 0.0), axis=-1, keepdims=True)
    y_ref[0, 0, :live] = (_dot(hid, wd_ref[0, 0].astype(BF16)) * gate).astype(BF16)
    y_ref[0, 0, live:] = jnp.zeros((y_ref.shape[2] - live, y_ref.shape[3]), BF16)


def _ffn(xg, gs, wg, wu, wd, layer, live):
    B, E, rows, D = xg.shape
    F = wg.shape[-1]
    return pl.pallas_call(
        functools.partial(_ffn_kernel, live=live),
        grid=(E, B),
        in_specs=[
            pl.BlockSpec((1, 1, rows, D), lambda e, b: (b, e, 0, 0)),
            pl.BlockSpec((1, 1, rows, LANES), lambda e, b: (b, e, 0, 0)),
            pl.BlockSpec((1, 1, D, F), lambda e, b: (layer, e, 0, 0)),
            pl.BlockSpec((1, 1, D, F), lambda e, b: (layer, e, 0, 0)),
            pl.BlockSpec((1, 1, F, D), lambda e, b: (layer, e, 0, 0)),
        ],
        out_specs=pl.BlockSpec((1, 1, rows, D), lambda e, b: (b, e, 0, 0)),
        out_shape=jax.ShapeDtypeStruct((B, E, rows, D), BF16),
        compiler_params=_cp("parallel", "arbitrary"),
        name="moe_ffn",
    )(xg, gs, wg, wu, wd)


def _expand_kernel(aoff_ref, nch_ref, nmax_ref, h_ref, post_ref, spread_ref, slot1_ref, y_ref, o_ref,
                   *, nblk, tb, W, rows):
    b, blk = pl.program_id(0), pl.program_id(2)
    base = (b * nblk + blk) * N_EXPERTS
    ngrp = N_EXPERTS * W // LANES
    lane = lax.broadcasted_iota(I32, (1, LANES), 1)
    rel = _dot(post_ref[...], spread_ref[...]) - slot1_ref[...]
    never = jnp.float32(-2.0 ** 20)

    def chunk(c, acc):
        tgt, wins = [], []
        for e in range(N_EXPERTS):
            a = aoff_ref[base + e] + c * W
            tgt.append(jnp.where(c < nch_ref[base + e], a.astype(F32), never))
            a_in = pl.multiple_of(jnp.minimum(a, rows - W), BF16_SUBLANES)
            wins.append(y_ref[0, e, pl.ds(a_in, W), :])
        cols = []
        for g in range(ngrp):
            first, last = (LANES * g) // W, (LANES * g + LANES - 1) // W
            t = jnp.full((1, LANES), tgt[last], F32)
            for e in range(last - 1, first - 1, -1):
                t = jnp.where(lane < (e + 1) * W - LANES * g, tgt[e], t)
            hit = rel[:, g * LANES:(g + 1) * LANES] == t
            cols.append(jnp.where(hit, 1.0, 0.0).astype(BF16))
        return acc + _dot(jnp.concatenate(cols, axis=1), jnp.concatenate(wins, axis=0))

    acc = lax.fori_loop(0, nmax_ref[b * nblk + blk], chunk, jnp.zeros(o_ref.shape, F32))
    o_ref[...] = h_ref[...] + acc


def _expand(aoff, nch, nmax, h, post, y, B, T, tb, dw, W):
    N, D = h.shape
    rows = y.shape[2]
    nblk = T // tb
    return pl.pallas_call(
        functools.partial(_expand_kernel, nblk=nblk, tb=tb, W=W, rows=rows),
        grid_spec=pltpu.PrefetchScalarGridSpec(
            num_scalar_prefetch=3,
            grid=(B, D // dw, nblk),
            in_specs=[
                pl.BlockSpec((tb, dw), lambda b, d, k, *_: (b * nblk + k, d)),
                pl.BlockSpec((tb, 2 * N_EXPERTS), lambda b, d, k, *_: (b * nblk + k, 0)),
                pl.BlockSpec((2 * N_EXPERTS, N_EXPERTS * W), lambda b, d, k, *_: (0, 0)),
                pl.BlockSpec((1, N_EXPERTS * W), lambda b, d, k, *_: (0, 0)),
                pl.BlockSpec((1, N_EXPERTS, rows, dw), lambda b, d, k, *_: (b, 0, 0, d),
                             pipeline_mode=pl.Buffered(1)),
            ],
            out_specs=pl.BlockSpec((tb, dw), lambda b, d, k, *_: (b * nblk + k, d)),
        ),
        out_shape=jax.ShapeDtypeStruct((N, D), F32),
        compiler_params=_cp("parallel", "parallel", "arbitrary"),
        name="moe_expand",
    )(aoff, nch, nmax, h, post, jnp.asarray(_spread_table(W), F32).astype(BF16),
      jnp.asarray((np.arange(N_EXPERTS * W) % W + 1).reshape(1, -1), F32), y)


def _moe(h, xn, gsp, aff4, wg, wu, wd, layer, B, T):
    N, D = h.shape
    tb = min(512, T)
    W = 96
    nblk = T // tb
    cap = EC_CAPACITY_FACTOR * T // N_EXPERTS
    live = -(-cap // BF16_SUBLANES) * BF16_SUBLANES
    rows = -(-(live + W) // LANES) * LANES
    dw = 256
    pos4, aoff4, cnt4 = _topk(aff4, T, tb)
    per = tb // LANES
    first = aoff4[:, ::per, :, 0]
    start = first & ~(BF16_SUBLANES - 1)
    aoff = start.reshape(-1)
    nch4 = (first - start + cnt4[:, ::per, :, 0] + (W - 1)) // W
    nch = nch4.reshape(-1)
    nmax = jnp.max(nch4, axis=-1).reshape(-1)
    post = jnp.transpose(pos4, (0, 1, 3, 2)).reshape(N, N_EXPERTS) + 1
    post = jnp.concatenate([post >> 5, post & 31], axis=1).astype(BF16)
    xg, gs = _compress(aoff, nch, nmax, xn, gsp, pos4, B, T, tb, rows, W)
    y = _ffn(xg, gs, wg, wu, wd, layer, live)
    return _expand(aoff, nch, nmax, h, post, y, B, T, tb, 2 * dw, W)


def _final_norm_kernel(h_ref, g_ref, o_ref):
    o_ref[...] = _rms(h_ref[...], g_ref[...])


def _final_norm(h, g, tm):
    N, D = h.shape
    return pl.pallas_call(
        _final_norm_kernel,
        grid=(N // tm,),
        in_specs=[pl.BlockSpec((tm, D), lambda i: (i, 0)), pl.BlockSpec((1, D), lambda i: (0, 0))],
        out_specs=pl.BlockSpec((tm, D), lambda i: (i, 0)),
        out_shape=jax.ShapeDtypeStruct((N, D), F32),
        compiler_params=_cp("parallel"),
        name="final_norm",
    )(h, g)


def kernel(x, mem, mix_norm_g, ffn_norm_g, mem_norm_g, final_norm_g, w_mem_kv, w_out, hy_w_in, hy_short_w, hy_filt_w1, hy_filt_b1, hy_filt_w2, hy_filt_b2, hy_filt_w3, hy_filt_freq, hy_skip, at_w_in, at_q_norm_g, at_k_norm_g, router_w, exp_w_gate, exp_w_up, exp_w_down):
    B, T, D = x.shape
    depth = mix_norm_g.shape[0]
    C = hy_skip.shape[-1]
    tm = min(512, T)
    cb = 16
    n1 = T // LANES
    h = x.reshape(B * T, D)
    row = lambda v: v.reshape(1, -1).astype(F32)
    lanes = lambda v: jnp.broadcast_to(v[..., None, None], v.shape + (1, LANES)).astype(F32)
    max_decay = math.log(HY_DECAY_TARGET) / HY_FAST_PCT
    min_decay = math.log(HY_DECAY_TARGET) / HY_SLOW_PCT
    absdelta = jnp.asarray(np.abs(np.linspace(min_decay, max_decay, C)).astype(np.float32)).reshape(C, 1)

    for i in range(depth):
        j = i // 2
        wo = w_out[i].astype(BF16)
        wkv = w_mem_kv[i].astype(BF16)
        mkt, mv = _memkv(mem, row(mem_norm_g), wkv[:, :D_MEM].T, wkv[:, D_MEM:])
        wr = router_w[i].astype(BF16)
        route_w = (row(ffn_norm_g[i]), jnp.zeros((D, LANES), BF16).at[:, :N_EXPERTS].set(wr), wr.T)
        if i % 2 == 0:
            w_in = hy_w_in[j].astype(BF16)
            ut, cq = _hy_inproj(h, row(mix_norm_g[i]), w_in[:, :3 * C].T, w_in[:, 3 * C:], B, T, tm)
            filt = dict(
                w1t=jnp.zeros((HY_FILT, 40), F32).at[:, :hy_filt_w1.shape[1]].set(hy_filt_w1[j].T),
                b1=hy_filt_b1[j].reshape(-1, 1), w2t=hy_filt_w2[j].T, b2=hy_filt_b2[j].reshape(-1, 1),
                fr=hy_filt_freq[j].reshape(-1, 1),
                w3t=hy_filt_w3[j].T.reshape(2, 2, C, HY_FILT), absdelta=absdelta,
            )
            kfft = _hyena_filters_fft(filt, T, 2 * cb)
            zt = _hy_conv(ut.reshape(B, 3 * C, n1, LANES), lanes(hy_short_w[j]), lanes(hy_skip[j]),
                          kfft, B, T, C, cb)
            routed = _outproj(h, zt, cq, mkt, mv, wo[:C], wo[C:], *route_w, B, T, min(1024, T))
        else:
            rep = lambda v, n: jnp.tile(v.astype(F32), n).reshape(1, -1)
            qt, kz, vtz, cq = _at_inproj(h, row(mix_norm_g[i]), at_w_in[j].astype(BF16),
                                         rep(at_q_norm_g[j], GQA_GROUP * N_KV_HEADS),
                                         rep(at_k_norm_g[j], N_KV_HEADS), B, T, tm)
            main_t = _flash(qt, kz, vtz, B, T, min(2048, T), min(1024, T))
            routed = _outproj(h, main_t, cq, mkt, mv, wo[:C], wo[C:], *route_w, B, T, tm)
        h = _moe(*routed, exp_w_gate, exp_w_up, exp_w_down, i, B, T)
    return _final_norm(h, row(final_norm_g), tm).reshape(B, T, D)
```

```python
import functools
import math

import numpy as np
import jax
import jax.numpy as jnp
from jax import lax
from jax.experimental import pallas as pl
from jax.experimental.pallas import tpu as pltpu

F32 = jnp.float32
BF16 = jnp.bfloat16
I32 = jnp.int32

HEAD_DIM = 64
MEM_HEADS = 4
D_MEM = MEM_HEADS * HEAD_DIM
N_KV_HEADS = 4
GQA_GROUP = 3
GRID_W = 64
ROPE_THETA = 10000.0
ROPE_AXIS_DIM = HEAD_DIM // 2
HY_BANDS = 16
HY_FILT = 64
HY_DECAY_TARGET = 1e-2
HY_FAST_PCT = 0.3
HY_SLOW_PCT = 1.5
N_EXPERTS = 16
EC_CAPACITY_FACTOR = 2
NORM_EPS = 1e-6

LANES = 128
BF16_SUBLANES = 16
VMEM_LIMIT = 56 * 1024 * 1024
HI = lax.Precision.HIGHEST
LOG2E = 1.4426950408889634


def _cp(*sem):
    return pltpu.CompilerParams(dimension_semantics=sem, vmem_limit_bytes=VMEM_LIMIT)


def _rms(x, g):
    ms = jnp.mean(x * x, axis=-1, keepdims=True)
    return x * lax.rsqrt(ms + NORM_EPS) * g


def _dot(a, b):
    return jnp.dot(a, b, preferred_element_type=F32)


def _dot_nt(a, b):
    return lax.dot_general(a, b, (((1,), (1,)), ((), ())), preferred_element_type=F32)


@functools.lru_cache(maxsize=None)
def _dft_tables(T, cb):
    nf = 2 * T
    n1k = nf // LANES
    n1 = T // LANES
    k1 = np.arange(n1k)[:, None]
    a1 = 2 * np.pi * k1 * np.arange(n1)[None, :] / n1k
    f1s = np.concatenate([np.cos(a1), -np.sin(a1)], axis=0)
    a1f = 2 * np.pi * k1 * np.arange(n1k)[None, :] / n1k
    f1full = np.concatenate([np.cos(a1f), -np.sin(a1f)], axis=0)
    tw = 2 * np.pi * k1 * np.arange(LANES)[None, :] / nf
    twr, twi = np.cos(tw), -np.sin(tw)
    a2 = 2 * np.pi * np.arange(LANES)[:, None] * np.arange(LANES)[None, :] / LANES
    cr, ci = np.cos(a2), -np.sin(a2)
    w2 = np.block([[cr, ci], [-ci, cr]])
    minv = np.block([[cr, -ci], [ci, cr]])
    a3 = 2 * np.pi * np.arange(n1)[:, None] * np.arange(n1k)[None, :] / n1k
    g1 = np.concatenate([np.cos(a3), -np.sin(a3)], axis=1) / nf
    return dict(
        f1s=f1s, f1full=f1full, w2=w2, minv=minv, g1=g1,
        twr_l=np.tile(twr, (1, cb)), twi_l=np.tile(twi, (1, cb)),
        twr_r=np.tile(twr, (cb, 1)), twi_r=np.tile(twi, (cb, 1)),
    )


@functools.lru_cache(maxsize=None)
def _filter_feats(T):
    t = np.linspace(0.0, 1.0, T)[None, :]
    w = (2.0 * np.pi) * np.arange(T)[None, :] / T
    bands = np.linspace(1e-4, HY_BANDS - 1, HY_BANDS)[:, None]
    feats = np.concatenate([t, np.cos(bands * w), -np.sin(bands * w)], axis=0)
    pad = np.zeros((40 - feats.shape[0], T))
    feats = np.concatenate([feats, pad], axis=0)
    rev = (T - np.arange(T)) % T
    return (np.stack([feats, feats[:, rev]]).astype(np.float32),
            np.stack([t, t[:, rev]]).astype(np.float32))


@functools.lru_cache(maxsize=None)
def _rope_tables(T):
    rows = T // GRID_W
    pos_row = np.repeat(np.arange(rows), GRID_W).astype(np.float64)
    pos_col = np.tile(np.arange(GRID_W), rows).astype(np.float64)
    inv = 1.0 / (ROPE_THETA ** (np.arange(0, ROPE_AXIS_DIM, 2, dtype=np.float64) / ROPE_AXIS_DIM))
    lane = np.arange(LANES)
    d = lane % HEAD_DIM
    axis = d // ROPE_AXIS_DIM
    half = (d // (ROPE_AXIS_DIM // 2)) % 2
    f = d % (ROPE_AXIS_DIM // 2)
    pos = np.where(axis[None, :] == 0, pos_row[:, None], pos_col[:, None])
    ang = pos * inv[f][None, :]
    cos2 = np.cos(ang)
    sin2 = np.where(half[None, :] == 0, -np.sin(ang), np.sin(ang))
    return cos2.astype(np.float32), sin2.astype(np.float32)


@functools.lru_cache(maxsize=None)
def _spread_table(W):
    owner = np.arange(N_EXPERTS * W) // W
    hit = (np.arange(N_EXPERTS)[:, None] == owner[None, :]).astype(np.float32)
    return np.concatenate([32.0 * hit, hit], axis=0)


@functools.lru_cache(maxsize=None)
def _topk_tables(T, tb):
    nj = T // LANES
    per = tb // LANES
    r = np.arange(nj * N_EXPERTS)
    j, e = r // N_EXPERTS, r % N_EXPERTS
    same_e = e[:, None] == e[None, :]
    blk = j // per
    same_blk = blk[:, None] == blk[None, :]
    m_all = same_e & (j[None, :] < j[:, None])
    m_w = same_e & same_blk & (j[None, :] < j[:, None])
    m_b = same_e & same_blk
    tri = np.triu(np.ones((LANES, LANES)))
    return tuple(np.asarray(m, np.float32) for m in (m_all, m_w, m_b, tri))


def _hy_inproj_kernel(h_ref, g_ref, wmt_ref, wcq_ref, ut_ref, cq_ref):
    xn = _rms(h_ref[...], g_ref[...]).astype(BF16)
    ut_ref[0] = _dot_nt(wmt_ref[...], xn)
    cq_ref[...] = _dot(xn, wcq_ref[...]).astype(BF16)


def _hy_inproj(h, g, wmt, wcq, B, T, tm):
    N, D = h.shape
    c3 = wmt.shape[0]
    nt = T // tm
    return pl.pallas_call(
        _hy_inproj_kernel,
        grid=(B, nt),
        in_specs=[
            pl.BlockSpec((tm, D), lambda b, i: (b * nt + i, 0)),
            pl.BlockSpec((1, D), lambda b, i: (0, 0)),
            pl.BlockSpec((c3, D), lambda b, i: (0, 0)),
            pl.BlockSpec((D, D_MEM), lambda b, i: (0, 0)),
        ],
        out_specs=[
            pl.BlockSpec((1, c3, tm), lambda b, i: (b, 0, i)),
            pl.BlockSpec((tm, D_MEM), lambda b, i: (b * nt + i, 0)),
        ],
        out_shape=[
            jax.ShapeDtypeStruct((B, c3, T), F32),
            jax.ShapeDtypeStruct((N, D_MEM), BF16),
        ],
        compiler_params=_cp("parallel", "parallel"),
        name="hy_inproj",
    )(h, g, wmt, wcq)


def _filt_mlp_kernel(feats_ref, w1t_ref, b1_ref, w2t_ref, b2_ref, fr_ref, h2_ref):
    fr = fr_ref[...]
    for d in range(2):
        a = jnp.dot(w1t_ref[...], feats_ref[d], precision=HI, preferred_element_type=F32)
        h1 = jnp.sin(fr * (a + b1_ref[...]))
        a = jnp.dot(w2t_ref[...], h1, precision=HI, preferred_element_type=F32)
        h2_ref[d] = jnp.sin(fr * (a + b2_ref[...]))


def _filt_taps_kernel(h2_ref, w3t_ref, t_ref, dl_ref, out_ref):
    T = h2_ref.shape[2]
    dl = dl_ref[...]
    hf = jnp.dot(w3t_ref[0, 0], h2_ref[0], precision=HI, preferred_element_type=F32) * jnp.exp(-t_ref[0] * dl)
    hb = jnp.dot(w3t_ref[0, 1], h2_ref[1], precision=HI, preferred_element_type=F32) * jnp.exp(-t_ref[1] * dl)
    nrm = jnp.sum(jnp.abs(hf) + jnp.abs(hb), axis=-1, keepdims=True) + 1e-6
    inv = 1.0 / nrm
    tap0 = lax.broadcasted_iota(I32, hf.shape, 1) == 0
    out_ref[0, :, :T] = ((hf + jnp.where(tap0, hb, 0.0)) * inv).astype(BF16)
    out_ref[0, :, T:] = (jnp.where(tap0, 0.0, hb) * inv).astype(BF16)


def _fwd_fft(x3, f1s, twr, twi, w2, cb, n1k):
    rhs = jnp.concatenate([x3[c].astype(BF16) for c in range(cb)], axis=1)
    a = _dot(f1s, rhs)
    ar, ai = a[:n1k], a[n1k:]
    tr = (ar * twr - ai * twi).astype(BF16)
    ti = (ar * twi + ai * twr).astype(BF16)
    lr = jnp.concatenate([tr[:, c * LANES:(c + 1) * LANES] for c in range(cb)], axis=0)
    li = jnp.concatenate([ti[:, c * LANES:(c + 1) * LANES] for c in range(cb)], axis=0)
    return _dot(jnp.concatenate([lr, li], axis=1), w2)


def _inv_fft(y, minv, twr, twi, g1, cb, n1k):
    b = _dot(y.astype(BF16), minv)
    br, bi = b[:, :LANES], b[:, LANES:]
    pr = (br * twr + bi * twi).astype(BF16)
    pi = (bi * twr - br * twi).astype(BF16)
    top = jnp.concatenate([pr[c * n1k:(c + 1) * n1k] for c in range(cb)], axis=1)
    bot = jnp.concatenate([pi[c * n1k:(c + 1) * n1k] for c in range(cb)], axis=1)
    rhs = jnp.concatenate([top, bot], axis=0)
    return _dot(g1, rhs)


def _filt_fft_kernel(taps_ref, f1_ref, twr_ref, twi_ref, w2_ref, k_ref, *, cb, n1k):
    k_ref[0] = _fwd_fft(taps_ref[0], f1_ref[...], twr_ref[...], twi_ref[...], w2_ref[...], cb, n1k).astype(BF16)


def _hyena_filters_fft(p, T, cb):
    feats, t_rows = _filter_feats(T)
    C = p["w3t"].shape[2]
    tabs = _dft_tables(T, cb)
    n1k = 2 * T // LANES
    h2 = pl.pallas_call(
        _filt_mlp_kernel,
        out_shape=jax.ShapeDtypeStruct((2, HY_FILT, T), F32),
        compiler_params=pltpu.CompilerParams(vmem_limit_bytes=VMEM_LIMIT),
        name="hy_filt_mlp",
    )(jnp.asarray(feats), p["w1t"], p["b1"], p["w2t"], p["b2"], p["fr"])
    cbt = 64
    taps = pl.pallas_call(
        _filt_taps_kernel,
        grid=(2, C // cbt),
        in_specs=[
            pl.BlockSpec((2, HY_FILT, T), lambda o, c: (0, 0, 0)),
            pl.BlockSpec((1, 2, cbt, HY_FILT), lambda o, c: (o, 0, c, 0)),
            pl.BlockSpec((2, 1, T), lambda o, c: (0, 0, 0)),
            pl.BlockSpec((cbt, 1), lambda o, c: (c, 0)),
        ],
        out_specs=pl.BlockSpec((1, cbt, 2 * T), lambda o, c: (o, c, 0)),
        out_shape=jax.ShapeDtypeStruct((2, C, 2 * T), BF16),
        compiler_params=_cp("parallel", "parallel"),
        name="hy_filt_taps",
    )(h2, p["w3t"], jnp.asarray(t_rows), p["absdelta"])
    taps = taps.reshape(2, C, n1k, LANES)
    const = lambda shape: pl.BlockSpec(shape, lambda o, c: (0,) * len(shape))
    f32 = lambda k: jnp.asarray(tabs[k], F32)
    return pl.pallas_call(
        functools.partial(_filt_fft_kernel, cb=cb, n1k=n1k),
        grid=(2, C // cb),
        in_specs=[
            pl.BlockSpec((1, cb, n1k, LANES), lambda o, c: (o, c, 0, 0)),
            const((2 * n1k, n1k)), const((n1k, cb * LANES)), const((n1k, cb * LANES)),
            const((2 * LANES, 2 * LANES)),
        ],
        out_specs=pl.BlockSpec((1, cb * n1k, 2 * LANES), lambda o, c: (o, c, 0)),
        out_shape=jax.ShapeDtypeStruct((2, C * n1k, 2 * LANES), BF16),
        compiler_params=_cp("parallel", "parallel"),
        name="hy_filt_fft",
    )(taps, f32("f1full").astype(BF16), f32("twr_l"), f32("twi_l"), f32("w2").astype(BF16))


def _time_neighbours(x):
    rows = x.shape[0]
    lane = lax.broadcasted_iota(I32, x.shape, 1)
    r = pltpu.roll(x, 1, 1)
    rr = pltpu.roll(r, 1, 0)
    prev = jnp.where(lane == 0, rr, r)
    r2 = pltpu.roll(x, LANES - 1, 1)
    rr2 = pltpu.roll(r2, rows - 1, 0)
    nxt = jnp.where(lane == LANES - 1, rr2, r2)
    return prev, nxt


def _hy_conv_kernel(x1_ref, x2_ref, v_ref, sw1_ref, sw2_ref, swv_ref, skip_ref, k_ref,
                    f1s_ref, twrl_ref, twil_ref, w2_ref, minv_ref, twrr_ref, twir_ref,
                    g1_ref, o_ref, *, cb, n1k, n1):
    rows = cb * n1
    shape2 = (rows, LANES)
    row = lax.broadcasted_iota(I32, shape2, 0)
    lane = lax.broadcasted_iota(I32, shape2, 1)
    first = (lane == 0) & (row % n1 == 0)
    last = (lane == LANES - 1) & (row % n1 == n1 - 1)

    def sconv(x_ref, sw_ref):
        x = x_ref[0].reshape(shape2)
        prev, nxt = _time_neighbours(x)
        prev = jnp.where(first, 0.0, prev)
        nxt = jnp.where(last, 0.0, nxt)
        w = [jnp.broadcast_to(sw_ref[j], (cb, n1, LANES)).reshape(shape2) for j in range(3)]
        return prev * w[0] + x * w[1] + nxt * w[2]

    z = sconv(v_ref, swv_ref)
    gates = (sconv(x1_ref, sw1_ref), sconv(x2_ref, sw2_ref))
    f1s, w2 = f1s_ref[...], w2_ref[...]
    for o in range(2):
        zf = _fwd_fft(z.reshape(cb, n1, LANES), f1s, twrl_ref[...], twil_ref[...], w2, cb, n1k)
        kk = k_ref[o].astype(F32)
        zr, zi = zf[:, :LANES], zf[:, LANES:]
        kr, ki = kk[:, :LANES], kk[:, LANES:]
        y = jnp.concatenate([zr * kr - zi * ki, zr * ki + zi * kr], axis=1)
        conv = _inv_fft(y, minv_ref[...], twrr_ref[...], twir_ref[...], g1_ref[...], cb, n1k)
        conv = jnp.concatenate([conv[:, c * LANES:(c + 1) * LANES] for c in range(cb)], axis=0)
        skip = jnp.broadcast_to(skip_ref[o], (cb, n1, LANES)).reshape(shape2)
        z = gates[o] * (conv + skip * z)
    o_ref[0] = z.reshape(cb, n1, LANES)


def _hy_conv(ut4, sw, skip, kfft, B, T, C, cb):
    n1k, n1 = 2 * T // LANES, T // LANES
    tabs = _dft_tables(T, cb)
    nct = C // cb
    bf = lambda k: jnp.asarray(tabs[k], F32).astype(BF16)
    f32 = lambda k: jnp.asarray(tabs[k], F32)
    const = lambda shape: pl.BlockSpec(shape, lambda c, b: (0,) * len(shape))
    ublk = lambda s: pl.BlockSpec((1, cb, n1, LANES), lambda c, b, s=s: (b, s * nct + c, 0, 0))
    wblk = lambda s: pl.BlockSpec((3, cb, 1, LANES), lambda c, b, s=s: (0, s * nct + c, 0, 0))
    return pl.pallas_call(
        functools.partial(_hy_conv_kernel, cb=cb, n1k=n1k, n1=n1),
        grid=(nct, B),
        in_specs=[
            ublk(0), ublk(1), ublk(2), wblk(0), wblk(1), wblk(2),
            pl.BlockSpec((2, cb, 1, LANES), lambda c, b: (0, c, 0, 0)),
            pl.BlockSpec((2, cb * n1k, 2 * LANES), lambda c, b: (0, c, 0)),
            const((2 * n1k, n1)), const((n1k, cb * LANES)), const((n1k, cb * LANES)),
            const((2 * LANES, 2 * LANES)), const((2 * LANES, 2 * LANES)),
            const((cb * n1k, LANES)), const((cb * n1k, LANES)), const((n1, 2 * n1k)),
        ],
        out_specs=pl.BlockSpec((1, cb, n1, LANES), lambda c, b: (b, c, 0, 0)),
        out_shape=jax.ShapeDtypeStruct((B, C, n1, LANES), F32),
        compiler_params=_cp("parallel", "arbitrary"),
        name="hy_conv",
    )(ut4, ut4, ut4, sw, sw, sw, skip, kfft,
      bf("f1s"), f32("twr_l"), f32("twi_l"), bf("w2"), bf("minv"),
      f32("twr_r"), f32("twi_r"), bf("g1"))


def _head_norm(x, bd, g):
    ss = _dot((x * x).astype(BF16), bd)
    return x * lax.rsqrt(ss * (1.0 / HEAD_DIM) + NORM_EPS) * g


def _rope(x, cos2, sin2):
    lane = lax.broadcasted_iota(I32, cos2.shape, 1)
    low = (lane // (ROPE_AXIS_DIM // 2)) % 2 == 0
    out = []
    for c in range(x.shape[1] // LANES):
        xc = x[:, c * LANES:(c + 1) * LANES]
        up = pltpu.roll(xc, LANES - 16, 1)
        dn = pltpu.roll(xc, 16, 1)
        out.append(xc * cos2 + jnp.where(low, up, dn) * sin2)
    return jnp.concatenate(out, axis=1)


def _at_inproj_kernel(h_ref, g_ref, w_ref, bdq_ref, bdk_ref, gq_ref, gk_ref, cos_ref, sin_ref,
                      qt_ref, kz_ref, vt_ref, cq_ref, *, dq, dk):
    xn = _rms(h_ref[...], g_ref[...]).astype(BF16)
    proj = _dot(xn, w_ref[...])
    q, k = proj[:, :dq], proj[:, dq:dq + dk]
    v, cq = proj[:, dq + dk:dq + 2 * dk], proj[:, dq + 2 * dk:]
    cos2, sin2 = cos_ref[...], sin_ref[...]
    qr = _rope(_head_norm(q, bdq_ref[...], gq_ref[...]), cos2, sin2)
    qt = (qr * (HEAD_DIM ** -0.5 * LOG2E)).T.astype(BF16)
    for p in range(dq // LANES):
        qt_ref[0, p] = qt[p * LANES:(p + 1) * LANES]
    kr = _rope(_head_norm(k, bdk_ref[...], gk_ref[...]), cos2, sin2)
    vt = v.T.astype(BF16)
    zv = jnp.zeros((HEAD_DIM, vt.shape[1]), BF16)
    lane = lax.broadcasted_iota(I32, (k.shape[0], LANES), 1)
    for kv in range(N_KV_HEADS):
        rows = vt[kv * HEAD_DIM:(kv + 1) * HEAD_DIM]
        vt_ref[0, kv, 0] = jnp.concatenate([rows, zv], axis=0)
        vt_ref[0, kv, 1] = jnp.concatenate([zv, rows], axis=0)
        pair = kr[:, (kv // 2) * LANES:(kv // 2 + 1) * LANES]
        own = jnp.where((lane < HEAD_DIM) == (kv % 2 == 0), pair, 0.0)
        other = pltpu.roll(own, HEAD_DIM, 1)
        lo, hi = (own, other) if kv % 2 == 0 else (other, own)
        kz_ref[0, kv, 0] = lo.astype(BF16)
        kz_ref[0, kv, 1] = hi.astype(BF16)
    cq_ref[...] = cq.astype(BF16)


def _at_inproj(h, g, w, gq, gk, B, T, tm):
    N, D = h.shape
    dq, dk = GQA_GROUP * N_KV_HEADS * HEAD_DIM, N_KV_HEADS * HEAD_DIM
    nt = T // tm
    cos2, sin2 = _rope_tables(T)
    bd = lambda n: jnp.asarray(np.kron(np.eye(n // HEAD_DIM), np.ones((HEAD_DIM, HEAD_DIM))), F32).astype(BF16)
    const = lambda shape: pl.BlockSpec(shape, lambda b, i: (0,) * len(shape))
    return pl.pallas_call(
        functools.partial(_at_inproj_kernel, dq=dq, dk=dk),
        grid=(B, nt),
        in_specs=[
            pl.BlockSpec((tm, D), lambda b, i: (b * nt + i, 0)),
            const((1, D)), const(w.shape), const((dq, dq)), const((dk, dk)),
            const((1, dq)), const((1, dk)),
            pl.BlockSpec((tm, LANES), lambda b, i: (i, 0)),
            pl.BlockSpec((tm, LANES), lambda b, i: (i, 0)),
        ],
        out_specs=[
            pl.BlockSpec((1, dq // LANES, LANES, tm), lambda b, i: (b, 0, 0, i)),
            pl.BlockSpec((1, N_KV_HEADS, 2, tm, LANES), lambda b, i: (b, 0, 0, i, 0)),
            pl.BlockSpec((1, N_KV_HEADS, 2, LANES, tm), lambda b, i: (b, 0, 0, 0, i)),
            pl.BlockSpec((tm, D_MEM), lambda b, i: (b * nt + i, 0)),
        ],
        out_shape=[
            jax.ShapeDtypeStruct((B, dq // LANES, LANES, T), BF16),
            jax.ShapeDtypeStruct((B, N_KV_HEADS, 2, T, LANES), BF16),
            jax.ShapeDtypeStruct((B, N_KV_HEADS, 2, LANES, T), BF16),
            jax.ShapeDtypeStruct((N, D_MEM), BF16),
        ],
        compiler_params=_cp("parallel", "parallel"),
        name="at_inproj",
    )(h, g, w, bd(dq), bd(dk), gq, gk, jnp.asarray(cos2), jnp.asarray(sin2))


def _flash_kernel(qt_ref, ka_ref, kb_ref, vta_ref, vtb_ref, o_ref, m_sc, l_sc, acc_sc, s_sc, p_sc):
    ki = pl.program_id(3)

    @pl.when(ki == 0)
    def _():
        m_sc[...] = jnp.full(m_sc.shape, -jnp.inf, F32)
        l_sc[...] = jnp.zeros(l_sc.shape, F32)
        acc_sc[...] = jnp.zeros(acc_sc.shape, F32)

    qt = qt_ref[0, 0]
    tk, tq = s_sc.shape[1], s_sc.shape[2]
    sub = 8
    ck = 2 * sub
    pv, alphas = [], []
    for idx, k_ref in enumerate((ka_ref, kb_ref)):
        s_sc[idx] = _dot(k_ref[0, 0, 0], qt)
    for idx, vt_ref in enumerate((vta_ref, vtb_ref)):
        mx = s_sc[idx, 0:sub, :]
        for c in range(1, tk // sub):
            mx = jnp.maximum(mx, s_sc[idx, c * sub:(c + 1) * sub, :])
        m_prev = m_sc[idx]
        m_new = jnp.maximum(m_prev, jnp.max(mx, axis=0, keepdims=True))
        alpha = jnp.exp2(m_prev - m_new)
        lsum = jnp.zeros((sub, tq), F32)
        for c in range(tk // ck):
            p = jnp.exp2(s_sc[idx, c * ck:(c + 1) * ck, :] - m_new)
            lsum = lsum + p[:sub] + p[sub:]
            p_sc[idx, c * ck:(c + 1) * ck, :] = p.astype(BF16)
        l_sc[idx] = alpha * l_sc[idx] + jnp.sum(lsum, axis=0, keepdims=True)
        m_sc[idx] = m_new
        pv.append(_dot(vt_ref[0, 0, 0], p_sc[idx]))
        alphas.append(alpha)
    row = lax.broadcasted_iota(I32, acc_sc.shape, 0)
    low = row < HEAD_DIM
    acc_sc[...] = acc_sc[...] * jnp.where(low, alphas[0], alphas[1]) + pv[0] + pv[1]

    @pl.when(ki == pl.num_programs(3) - 1)
    def _():
        o_ref[0] = acc_sc[...] / jnp.where(low, l_sc[0], l_sc[1])


def _flash(qt, kz, vtz, B, T, tq, tk):
    npair = qt.shape[1]
    nq, nk = T // tq, T // tk
    kv_of = lambda p, j: (2 * p + j) // GQA_GROUP
    return pl.pallas_call(
        _flash_kernel,
        grid=(B, npair, nq, nk),
        in_specs=[
            pl.BlockSpec((1, 1, LANES, tq), lambda b, p, i, k: (b, p, 0, i)),
            pl.BlockSpec((1, 1, 1, tk, LANES), lambda b, p, i, k: (b, kv_of(p, 0), 0, k, 0)),
            pl.BlockSpec((1, 1, 1, tk, LANES), lambda b, p, i, k: (b, kv_of(p, 1), 1, k, 0)),
            pl.BlockSpec((1, 1, 1, LANES, tk), lambda b, p, i, k: (b, kv_of(p, 0), 0, 0, k)),
            pl.BlockSpec((1, 1, 1, LANES, tk), lambda b, p, i, k: (b, kv_of(p, 1), 1, 0, k)),
        ],
        out_specs=pl.BlockSpec((1, LANES, tq), lambda b, p, i, k: (b, p, i)),
        out_shape=jax.ShapeDtypeStruct((B, npair * LANES, T), F32),
        scratch_shapes=[
            pltpu.VMEM((2, 1, tq), F32), pltpu.VMEM((2, 1, tq), F32), pltpu.VMEM((LANES, tq), F32),
            pltpu.VMEM((2, tk, tq), F32), pltpu.VMEM((2, tk, tq), BF16),
        ],
        compiler_params=_cp("parallel", "parallel", "parallel", "arbitrary"),
        name="flash_gqa",
    )(qt, kz, kz, vtz, vtz)


def _memkv_kernel(mem_ref, g_ref, wkvt_ref, wv_ref, mkt_ref, mv_ref):
    mn = _rms(mem_ref[0], g_ref[...]).astype(BF16)
    kt = _dot_nt(wkvt_ref[...], mn) * (HEAD_DIM ** -0.5)
    v = _dot(mn, wv_ref[...])
    row = lax.broadcasted_iota(I32, kt.shape, 0)
    col = lax.broadcasted_iota(I32, v.shape, 1)
    for hd in range(MEM_HEADS):
        mkt_ref[0, hd] = jnp.where(row // HEAD_DIM == hd, kt, 0.0).astype(BF16)
        mv_ref[0, hd] = jnp.where(col // HEAD_DIM == hd, v, 0.0).astype(BF16)


def _memkv(mem, g, wkt, wv):
    B, M, D = mem.shape
    return pl.pallas_call(
        _memkv_kernel,
        grid=(B,),
        in_specs=[
            pl.BlockSpec((1, M, D), lambda b: (b, 0, 0)),
            pl.BlockSpec((1, D), lambda b: (0, 0)),
            pl.BlockSpec((D_MEM, D), lambda b: (0, 0)),
            pl.BlockSpec((D, D_MEM), lambda b: (0, 0)),
        ],
        out_specs=[
            pl.BlockSpec((1, MEM_HEADS, D_MEM, M), lambda b: (b, 0, 0, 0)),
            pl.BlockSpec((1, MEM_HEADS, M, D_MEM), lambda b: (b, 0, 0, 0)),
        ],
        out_shape=[
            jax.ShapeDtypeStruct((B, MEM_HEADS, D_MEM, M), BF16),
            jax.ShapeDtypeStruct((B, MEM_HEADS, M, D_MEM), BF16),
        ],
        compiler_params=_cp("parallel"),
        name="mem_kv",
    )(mem, g, wkt, wv)


def _cross_attn(cq, mkt_ref, mv_ref):
    acc = jnp.zeros(cq.shape, F32)
    for hd in range(MEM_HEADS):
        s = _dot(cq, mkt_ref[0, hd])
        p = jnp.exp(s - jnp.max(s, axis=-1, keepdims=True))
        p = p / jnp.sum(p, axis=-1, keepdims=True)
        acc = acc + _dot(p.astype(BF16), mv_ref[0, hd])
    return acc


def _route(h, g_ref, wr_ref, wrt_ref, xn_ref, gsp_ref, afft_ref):
    xn = _rms(h, g_ref[...]).astype(BF16)
    xn_ref[...] = xn
    lg = _dot(xn, wr_ref[...])
    lane = lax.broadcasted_iota(I32, lg.shape, 1)
    lg = jnp.where(lane < N_EXPERTS, lg, -jnp.inf)
    p = jnp.exp(lg - jnp.max(lg, axis=-1, keepdims=True))
    aff = p / jnp.sum(p, axis=-1, keepdims=True)
    hi = aff.astype(BF16).astype(F32)
    mid = (aff - hi).astype(BF16).astype(F32)
    lo = (aff - hi - mid).astype(BF16).astype(F32)
    gsp_ref[...] = (hi + pltpu.roll(mid, N_EXPERTS, 1) + pltpu.roll(lo, 2 * N_EXPERTS, 1)).astype(BF16)
    lt = _dot_nt(wrt_ref[...], xn)
    pt = jnp.exp(lt - jnp.max(lt, axis=0, keepdims=True))
    pt = pt / jnp.sum(pt, axis=0, keepdims=True)
    for j in range(lt.shape[1] // LANES):
        afft_ref[0, j] = pt[:, j * LANES:(j + 1) * LANES]


def _outproj_kernel(h_ref, main_ref, cq_ref, mkt_ref, mv_ref, wm_ref, wc_ref, g_ref, wr_ref, wrt_ref,
                    o_ref, xn_ref, gsp_ref, afft_ref):
    if len(main_ref.shape) == 3:
        main = main_ref[0].T
    else:
        main = jnp.concatenate([main_ref[0, :, j, :].T for j in range(main_ref.shape[2])], axis=0)
    main = main.astype(BF16)
    cross = _cross_attn(cq_ref[...], mkt_ref, mv_ref).astype(BF16)
    h_new = h_ref[...] + _dot(main, wm_ref[...]) + _dot(cross, wc_ref[...])
    o_ref[...] = h_new
    _route(h_new, g_ref, wr_ref, wrt_ref, xn_ref, gsp_ref, afft_ref)


def _outproj(h, main, cq, mkt, mv, wm, wc, g, wr, wrt, B, T, tm):
    N, D = h.shape
    C = wm.shape[0]
    M = mkt.shape[-1]
    nt = T // tm
    if main.ndim == 3:
        mspec = pl.BlockSpec((1, C, tm), lambda b, i: (b, 0, i))
    else:
        mspec = pl.BlockSpec((1, C, tm // LANES, LANES), lambda b, i: (b, 0, i, 0))
    return pl.pallas_call(
        _outproj_kernel,
        grid=(B, nt),
        in_specs=[
            pl.BlockSpec((tm, D), lambda b, i: (b * nt + i, 0)),
            mspec,
            pl.BlockSpec((tm, D_MEM), lambda b, i: (b * nt + i, 0)),
            pl.BlockSpec((1, MEM_HEADS, D_MEM, M), lambda b, i: (b, 0, 0, 0)),
            pl.BlockSpec((1, MEM_HEADS, M, D_MEM), lambda b, i: (b, 0, 0, 0)),
            pl.BlockSpec((C, D), lambda b, i: (0, 0)),
            pl.BlockSpec((D_MEM, D), lambda b, i: (0, 0)),
            pl.BlockSpec((1, D), lambda b, i: (0, 0)),
            pl.BlockSpec((D, LANES), lambda b, i: (0, 0)),
            pl.BlockSpec((N_EXPERTS, D), lambda b, i: (0, 0)),
        ],
        out_specs=[
            pl.BlockSpec((tm, D), lambda b, i: (b * nt + i, 0)),
            pl.BlockSpec((tm, D), lambda b, i: (b * nt + i, 0)),
            pl.BlockSpec((tm, LANES), lambda b, i: (b * nt + i, 0)),
            pl.BlockSpec((1, tm // LANES, N_EXPERTS, LANES), lambda b, i: (b, i, 0, 0)),
        ],
        out_shape=[
            jax.ShapeDtypeStruct((N, D), F32),
            jax.ShapeDtypeStruct((N, D), BF16),
            jax.ShapeDtypeStruct((N, LANES), BF16),
            jax.ShapeDtypeStruct((B, T // LANES, N_EXPERTS, LANES), F32),
        ],
        compiler_params=_cp("parallel", "parallel"),
        name="outproj",
    )(h, main, cq, mkt, mv, wm, wc, g, wr, wrt)


def _topk_kernel(aff_ref, mall_ref, mw_ref, mb_ref, tri_ref, pos_ref, aoff_ref, cnt_ref, *, cap, nj):
    E = N_EXPERTS
    aff3 = aff_ref[0]

    def count(mask3):
        per = jnp.sum(mask3.astype(F32), axis=0)
        return jnp.broadcast_to(jnp.sum(per, axis=-1, keepdims=True), (E, LANES))

    def step(i, thr):
        cand = thr | lax.shift_left(jnp.int32(1), 30 - i)
        ok = count(aff3 >= pltpu.bitcast(cand, F32)[None]) >= cap
        return jnp.where(ok, cand, thr)

    thr = lax.fori_loop(0, 31, step, jnp.zeros((E, LANES), I32))
    thr = pltpu.bitcast(thr, F32)
    gt3 = aff3 > thr[None]
    eq3 = aff3 == thr[None]
    need = cap - count(gt3)

    ones = jnp.ones((LANES, LANES), BF16)
    tri = tri_ref[...]

    def prefix(mask2):
        mb = mask2.astype(BF16)
        incl = _dot(mb, tri)
        tot = _dot(mb, ones)
        return incl - mask2, tot

    eq2 = eq3.reshape(nj * E, LANES).astype(F32)
    ex, tot = prefix(eq2)
    eq_rank = ex + _dot(mall_ref[...], tot.astype(BF16))
    need2 = jnp.broadcast_to(need[None], (nj, E, LANES)).reshape(nj * E, LANES)
    sel = jnp.where((gt3.reshape(nj * E, LANES)) | ((eq2 > 0) & (eq_rank < need2)), 1.0, 0.0)

    ex, tot = prefix(sel)
    totb = tot.astype(BF16)
    before = _dot(mall_ref[...], totb)
    within = _dot(mw_ref[...], totb)
    cnt = _dot(mb_ref[...], totb)
    pos_ref[0] = jnp.where(sel > 0, (before + ex).astype(I32), -1).reshape(nj, E, LANES)
    aoff_ref[0] = (before - within).astype(I32).reshape(nj, E, LANES)
    cnt_ref[0] = cnt.astype(I32).reshape(nj, E, LANES)


def _topk(aff4, T, tb):
    B, nj = aff4.shape[0], aff4.shape[1]
    cap = EC_CAPACITY_FACTOR * T // N_EXPERTS
    tabs = [jnp.asarray(m, F32).astype(BF16) for m in _topk_tables(T, tb)]
    R = nj * N_EXPERTS
    blk = pl.BlockSpec((1, nj, N_EXPERTS, LANES), lambda b: (b, 0, 0, 0))
    const = lambda shape: pl.BlockSpec(shape, lambda b: (0,) * len(shape))
    out = jax.ShapeDtypeStruct((B, nj, N_EXPERTS, LANES), I32)
    return pl.pallas_call(
        functools.partial(_topk_kernel, cap=cap, nj=nj),
        grid=(B,),
        in_specs=[blk, const((R, R)), const((R, R)), const((R, R)), const((LANES, LANES))],
        out_specs=[blk, blk, blk],
        out_shape=[out, out, out],
        compiler_params=_cp("parallel"),
        name="moe_topk",
    )(aff4, *tabs)


def _compress_kernel(aoff_ref, nch_ref, nmax_ref, xn_ref, gsp_ref, pos_ref, xg_ref, gs_ref,
                     *, nblk, tb, W, eg, nsub):
    b, grp, blk = pl.program_id(0), pl.program_id(1), pl.program_id(2)

    @pl.when(blk == 0)
    def _():
        xg_ref[...] = jnp.zeros(xg_ref.shape, BF16)
        gs_ref[...] = jnp.zeros(gs_ref.shape, F32)

    iota_s = lax.broadcasted_iota(I32, (W, LANES), 0)
    per = tb // LANES
    for sb in range(nsub):
        tblk = blk * nsub + sb
        base = (b * nblk + tblk) * N_EXPERTS + grp * eg
        tok = slice(sb * tb, (sb + 1) * tb)

        def chunk(c, carry, base=base, tok=tok, sb=sb):
            pieces = []
            for i in range(eg):
                a = aoff_ref[base + i] + c * W
                e = grp * eg + i
                g = [pos_ref[0, sb * per + jj, pl.ds(e, 1), :] - a == iota_s for jj in range(per)]
                pieces.append(jnp.where(jnp.concatenate(g, axis=1), 1.0, 0.0).astype(BF16))
            lhs = jnp.concatenate(pieces, axis=0)
            res = _dot(lhs, xn_ref[tok, :])
            resg = _dot(lhs, gsp_ref[tok, :])
            for i in range(eg):
                @pl.when(c < nch_ref[base + i])
                def _():
                    win = pl.ds(pl.multiple_of(aoff_ref[base + i] + c * W, BF16_SUBLANES), W)
                    xg_ref[0, i, win, :] = (xg_ref[0, i, win, :] + res[i * W:(i + 1) * W]).astype(BF16)
                    gs_ref[0, i, win, :] = gs_ref[0, i, win, :] + resg[i * W:(i + 1) * W]
            return carry

        lax.fori_loop(0, nmax_ref[b * nblk + tblk], chunk, 0)


def _compress(aoff, nch, nmax, xn, gsp, pos4, B, T, tb, rows, W):
    N, D = xn.shape
    nblk = T // tb
    eg = 4
    nsub = 2 if nblk % 2 == 0 else 1
    ns, ts = nblk // nsub, nsub * tb
    return pl.pallas_call(
        functools.partial(_compress_kernel, nblk=nblk, tb=tb, W=W, eg=eg, nsub=nsub),
        grid_spec=pltpu.PrefetchScalarGridSpec(
            num_scalar_prefetch=3,
            grid=(B, N_EXPERTS // eg, ns),
            in_specs=[
                pl.BlockSpec((ts, D), lambda b, g, k, *_: (b * ns + k, 0)),
                pl.BlockSpec((ts, LANES), lambda b, g, k, *_: (b * ns + k, 0)),
                pl.BlockSpec((1, ts // LANES, N_EXPERTS, LANES), lambda b, g, k, *_: (b, k, 0, 0)),
            ],
            out_specs=[
                pl.BlockSpec((1, eg, rows, D), lambda b, g, k, *_: (b, g, 0, 0)),
                pl.BlockSpec((1, eg, rows, LANES), lambda b, g, k, *_: (b, g, 0, 0)),
            ],
        ),
        out_shape=[
            jax.ShapeDtypeStruct((B, N_EXPERTS, rows, D), BF16),
            jax.ShapeDtypeStruct((B, N_EXPERTS, rows, LANES), F32),
        ],
        compiler_params=_cp("parallel", "parallel", "arbitrary"),
        name="moe_compress",
    )(aoff, nch, nmax, xn, gsp, pos4)


def _ffn_kernel(x_ref, gs_ref, wg_ref, wu_ref, wd_ref, y_ref, *, live):
    e = pl.program_id(0)
    x = x_ref[0, 0, :live]
    g = _dot(x, wg_ref[0, 0].astype(BF16))
    u = _dot(x, wu_ref[0, 0].astype(BF16))
    hid = (g * (1.0 / (1.0 + jnp.exp(-g))) * u).astype(BF16)
    gs = gs_ref[0, 0, :live]
    lane = lax.broadcasted_iota(I32, gs.shape, 1)
    mine = (lane == e) | (lane == e + N_EXPERTS) | (lane == e + 2 * N_EXPERTS)
    gate = jnp.sum(jnp.where(mine, gs, 0.0), axis=-1, keepdims=True)
    y_ref[0, 0, :live] = (_dot(hid, wd_ref[0, 0].astype(BF16)) * gate).astype(BF16)
    y_ref[0, 0, live:] = jnp.zeros((y_ref.shape[2] - live, y_ref.shape[3]), BF16)


def _ffn(xg, gs, wg, wu, wd, layer, live):
    B, E, rows, D = xg.shape
    F = wg.shape[-1]
    return pl.pallas_call(
        functools.partial(_ffn_kernel, live=live),
        grid=(E, B),
        in_specs=[
            pl.BlockSpec((1, 1, rows, D), lambda e, b: (b, e, 0, 0)),
            pl.BlockSpec((1, 1, rows, LANES), lambda e, b: (b, e, 0, 0)),
            pl.BlockSpec((1, 1, D, F), lambda e, b: (layer, e, 0, 0)),
            pl.BlockSpec((1, 1, D, F), lambda e, b: (layer, e, 0, 0)),
            pl.BlockSpec((1, 1, F, D), lambda e, b: (layer, e, 0, 0)),
        ],
        out_specs=pl.BlockSpec((1, 1, rows, D), lambda e, b: (b, e, 0, 0)),
        out_shape=jax.ShapeDtypeStruct((B, E, rows, D), BF16),
        compiler_params=_cp("parallel", "arbitrary"),
        name="moe_ffn",
    )(xg, gs, wg, wu, wd)


def _expand_kernel(aoff_ref, nch_ref, nmax_ref, h_ref, post_ref, spread_ref, slot1_ref, y_ref, o_ref,
                   *, nblk, tb, W, rows):
    b, blk = pl.program_id(0), pl.program_id(2)
    base = (b * nblk + blk) * N_EXPERTS
    ngrp = N_EXPERTS * W // LANES
    lane = lax.broadcasted_iota(I32, (1, LANES), 1)
    rel = _dot(post_ref[...], spread_ref[...]) - slot1_ref[...]
    never = jnp.float32(-2.0 ** 20)

    def chunk(c, acc):
        tgt, wins = [], []
        for e in range(N_EXPERTS):
            a = aoff_ref[base + e] + c * W
            tgt.append(jnp.where(c < nch_ref[base + e], a.astype(F32), never))
            a_in = pl.multiple_of(jnp.minimum(a, rows - W), BF16_SUBLANES)
            wins.append(y_ref[0, e, pl.ds(a_in, W), :])
        cols = []
        for g in range(ngrp):
            first, last = (LANES * g) // W, (LANES * g + LANES - 1) // W
            t = jnp.full((1, LANES), tgt[last], F32)
            for e in range(last - 1, first - 1, -1):
                t = jnp.where(lane < (e + 1) * W - LANES * g, tgt[e], t)
            hit = rel[:, g * LANES:(g + 1) * LANES] == t
            cols.append(jnp.where(hit, 1.0, 0.0).astype(BF16))
        return acc + _dot(jnp.concatenate(cols, axis=1), jnp.concatenate(wins, axis=0))

    acc = lax.fori_loop(0, nmax_ref[b * nblk + blk], chunk, jnp.zeros(o_ref.shape, F32))
    o_ref[...] = h_ref[...] + acc


def _expand(aoff, nch, nmax, h, post, y, B, T, tb, dw, W):
    N, D = h.shape
    rows = y.shape[2]
    nblk = T // tb
    return pl.pallas_call(
        functools.partial(_expand_kernel, nblk=nblk, tb=tb, W=W, rows=rows),
        grid_spec=pltpu.PrefetchScalarGridSpec(
            num_scalar_prefetch=3,
            grid=(B, D // dw, nblk),
            in_specs=[
                pl.BlockSpec((tb, dw), lambda b, d, k, *_: (b * nblk + k, d)),
                pl.BlockSpec((tb, 2 * N_EXPERTS), lambda b, d, k, *_: (b * nblk + k, 0)),
                pl.BlockSpec((2 * N_EXPERTS, N_EXPERTS * W), lambda b, d, k, *_: (0, 0)),
                pl.BlockSpec((1, N_EXPERTS * W), lambda b, d, k, *_: (0, 0)),
                pl.BlockSpec((1, N_EXPERTS, rows, dw), lambda b, d, k, *_: (b, 0, 0, d),
                             pipeline_mode=pl.Buffered(1)),
            ],
            out_specs=pl.BlockSpec((tb, dw), lambda b, d, k, *_: (b * nblk + k, d)),
        ),
        out_shape=jax.ShapeDtypeStruct((N, D), F32),
        compiler_params=_cp("parallel", "parallel", "arbitrary"),
        name="moe_expand",
    )(aoff, nch, nmax, h, post, jnp.asarray(_spread_table(W), F32).astype(BF16),
      jnp.asarray((np.arange(N_EXPERTS * W) % W + 1).reshape(1, -1), F32), y)


def _moe(h, xn, gsp, aff4, wg, wu, wd, layer, B, T):
    N, D = h.shape
    tb = min(512, T)
    W = 96
    nblk = T // tb
    cap = EC_CAPACITY_FACTOR * T // N_EXPERTS
    live = -(-cap // BF16_SUBLANES) * BF16_SUBLANES
    rows = -(-(live + W) // LANES) * LANES
    dw = 256
    pos4, aoff4, cnt4 = _topk(aff4, T, tb)
    per = tb // LANES
    first = aoff4[:, ::per, :, 0]
    start = first & ~(BF16_SUBLANES - 1)
    aoff = start.reshape(-1)
    nch4 = (first - start + cnt4[:, ::per, :, 0] + (W - 1)) // W
    nch = nch4.reshape(-1)
    nmax = jnp.max(nch4, axis=-1).reshape(-1)
    post = jnp.transpose(pos4, (0, 1, 3, 2)).reshape(N, N_EXPERTS) + 1
    post = jnp.concatenate([post >> 5, post & 31], axis=1).astype(BF16)
    xg, gs = _compress(aoff, nch, nmax, xn, gsp, pos4, B, T, tb, rows, W)
    y = _ffn(xg, gs, wg, wu, wd, layer, live)
    return _expand(aoff, nch, nmax, h, post, y, B, T, tb, 2 * dw, W)


def _final_norm_kernel(h_ref, g_ref, o_ref):
    o_ref[...] = _rms(h_ref[...], g_ref[...])


def _final_norm(h, g, tm):
    N, D = h.shape
    return pl.pallas_call(
        _final_norm_kernel,
        grid=(N // tm,),
        in_specs=[pl.BlockSpec((tm, D), lambda i: (i, 0)), pl.BlockSpec((1, D), lambda i: (0, 0))],
        out_specs=pl.BlockSpec((tm, D), lambda i: (i, 0)),
        out_shape=jax.ShapeDtypeStruct((N, D), F32),
        compiler_params=_cp("parallel"),
        name="final_norm",
    )(h, g)


def kernel(x, mem, mix_norm_g, ffn_norm_g, mem_norm_g, final_norm_g, w_mem_kv, w_out, hy_w_in, hy_short_w, hy_filt_w1, hy_filt_b1, hy_filt_w2, hy_filt_b2, hy_filt_w3, hy_filt_freq, hy_skip, at_w_in, at_q_norm_g, at_k_norm_g, router_w, exp_w_gate, exp_w_up, exp_w_down):
    B, T, D = x.shape
    depth = mix_norm_g.shape[0]
    C = hy_skip.shape[-1]
    tm = min(512, T)
    cb = 16
    n1 = T // LANES
    h = x.reshape(B * T, D)
    row = lambda v: v.reshape(1, -1).astype(F32)
    lanes = lambda v: jnp.broadcast_to(v[..., None, None], v.shape + (1, LANES)).astype(F32)
    max_decay = math.log(HY_DECAY_TARGET) / HY_FAST_PCT
    min_decay = math.log(HY_DECAY_TARGET) / HY_SLOW_PCT
    absdelta = jnp.asarray(np.abs(np.linspace(min_decay, max_decay, C)).astype(np.float32)).reshape(C, 1)

    for i in range(depth):
        j = i // 2
        wo = w_out[i].astype(BF16)
        wkv = w_mem_kv[i].astype(BF16)
        mkt, mv = _memkv(mem, row(mem_norm_g), wkv[:, :D_MEM].T, wkv[:, D_MEM:])
        wr = router_w[i].astype(BF16)
        route_w = (row(ffn_norm_g[i]), jnp.zeros((D, LANES), BF16).at[:, :N_EXPERTS].set(wr), wr.T)
        if i % 2 == 0:
            w_in = hy_w_in[j].astype(BF16)
            ut, cq = _hy_inproj(h, row(mix_norm_g[i]), w_in[:, :3 * C].T, w_in[:, 3 * C:], B, T, tm)
            filt = dict(
                w1t=jnp.zeros((HY_FILT, 40), F32).at[:, :hy_filt_w1.shape[1]].set(hy_filt_w1[j].T),
                b1=hy_filt_b1[j].reshape(-1, 1), w2t=hy_filt_w2[j].T, b2=hy_filt_b2[j].reshape(-1, 1),
                fr=hy_filt_freq[j].reshape(-1, 1),
                w3t=hy_filt_w3[j].T.reshape(2, 2, C, HY_FILT), absdelta=absdelta,
            )
            kfft = _hyena_filters_fft(filt, T, 2 * cb)
            zt = _hy_conv(ut.reshape(B, 3 * C, n1, LANES), lanes(hy_short_w[j]), lanes(hy_skip[j]),
                          kfft, B, T, C, cb)
            routed = _outproj(h, zt, cq, mkt, mv, wo[:C], wo[C:], *route_w, B, T, min(1024, T))
        else:
            rep = lambda v, n: jnp.tile(v.astype(F32), n).reshape(1, -1)
            qt, kz, vtz, cq = _at_inproj(h, row(mix_norm_g[i]), at_w_in[j].astype(BF16),
                                         rep(at_q_norm_g[j], GQA_GROUP * N_KV_HEADS),
                                         rep(at_k_norm_g[j], N_KV_HEADS), B, T, tm)
            main_t = _flash(qt, kz, vtz, B, T, min(2048, T), min(1024, T))
            routed = _outproj(h, main_t, cq, mkt, mv, wo[:C], wo[C:], *route_w, B, T, tm)
        h = _moe(*routed, exp_w_gate, exp_w_up, exp_w_down, i, B, T)
    return _final_norm(h, row(final_norm_g), tm).reshape(B, T, D)
```

```python
import functools
import math

import numpy as np
import jax
import jax.numpy as jnp
from jax import lax
from jax.experimental import pallas as pl
from jax.experimental.pallas import tpu as pltpu

F32 = jnp.float32
BF16 = jnp.bfloat16
I32 = jnp.int32

HEAD_DIM = 64
MEM_HEADS = 4
D_MEM = MEM_HEADS * HEAD_DIM
N_KV_HEADS = 4
GQA_GROUP = 3
GRID_W = 64
ROPE_THETA = 10000.0
ROPE_AXIS_DIM = HEAD_DIM // 2
HY_BANDS = 16
HY_FILT = 64
HY_DECAY_TARGET = 1e-2
HY_FAST_PCT = 0.3
HY_SLOW_PCT = 1.5
N_EXPERTS = 16
EC_CAPACITY_FACTOR = 2
NORM_EPS = 1e-6

LANES = 128
BF16_SUBLANES = 16
VMEM_LIMIT = 56 * 1024 * 1024
HI = lax.Precision.HIGHEST
LOG2E = 1.4426950408889634


def _cp(*sem):
    return pltpu.CompilerParams(dimension_semantics=sem, vmem_limit_bytes=VMEM_LIMIT)


def _rms(x, g):
    ms = jnp.mean(x * x, axis=-1, keepdims=True)
    return x * lax.rsqrt(ms + NORM_EPS) * g


def _dot(a, b):
    return jnp.dot(a, b, preferred_element_type=F32)


def _dot_nt(a, b):
    return lax.dot_general(a, b, (((1,), (1,)), ((), ())), preferred_element_type=F32)


@functools.lru_cache(maxsize=None)
def _dft_tables(T, cb):
    nf = 2 * T
    n1k = nf // LANES
    n1 = T // LANES
    k1 = np.arange(n1k)[:, None]
    a1 = 2 * np.pi * k1 * np.arange(n1)[None, :] / n1k
    a1f = 2 * np.pi * k1 * np.arange(n1k)[None, :] / n1k
    f1full = np.concatenate([np.cos(a1f), -np.sin(a1f)], axis=0)
    tw = 2 * np.pi * k1 * np.arange(LANES)[None, :] / nf
    twr, twi = np.cos(tw), -np.sin(tw)
    a2 = 2 * np.pi * np.arange(LANES)[:, None] * np.arange(LANES)[None, :] / LANES
    cr, ci = np.cos(a2), -np.sin(a2)
    w2 = np.block([[cr, ci], [-ci, cr]])
    minv = np.block([[cr, -ci], [ci, cr]])
    a3 = 2 * np.pi * np.arange(n1)[:, None] * np.arange(n1k)[None, :] / n1k
    g1c = np.block([[np.cos(a3), -np.sin(a3)], [np.sin(a3), np.cos(a3)]]) / nf
    f1c = np.block([[np.cos(a1), np.sin(a1)], [-np.sin(a1), np.cos(a1)]])
    return dict(
        f1c=f1c, f1full=f1full, w2=w2, minv=minv, g1c=g1c,
        twr_l=np.tile(twr, (1, cb)), twi_l=np.tile(twi, (1, cb)),
        twr_r=np.tile(twr, (cb, 1)), twi_r=np.tile(twi, (cb, 1)),
    )


@functools.lru_cache(maxsize=None)
def _filter_feats(T):
    t = np.linspace(0.0, 1.0, T)[None, :]
    w = (2.0 * np.pi) * np.arange(T)[None, :] / T
    bands = np.linspace(1e-4, HY_BANDS - 1, HY_BANDS)[:, None]
    feats = np.concatenate([t, np.cos(bands * w), -np.sin(bands * w)], axis=0)
    pad = np.zeros((40 - feats.shape[0], T))
    feats = np.concatenate([feats, pad], axis=0)
    rev = (T - np.arange(T)) % T
    return (np.stack([feats, feats[:, rev]]).astype(np.float32),
            np.stack([t, t[:, rev]]).astype(np.float32))


@functools.lru_cache(maxsize=None)
def _rope_tables(T):
    rows = T // GRID_W
    pos_row = np.repeat(np.arange(rows), GRID_W).astype(np.float64)
    pos_col = np.tile(np.arange(GRID_W), rows).astype(np.float64)
    inv = 1.0 / (ROPE_THETA ** (np.arange(0, ROPE_AXIS_DIM, 2, dtype=np.float64) / ROPE_AXIS_DIM))
    lane = np.arange(LANES)
    d = lane % HEAD_DIM
    axis = d // ROPE_AXIS_DIM
    half = (d // (ROPE_AXIS_DIM // 2)) % 2
    f = d % (ROPE_AXIS_DIM // 2)
    pos = np.where(axis[None, :] == 0, pos_row[:, None], pos_col[:, None])
    ang = pos * inv[f][None, :]
    cos2 = np.cos(ang)
    sin2 = np.where(half[None, :] == 0, -np.sin(ang), np.sin(ang))
    return cos2.astype(np.float32), sin2.astype(np.float32)


@functools.lru_cache(maxsize=None)
def _spread_table(W):
    owner = np.arange(N_EXPERTS * W) // W
    hit = (np.arange(N_EXPERTS)[:, None] == owner[None, :]).astype(np.float32)
    return np.concatenate([32.0 * hit, hit], axis=0)


@functools.lru_cache(maxsize=None)
def _topk_tables(T, tb):
    nj = T // LANES
    per = tb // LANES
    r = np.arange(nj * N_EXPERTS)
    j, e = r // N_EXPERTS, r % N_EXPERTS
    same_e = e[:, None] == e[None, :]
    blk = j // per
    same_blk = blk[:, None] == blk[None, :]
    m_all = same_e & (j[None, :] < j[:, None])
    m_w = same_e & same_blk & (j[None, :] < j[:, None])
    m_b = same_e & same_blk
    tri = np.triu(np.ones((LANES, LANES)))
    return tuple(np.asarray(m, np.float32) for m in (m_all, m_w, m_b, tri))


def _hy_inproj_kernel(h_ref, g_ref, wmt_ref, wcq_ref, ut_ref, cq_ref):
    xn = _rms(h_ref[...], g_ref[...]).astype(BF16)
    ut_ref[0] = _dot_nt(wmt_ref[...], xn)
    cq_ref[...] = _dot(xn, wcq_ref[...]).astype(BF16)


def _hy_inproj(h, g, wmt, wcq, B, T, tm):
    N, D = h.shape
    c3 = wmt.shape[0]
    nt = T // tm
    return pl.pallas_call(
        _hy_inproj_kernel,
        grid=(B, nt),
        in_specs=[
            pl.BlockSpec((tm, D), lambda b, i: (b * nt + i, 0)),
            pl.BlockSpec((1, D), lambda b, i: (0, 0)),
            pl.BlockSpec((c3, D), lambda b, i: (0, 0)),
            pl.BlockSpec((D, D_MEM), lambda b, i: (0, 0)),
        ],
        out_specs=[
            pl.BlockSpec((1, c3, tm), lambda b, i: (b, 0, i)),
            pl.BlockSpec((tm, D_MEM), lambda b, i: (b * nt + i, 0)),
        ],
        out_shape=[
            jax.ShapeDtypeStruct((B, c3, T), F32),
            jax.ShapeDtypeStruct((N, D_MEM), BF16),
        ],
        compiler_params=_cp("parallel", "parallel"),
        name="hy_inproj",
    )(h, g, wmt, wcq)


def _filt_mlp_kernel(feats_ref, w1t_ref, b1_ref, w2t_ref, b2_ref, fr_ref, h2_ref):
    fr = fr_ref[...]
    for d in range(2):
        a = jnp.dot(w1t_ref[...], feats_ref[d], precision=HI, preferred_element_type=F32)
        h1 = jnp.sin(fr * (a + b1_ref[...]))
        a = jnp.dot(w2t_ref[...], h1, precision=HI, preferred_element_type=F32)
        h2_ref[d] = jnp.sin(fr * (a + b2_ref[...]))


def _filt_taps_kernel(h2_ref, w3t_ref, t_ref, dl_ref, out_ref):
    T = h2_ref.shape[2]
    dl = dl_ref[...]
    hf = jnp.dot(w3t_ref[0, 0], h2_ref[0], precision=HI, preferred_element_type=F32) * jnp.exp(-t_ref[0] * dl)
    hb = jnp.dot(w3t_ref[0, 1], h2_ref[1], precision=HI, preferred_element_type=F32) * jnp.exp(-t_ref[1] * dl)
    nrm = jnp.sum(jnp.abs(hf) + jnp.abs(hb), axis=-1, keepdims=True) + 1e-6
    inv = 1.0 / nrm
    tap0 = lax.broadcasted_iota(I32, hf.shape, 1) == 0
    out_ref[0, :, :T] = ((hf + jnp.where(tap0, hb, 0.0)) * inv).astype(BF16)
    out_ref[0, :, T:] = (jnp.where(tap0, 0.0, hb) * inv).astype(BF16)


def _fwd_fft(x3, f1s, twr, twi, w2, cb, n1k):
    rhs = jnp.concatenate([x3[c].astype(BF16) for c in range(cb)], axis=1)
    a = _dot(f1s, rhs)
    ar, ai = a[:n1k], a[n1k:]
    tr = (ar * twr - ai * twi).astype(BF16)
    ti = (ar * twi + ai * twr).astype(BF16)
    lr = jnp.concatenate([tr[:, c * LANES:(c + 1) * LANES] for c in range(cb)], axis=0)
    li = jnp.concatenate([ti[:, c * LANES:(c + 1) * LANES] for c in range(cb)], axis=0)
    return _dot(jnp.concatenate([lr, li], axis=1), w2)


def _inv_fft(y, minv, twr, twi, g1, cb, n1k):
    b = _dot(y.astype(BF16), minv)
    br, bi = b[:, :LANES], b[:, LANES:]
    pr = (br * twr + bi * twi).astype(BF16)
    pi = (bi * twr - br * twi).astype(BF16)
    top = jnp.concatenate([pr[c * n1k:(c + 1) * n1k] for c in range(cb)], axis=1)
    bot = jnp.concatenate([pi[c * n1k:(c + 1) * n1k] for c in range(cb)], axis=1)
    rhs = jnp.concatenate([top, bot], axis=0)
    return _dot(g1, rhs)


def _filt_fft_kernel(taps_ref, f1_ref, twr_ref, twi_ref, w2_ref, k_ref, *, cb, n1k):
    k_ref[0] = _fwd_fft(taps_ref[0], f1_ref[...], twr_ref[...], twi_ref[...], w2_ref[...], cb, n1k).astype(BF16)


def _hyena_filters_fft(p, T, cb):
    feats, t_rows = _filter_feats(T)
    C = p["w3t"].shape[2]
    tabs = _dft_tables(T, cb)
    n1k = 2 * T // LANES
    h2 = pl.pallas_call(
        _filt_mlp_kernel,
        out_shape=jax.ShapeDtypeStruct((2, HY_FILT, T), F32),
        compiler_params=pltpu.CompilerParams(vmem_limit_bytes=VMEM_LIMIT),
        name="hy_filt_mlp",
    )(jnp.asarray(feats), p["w1t"], p["b1"], p["w2t"], p["b2"], p["fr"])
    cbt = 64
    taps = pl.pallas_call(
        _filt_taps_kernel,
        grid=(2, C // cbt),
        in_specs=[
            pl.BlockSpec((2, HY_FILT, T), lambda o, c: (0, 0, 0)),
            pl.BlockSpec((1, 2, cbt, HY_FILT), lambda o, c: (o, 0, c, 0)),
            pl.BlockSpec((2, 1, T), lambda o, c: (0, 0, 0)),
            pl.BlockSpec((cbt, 1), lambda o, c: (c, 0)),
        ],
        out_specs=pl.BlockSpec((1, cbt, 2 * T), lambda o, c: (o, c, 0)),
        out_shape=jax.ShapeDtypeStruct((2, C, 2 * T), BF16),
        compiler_params=_cp("parallel", "parallel"),
        name="hy_filt_taps",
    )(h2, p["w3t"], jnp.asarray(t_rows), p["absdelta"])
    taps = taps.reshape(2, C, n1k, LANES)
    const = lambda shape: pl.BlockSpec(shape, lambda o, c: (0,) * len(shape))
    f32 = lambda k: jnp.asarray(tabs[k], F32)
    return pl.pallas_call(
        functools.partial(_filt_fft_kernel, cb=cb, n1k=n1k),
        grid=(2, C // cb),
        in_specs=[
            pl.BlockSpec((1, cb, n1k, LANES), lambda o, c: (o, c, 0, 0)),
            const((2 * n1k, n1k)), const((n1k, cb * LANES)), const((n1k, cb * LANES)),
            const((2 * LANES, 2 * LANES)),
        ],
        out_specs=pl.BlockSpec((1, cb * n1k, 2 * LANES), lambda o, c: (o, c, 0)),
        out_shape=jax.ShapeDtypeStruct((2, C * n1k, 2 * LANES), BF16),
        compiler_params=_cp("parallel", "parallel"),
        name="hy_filt_fft",
    )(taps, f32("f1full").astype(BF16), f32("twr_l"), f32("twi_l"), f32("w2").astype(BF16))


def _time_neighbours(x):
    rows = x.shape[0]
    lane = lax.broadcasted_iota(I32, x.shape, 1)
    r = pltpu.roll(x, 1, 1)
    rr = pltpu.roll(r, 1, 0)
    prev = jnp.where(lane == 0, rr, r)
    r2 = pltpu.roll(x, LANES - 1, 1)
    rr2 = pltpu.roll(r2, rows - 1, 0)
    nxt = jnp.where(lane == LANES - 1, rr2, r2)
    return prev, nxt


def _hy_conv_kernel(x1_ref, x2_ref, v_ref, sw1_ref, sw2_ref, swv_ref, skip_ref, k_ref,
                    f1c_ref, twrl_ref, twil_ref, w2_ref, minv_ref, twrr_ref, twir_ref,
                    g1c_ref, o_ref, *, cb, n1k, n1):
    rows = cb * n1
    shape2 = (rows, LANES)
    row = lax.broadcasted_iota(I32, shape2, 0)
    lane = lax.broadcasted_iota(I32, shape2, 1)
    first = (lane == 0) & (row % n1 == 0)
    last = (lane == LANES - 1) & (row % n1 == n1 - 1)

    def sconv(x, sw_ref):
        x = x.reshape(shape2)
        prev, nxt = _time_neighbours(x)
        prev = jnp.where(first, 0.0, prev)
        nxt = jnp.where(last, 0.0, nxt)
        w = [jnp.broadcast_to(sw_ref[j], (cb, n1, LANES)).reshape(shape2) for j in range(3)]
        return prev * w[0] + x * w[1] + nxt * w[2]

    z = [sconv(v_ref[b], swv_ref) for b in range(2)]
    gates = [[sconv(x_ref[b], sw_ref) for b in range(2)] for x_ref, sw_ref in ((x1_ref, sw1_ref), (x2_ref, sw2_ref))]
    f1c, w2 = f1c_ref[...], w2_ref[...]
    for o in range(2):
        pair = jnp.concatenate([z[0].reshape(cb, n1, LANES), z[1].reshape(cb, n1, LANES)], axis=1)
        zf = _fwd_fft(pair, f1c, twrl_ref[...], twil_ref[...], w2, cb, n1k)
        kk = k_ref[o].astype(F32)
        zr, zi = zf[:, :LANES], zf[:, LANES:]
        kr, ki = kk[:, :LANES], kk[:, LANES:]
        y = jnp.concatenate([zr * kr - zi * ki, zr * ki + zi * kr], axis=1)
        conv = _inv_fft(y, minv_ref[...], twrr_ref[...], twir_ref[...], g1c_ref[...], cb, n1k)
        skip = jnp.broadcast_to(skip_ref[o], (cb, n1, LANES)).reshape(shape2)
        for b in range(2):
            cv = conv[b * n1:(b + 1) * n1]
            cv = jnp.concatenate([cv[:, c * LANES:(c + 1) * LANES] for c in range(cb)], axis=0)
            z[b] = gates[o][b] * (cv + skip * z[b])
    for b in range(2):
        o_ref[b] = z[b].reshape(cb, n1, LANES)


def _hy_conv(ut4, sw, skip, kfft, B, T, C, cb):
    n1k, n1 = 2 * T // LANES, T // LANES
    tabs = _dft_tables(T, cb)
    nct = C // cb
    bf = lambda k: jnp.asarray(tabs[k], F32).astype(BF16)
    f32 = lambda k: jnp.asarray(tabs[k], F32)
    const = lambda shape: pl.BlockSpec(shape, lambda c, b: (0,) * len(shape))
    ublk = lambda s: pl.BlockSpec((2, cb, n1, LANES), lambda c, b, s=s: (b, s * nct + c, 0, 0))
    wblk = lambda s: pl.BlockSpec((3, cb, 1, LANES), lambda c, b, s=s: (0, s * nct + c, 0, 0))
    return pl.pallas_call(
        functools.partial(_hy_conv_kernel, cb=cb, n1k=n1k, n1=n1),
        grid=(nct, B // 2),
        in_specs=[
            ublk(0), ublk(1), ublk(2), wblk(0), wblk(1), wblk(2),
            pl.BlockSpec((2, cb, 1, LANES), lambda c, b: (0, c, 0, 0)),
            pl.BlockSpec((2, cb * n1k, 2 * LANES), lambda c, b: (0, c, 0)),
            const((2 * n1k, 2 * n1)), const((n1k, cb * LANES)), const((n1k, cb * LANES)),
            const((2 * LANES, 2 * LANES)), const((2 * LANES, 2 * LANES)),
            const((cb * n1k, LANES)), const((cb * n1k, LANES)), const((2 * n1, 2 * n1k)),
        ],
        out_specs=pl.BlockSpec((2, cb, n1, LANES), lambda c, b: (b, c, 0, 0)),
        out_shape=jax.ShapeDtypeStruct((B, C, n1, LANES), F32),
        compiler_params=_cp("parallel", "arbitrary"),
        name="hy_conv",
    )(ut4, ut4, ut4, sw, sw, sw, skip, kfft,
      bf("f1c"), f32("twr_l"), f32("twi_l"), bf("w2"), bf("minv"),
      f32("twr_r"), f32("twi_r"), bf("g1c"))


def _head_norm(x, bd, g):
    ss = _dot((x * x).astype(BF16), bd)
    return x * lax.rsqrt(ss * (1.0 / HEAD_DIM) + NORM_EPS) * g


def _rope(x, cos2, sin2):
    lane = lax.broadcasted_iota(I32, cos2.shape, 1)
    low = (lane // (ROPE_AXIS_DIM // 2)) % 2 == 0
    out = []
    for c in range(x.shape[1] // LANES):
        xc = x[:, c * LANES:(c + 1) * LANES]
        up = pltpu.roll(xc, LANES - 16, 1)
        dn = pltpu.roll(xc, 16, 1)
        out.append(xc * cos2 + jnp.where(low, up, dn) * sin2)
    return jnp.concatenate(out, axis=1)


def _at_inproj_kernel(h_ref, g_ref, w_ref, bdq_ref, bdk_ref, gq_ref, gk_ref, cos_ref, sin_ref,
                      qt_ref, kz_ref, vt_ref, cq_ref, *, dq, dk):
    xn = _rms(h_ref[...], g_ref[...]).astype(BF16)
    proj = _dot(xn, w_ref[...])
    q, k = proj[:, :dq], proj[:, dq:dq + dk]
    v, cq = proj[:, dq + dk:dq + 2 * dk], proj[:, dq + 2 * dk:]
    cos2, sin2 = cos_ref[...], sin_ref[...]
    qr = _rope(_head_norm(q, bdq_ref[...], gq_ref[...]), cos2, sin2)
    qt = (qr * (HEAD_DIM ** -0.5 * LOG2E)).T.astype(BF16)
    for p in range(dq // LANES):
        qt_ref[0, p] = qt[p * LANES:(p + 1) * LANES]
    kr = _rope(_head_norm(k, bdk_ref[...], gk_ref[...]), cos2, sin2)
    vt = v.T.astype(BF16)
    zv = jnp.zeros((HEAD_DIM, vt.shape[1]), BF16)
    lane = lax.broadcasted_iota(I32, (k.shape[0], LANES), 1)
    for kv in range(N_KV_HEADS):
        rows = vt[kv * HEAD_DIM:(kv + 1) * HEAD_DIM]
        vt_ref[0, kv, 0] = jnp.concatenate([rows, zv], axis=0)
        vt_ref[0, kv, 1] = jnp.concatenate([zv, rows], axis=0)
        pair = kr[:, (kv // 2) * LANES:(kv // 2 + 1) * LANES]
        own = jnp.where((lane < HEAD_DIM) == (kv % 2 == 0), pair, 0.0)
        other = pltpu.roll(own, HEAD_DIM, 1)
        lo, hi = (own, other) if kv % 2 == 0 else (other, own)
        kz_ref[0, kv, 0] = lo.astype(BF16)
        kz_ref[0, kv, 1] = hi.astype(BF16)
    cq_ref[...] = cq.astype(BF16)


def _at_inproj(h, g, w, gq, gk, B, T, tm):
    N, D = h.shape
    dq, dk = GQA_GROUP * N_KV_HEADS * HEAD_DIM, N_KV_HEADS * HEAD_DIM
    nt = T // tm
    cos2, sin2 = _rope_tables(T)
    bd = lambda n: jnp.asarray(np.kron(np.eye(n // HEAD_DIM), np.ones((HEAD_DIM, HEAD_DIM))), F32).astype(BF16)
    const = lambda shape: pl.BlockSpec(shape, lambda b, i: (0,) * len(shape))
    return pl.pallas_call(
        functools.partial(_at_inproj_kernel, dq=dq, dk=dk),
        grid=(B, nt),
        in_specs=[
            pl.BlockSpec((tm, D), lambda b, i: (b * nt + i, 0)),
            const((1, D)), const(w.shape), const((dq, dq)), const((dk, dk)),
            const((1, dq)), const((1, dk)),
            pl.BlockSpec((tm, LANES), lambda b, i: (i, 0)),
            pl.BlockSpec((tm, LANES), lambda b, i: (i, 0)),
        ],
        out_specs=[
            pl.BlockSpec((1, dq // LANES, LANES, tm), lambda b, i: (b, 0, 0, i)),
            pl.BlockSpec((1, N_KV_HEADS, 2, tm, LANES), lambda b, i: (b, 0, 0, i, 0)),
            pl.BlockSpec((1, N_KV_HEADS, 2, LANES, tm), lambda b, i: (b, 0, 0, 0, i)),
            pl.BlockSpec((tm, D_MEM), lambda b, i: (b * nt + i, 0)),
        ],
        out_shape=[
            jax.ShapeDtypeStruct((B, dq // LANES, LANES, T), BF16),
            jax.ShapeDtypeStruct((B, N_KV_HEADS, 2, T, LANES), BF16),
            jax.ShapeDtypeStruct((B, N_KV_HEADS, 2, LANES, T), BF16),
            jax.ShapeDtypeStruct((N, D_MEM), BF16),
        ],
        compiler_params=_cp("parallel", "parallel"),
        name="at_inproj",
    )(h, g, w, bd(dq), bd(dk), gq, gk, jnp.asarray(cos2), jnp.asarray(sin2))


def _flash_kernel(qt_ref, ka_ref, kb_ref, vta_ref, vtb_ref, o_ref, m_sc, l_sc, acc_sc, s_sc, p_sc):
    ki = pl.program_id(3)

    @pl.when(ki == 0)
    def _():
        m_sc[...] = jnp.full(m_sc.shape, -jnp.inf, F32)
        l_sc[...] = jnp.zeros(l_sc.shape, F32)
        acc_sc[...] = jnp.zeros(acc_sc.shape, F32)

    qt = qt_ref[0, 0]
    tk, tq = s_sc.shape[1], s_sc.shape[2]
    sub = 8
    ck = 2 * sub
    pv, alphas = [], []
    for idx, k_ref in enumerate((ka_ref, kb_ref)):
        s_sc[idx] = _dot(k_ref[0, 0, 0], qt)
    for idx, vt_ref in enumerate((vta_ref, vtb_ref)):
        mx = s_sc[idx, 0:sub, :]
        for c in range(1, tk // sub):
            mx = jnp.maximum(mx, s_sc[idx, c * sub:(c + 1) * sub, :])
        m_prev = m_sc[idx]
        m_new = jnp.maximum(m_prev, jnp.max(mx, axis=0, keepdims=True))
        alpha = jnp.exp2(m_prev - m_new)
        lsum = jnp.zeros((sub, tq), F32)
        for c in range(tk // ck):
            p = jnp.exp2(s_sc[idx, c * ck:(c + 1) * ck, :] - m_new)
            lsum = lsum + p[:sub] + p[sub:]
            p_sc[idx, c * ck:(c + 1) * ck, :] = p.astype(BF16)
        l_sc[idx] = alpha * l_sc[idx] + jnp.sum(lsum, axis=0, keepdims=True)
        m_sc[idx] = m_new
        pv.append(_dot(vt_ref[0, 0, 0], p_sc[idx]))
        alphas.append(alpha)
    row = lax.broadcasted_iota(I32, acc_sc.shape, 0)
    low = row < HEAD_DIM
    acc_sc[...] = acc_sc[...] * jnp.where(low, alphas[0], alphas[1]) + pv[0] + pv[1]

    @pl.when(ki == pl.num_programs(3) - 1)
    def _():
        o_ref[0] = acc_sc[...] / jnp.where(low, l_sc[0], l_sc[1])


def _flash(qt, kz, vtz, B, T, tq, tk):
    npair = qt.shape[1]
    nq, nk = T // tq, T // tk
    kv_of = lambda p, j: (2 * p + j) // GQA_GROUP
    return pl.pallas_call(
        _flash_kernel,
        grid=(B, npair, nq, nk),
        in_specs=[
            pl.BlockSpec((1, 1, LANES, tq), lambda b, p, i, k: (b, p, 0, i)),
            pl.BlockSpec((1, 1, 1, tk, LANES), lambda b, p, i, k: (b, kv_of(p, 0), 0, k, 0)),
            pl.BlockSpec((1, 1, 1, tk, LANES), lambda b, p, i, k: (b, kv_of(p, 1), 1, k, 0)),
            pl.BlockSpec((1, 1, 1, LANES, tk), lambda b, p, i, k: (b, kv_of(p, 0), 0, 0, k)),
            pl.BlockSpec((1, 1, 1, LANES, tk), lambda b, p, i, k: (b, kv_of(p, 1), 1, 0, k)),
        ],
        out_specs=pl.BlockSpec((1, LANES, tq), lambda b, p, i, k: (b, p, i)),
        out_shape=jax.ShapeDtypeStruct((B, npair * LANES, T), F32),
        scratch_shapes=[
            pltpu.VMEM((2, 1, tq), F32), pltpu.VMEM((2, 1, tq), F32), pltpu.VMEM((LANES, tq), F32),
            pltpu.VMEM((2, tk, tq), F32), pltpu.VMEM((2, tk, tq), BF16),
        ],
        compiler_params=_cp("parallel", "parallel", "parallel", "arbitrary"),
        name="flash_gqa",
    )(qt, kz, kz, vtz, vtz)


def _memkv_kernel(mem_ref, g_ref, wkvt_ref, wv_ref, mkt_ref, mv_ref):
    mn = _rms(mem_ref[0], g_ref[...]).astype(BF16)
    kt = _dot_nt(wkvt_ref[...], mn) * (HEAD_DIM ** -0.5)
    v = _dot(mn, wv_ref[...])
    row = lax.broadcasted_iota(I32, kt.shape, 0)
    col = lax.broadcasted_iota(I32, v.shape, 1)
    for hd in range(MEM_HEADS):
        mkt_ref[0, hd] = jnp.where(row // HEAD_DIM == hd, kt, 0.0).astype(BF16)
        mv_ref[0, hd] = jnp.where(col // HEAD_DIM == hd, v, 0.0).astype(BF16)


def _memkv(mem, g, wkt, wv):
    B, M, D = mem.shape
    return pl.pallas_call(
        _memkv_kernel,
        grid=(B,),
        in_specs=[
            pl.BlockSpec((1, M, D), lambda b: (b, 0, 0)),
            pl.BlockSpec((1, D), lambda b: (0, 0)),
            pl.BlockSpec((D_MEM, D), lambda b: (0, 0)),
            pl.BlockSpec((D, D_MEM), lambda b: (0, 0)),
        ],
        out_specs=[
            pl.BlockSpec((1, MEM_HEADS, D_MEM, M), lambda b: (b, 0, 0, 0)),
            pl.BlockSpec((1, MEM_HEADS, M, D_MEM), lambda b: (b, 0, 0, 0)),
        ],
        out_shape=[
            jax.ShapeDtypeStruct((B, MEM_HEADS, D_MEM, M), BF16),
            jax.ShapeDtypeStruct((B, MEM_HEADS, M, D_MEM), BF16),
        ],
        compiler_params=_cp("parallel"),
        name="mem_kv",
    )(mem, g, wkt, wv)


def _cross_attn(cq, mkt_ref, mv_ref):
    acc = jnp.zeros(cq.shape, F32)
    for hd in range(MEM_HEADS):
        s = _dot(cq, mkt_ref[0, hd])
        p = jnp.exp(s - jnp.max(s, axis=-1, keepdims=True))
        p = p / jnp.sum(p, axis=-1, keepdims=True)
        acc = acc + _dot(p.astype(BF16), mv_ref[0, hd])
    return acc


def _route(h, g_ref, wr_ref, wrt_ref, xn_ref, gsp_ref, afft_ref):
    xn = _rms(h, g_ref[...]).astype(BF16)
    xn_ref[...] = xn
    lg = _dot(xn, wr_ref[...])
    lane = lax.broadcasted_iota(I32, lg.shape, 1)
    lg = jnp.where(lane < N_EXPERTS, lg, -jnp.inf)
    p = jnp.exp(lg - jnp.max(lg, axis=-1, keepdims=True))
    aff = p / jnp.sum(p, axis=-1, keepdims=True)
    hi = aff.astype(BF16).astype(F32)
    mid = (aff - hi).astype(BF16).astype(F32)
    lo = (aff - hi - mid).astype(BF16).astype(F32)
    gsp_ref[...] = (hi + pltpu.roll(mid, N_EXPERTS, 1) + pltpu.roll(lo, 2 * N_EXPERTS, 1)).astype(BF16)
    lt = _dot_nt(wrt_ref[...], xn)
    pt = jnp.exp(lt - jnp.max(lt, axis=0, keepdims=True))
    pt = pt / jnp.sum(pt, axis=0, keepdims=True)
    for j in range(lt.shape[1] // LANES):
        afft_ref[0, j] = pt[:, j * LANES:(j + 1) * LANES]


def _outproj_kernel(h_ref, main_ref, cq_ref, mkt_ref, mv_ref, wm_ref, wc_ref, g_ref, wr_ref, wrt_ref,
                    o_ref, xn_ref, gsp_ref, afft_ref):
    if len(main_ref.shape) == 3:
        main = main_ref[0].T
    else:
        main = jnp.concatenate([main_ref[0, :, j, :].T for j in range(main_ref.shape[2])], axis=0)
    main = main.astype(BF16)
    cross = _cross_attn(cq_ref[...], mkt_ref, mv_ref).astype(BF16)
    h_new = h_ref[...] + _dot(main, wm_ref[...]) + _dot(cross, wc_ref[...])
    o_ref[...] = h_new
    _route(h_new, g_ref, wr_ref, wrt_ref, xn_ref, gsp_ref, afft_ref)


def _outproj(h, main, cq, mkt, mv, wm, wc, g, wr, wrt, B, T, tm):
    N, D = h.shape
    C = wm.shape[0]
    M = mkt.shape[-1]
    nt = T // tm
    if main.ndim == 3:
        mspec = pl.BlockSpec((1, C, tm), lambda b, i: (b, 0, i))
    else:
        mspec = pl.BlockSpec((1, C, tm // LANES, LANES), lambda b, i: (b, 0, i, 0))
    return pl.pallas_call(
        _outproj_kernel,
        grid=(B, nt),
        in_specs=[
            pl.BlockSpec((tm, D), lambda b, i: (b * nt + i, 0)),
            mspec,
            pl.BlockSpec((tm, D_MEM), lambda b, i: (b * nt + i, 0)),
            pl.BlockSpec((1, MEM_HEADS, D_MEM, M), lambda b, i: (b, 0, 0, 0)),
            pl.BlockSpec((1, MEM_HEADS, M, D_MEM), lambda b, i: (b, 0, 0, 0)),
            pl.BlockSpec((C, D), lambda b, i: (0, 0)),
            pl.BlockSpec((D_MEM, D), lambda b, i: (0, 0)),
            pl.BlockSpec((1, D), lambda b, i: (0, 0)),
            pl.BlockSpec((D, LANES), lambda b, i: (0, 0)),
            pl.BlockSpec((N_EXPERTS, D), lambda b, i: (0, 0)),
        ],
        out_specs=[
            pl.BlockSpec((tm, D), lambda b, i: (b * nt + i, 0)),
            pl.BlockSpec((tm, D), lambda b, i: (b * nt + i, 0)),
            pl.BlockSpec((tm, LANES), lambda b, i: (b * nt + i, 0)),
            pl.BlockSpec((1, tm // LANES, N_EXPERTS, LANES), lambda b, i: (b, i, 0, 0)),
        ],
        out_shape=[
            jax.ShapeDtypeStruct((N, D), F32),
            jax.ShapeDtypeStruct((N, D), BF16),
            jax.ShapeDtypeStruct((N, LANES), BF16),
            jax.ShapeDtypeStruct((B, T // LANES, N_EXPERTS, LANES), F32),
        ],
        compiler_params=_cp("parallel", "parallel"),
        name="outproj",
    )(h, main, cq, mkt, mv, wm, wc, g, wr, wrt)


def _topk_kernel(aff_ref, mall_ref, mw_ref, mb_ref, tri_ref, pos_ref, aoff_ref, cnt_ref, *, cap, nj):
    E = N_EXPERTS
    aff3 = aff_ref[0]

    def count(mask3):
        per = jnp.sum(mask3.astype(F32), axis=0)
        return jnp.broadcast_to(jnp.sum(per, axis=-1, keepdims=True), (E, LANES))

    def step(i, thr):
        cand = thr | lax.shift_left(jnp.int32(1), 30 - i)
        ok = count(aff3 >= pltpu.bitcast(cand, F32)[None]) >= cap
        return jnp.where(ok, cand, thr)

    thr = lax.fori_loop(0, 31, step, jnp.zeros((E, LANES), I32))
    thr = pltpu.bitcast(thr, F32)
    gt3 = aff3 > thr[None]
    eq3 = aff3 == thr[None]
    need = cap - count(gt3)

    ones = jnp.ones((LANES, LANES), BF16)
    tri = tri_ref[...]

    def prefix(mask2):
        mb = mask2.astype(BF16)
        incl = _dot(mb, tri)
        tot = _dot(mb, ones)
        return incl - mask2, tot

    eq2 = eq3.reshape(nj * E, LANES).astype(F32)
    ex, tot = prefix(eq2)
    eq_rank = ex + _dot(mall_ref[...], tot.astype(BF16))
    need2 = jnp.broadcast_to(need[None], (nj, E, LANES)).reshape(nj * E, LANES)
    sel = jnp.where((gt3.reshape(nj * E, LANES)) | ((eq2 > 0) & (eq_rank < need2)), 1.0, 0.0)

    ex, tot = prefix(sel)
    totb = tot.astype(BF16)
    before = _dot(mall_ref[...], totb)
    within = _dot(mw_ref[...], totb)
    cnt = _dot(mb_ref[...], totb)
    pos_ref[0] = jnp.where(sel > 0, (before + ex).astype(I32), -1).reshape(nj, E, LANES)
    aoff_ref[0] = (before - within).astype(I32).reshape(nj, E, LANES)
    cnt_ref[0] = cnt.astype(I32).reshape(nj, E, LANES)


def _topk(aff4, T, tb):
    B, nj = aff4.shape[0], aff4.shape[1]
    cap = EC_CAPACITY_FACTOR * T // N_EXPERTS
    tabs = [jnp.asarray(m, F32).astype(BF16) for m in _topk_tables(T, tb)]
    R = nj * N_EXPERTS
    blk = pl.BlockSpec((1, nj, N_EXPERTS, LANES), lambda b: (b, 0, 0, 0))
    const = lambda shape: pl.BlockSpec(shape, lambda b: (0,) * len(shape))
    out = jax.ShapeDtypeStruct((B, nj, N_EXPERTS, LANES), I32)
    return pl.pallas_call(
        functools.partial(_topk_kernel, cap=cap, nj=nj),
        grid=(B,),
        in_specs=[blk, const((R, R)), const((R, R)), const((R, R)), const((LANES, LANES))],
        out_specs=[blk, blk, blk],
        out_shape=[out, out, out],
        compiler_params=_cp("parallel"),
        name="moe_topk",
    )(aff4, *tabs)


def _compress_kernel(aoff_ref, nch_ref, nmax_ref, xn_ref, gsp_ref, pos_ref, xg_ref, gs_ref,
                     *, nblk, tb, W, eg, nsub):
    b, grp, blk = pl.program_id(0), pl.program_id(1), pl.program_id(2)

    @pl.when(blk == 0)
    def _():
        xg_ref[...] = jnp.zeros(xg_ref.shape, BF16)
        gs_ref[...] = jnp.zeros(gs_ref.shape, F32)

    iota_s = lax.broadcasted_iota(I32, (W, LANES), 0)
    per = tb // LANES
    for sb in range(nsub):
        tblk = blk * nsub + sb
        base = (b * nblk + tblk) * N_EXPERTS + grp * eg
        tok = slice(sb * tb, (sb + 1) * tb)

        def chunk(c, carry, base=base, tok=tok, sb=sb):
            pieces = []
            for i in range(eg):
                a = aoff_ref[base + i] + c * W
                e = grp * eg + i
                g = [pos_ref[0, sb * per + jj, pl.ds(e, 1), :] - a == iota_s for jj in range(per)]
                pieces.append(jnp.where(jnp.concatenate(g, axis=1), 1.0, 0.0).astype(BF16))
            lhs = jnp.concatenate(pieces, axis=0)
            res = _dot(lhs, xn_ref[tok, :])
            resg = _dot(lhs, gsp_ref[tok, :])
            for i in range(eg):
                @pl.when(c < nch_ref[base + i])
                def _():
                    win = pl.ds(pl.multiple_of(aoff_ref[base + i] + c * W, BF16_SUBLANES), W)
                    xg_ref[0, i, win, :] = (xg_ref[0, i, win, :] + res[i * W:(i + 1) * W]).astype(BF16)
                    gs_ref[0, i, win, :] = gs_ref[0, i, win, :] + resg[i * W:(i + 1) * W]
            return carry

        lax.fori_loop(0, nmax_ref[b * nblk + tblk], chunk, 0)


def _compress(aoff, nch, nmax, xn, gsp, pos4, B, T, tb, rows, W):
    N, D = xn.shape
    nblk = T // tb
    eg = 4
    nsub = 2 if nblk % 2 == 0 else 1
    ns, ts = nblk // nsub, nsub * tb
    return pl.pallas_call(
        functools.partial(_compress_kernel, nblk=nblk, tb=tb, W=W, eg=eg, nsub=nsub),
        grid_spec=pltpu.PrefetchScalarGridSpec(
            num_scalar_prefetch=3,
            grid=(B, N_EXPERTS // eg, ns),
            in_specs=[
                pl.BlockSpec((ts, D), lambda b, g, k, *_: (b * ns + k, 0)),
                pl.BlockSpec((ts, LANES), lambda b, g, k, *_: (b * ns + k, 0)),
                pl.BlockSpec((1, ts // LANES, N_EXPERTS, LANES), lambda b, g, k, *_: (b, k, 0, 0)),
            ],
            out_specs=[
                pl.BlockSpec((1, eg, rows, D), lambda b, g, k, *_: (b, g, 0, 0)),
                pl.BlockSpec((1, eg, rows, LANES), lambda b, g, k, *_: (b, g, 0, 0)),
            ],
        ),
        out_shape=[
            jax.ShapeDtypeStruct((B, N_EXPERTS, rows, D), BF16),
            jax.ShapeDtypeStruct((B, N_EXPERTS, rows, LANES), F32),
        ],
        compiler_params=_cp("parallel", "parallel", "arbitrary"),
        name="moe_compress",
    )(aoff, nch, nmax, xn, gsp, pos4)


def _ffn_kernel(x_ref, gs_ref, wg_ref, wu_ref, wd_ref, y_ref, *, live):
    e = pl.program_id(0)
    x = x_ref[0, 0, :live]
    g = _dot(x, wg_ref[0, 0].astype(BF16))
    u = _dot(x, wu_ref[0, 0].astype(BF16))
    hid = (g * (1.0 / (1.0 + jnp.exp(-g))) * u).astype(BF16)
    gs = gs_ref[0, 0, :live]
    lane = lax.broadcasted_iota(I32, gs.shape, 1)
    mine = (lane == e) | (lane == e + N_EXPERTS) | (lane == e + 2 * N_EXPERTS)
    gate = jnp.sum(jnp.where(mine, gs, 0.0), axis=-1, keepdims=True)
    y_ref[0, 0, :live] = (_dot(hid, wd_ref[0, 0].astype(BF16)) * gate).astype(BF16)
    y_ref[0, 0, live:] = jnp.zeros((y_ref.shape[2] - live, y_ref.shape[3]), BF16)


def _ffn(xg, gs, wg, wu, wd, layer, live):
    B, E, rows, D = xg.shape
    F = wg.shape[-1]
    return pl.pallas_call(
        functools.partial(_ffn_kernel, live=live),
        grid=(E, B),
        in_specs=[
            pl.BlockSpec((1, 1, rows, D), lambda e, b: (b, e, 0, 0)),
            pl.BlockSpec((1, 1, rows, LANES), lambda e, b: (b, e, 0, 0)),
            pl.BlockSpec((1, 1, D, F), lambda e, b: (layer, e, 0, 0)),
            pl.BlockSpec((1, 1, D, F), lambda e, b: (layer, e, 0, 0)),
            pl.BlockSpec((1, 1, F, D), lambda e, b: (layer, e, 0, 0)),
        ],
        out_specs=pl.BlockSpec((1, 1, rows, D), lambda e, b: (b, e, 0, 0)),
        out_shape=jax.ShapeDtypeStruct((B, E, rows, D), BF16),
        compiler_params=_cp("parallel", "arbitrary"),
        name="moe_ffn",
    )(xg, gs, wg, wu, wd)


def _expand_kernel(aoff_ref, nch_ref, nmax_ref, h_ref, post_ref, spread_ref, slot1_ref, y_ref, o_ref,
                   *, nblk, tb, W, rows):
    b, blk = pl.program_id(0), pl.program_id(2)
    base = (b * nblk + blk) * N_EXPERTS
    ngrp = N_EXPERTS * W // LANES
    lane = lax.broadcasted_iota(I32, (1, LANES), 1)
    rel = _dot(post_ref[...], spread_ref[...]) - slot1_ref[...]
    never = jnp.float32(-2.0 ** 20)

    def chunk(c, acc):
        tgt, wins = [], []
        for e in range(N_EXPERTS):
            a = aoff_ref[base + e] + c * W
            tgt.append(jnp.where(c < nch_ref[base + e], a.astype(F32), never))
            a_in = pl.multiple_of(jnp.minimum(a, rows - W), BF16_SUBLANES)
            wins.append(y_ref[0, e, pl.ds(a_in, W), :])
        cols = []
        for g in range(ngrp):
            first, last = (LANES * g) // W, (LANES * g + LANES - 1) // W
            t = jnp.full((1, LANES), tgt[last], F32)
            for e in range(last - 1, first - 1, -1):
                t = jnp.where(lane < (e + 1) * W - LANES * g, tgt[e], t)
            hit = rel[:, g * LANES:(g + 1) * LANES] == t
            cols.append(jnp.where(hit, 1.0, 0.0).astype(BF16))
        return acc + _dot(jnp.concatenate(cols, axis=1), jnp.concatenate(wins, axis=0))

    acc = lax.fori_loop(0, nmax_ref[b * nblk + blk], chunk, jnp.zeros(o_ref.shape, F32))
    o_ref[...] = h_ref[...] + acc


def _expand(aoff, nch, nmax, h, post, y, B, T, tb, dw, W):
    N, D = h.shape
    rows = y.shape[2]
    nblk = T // tb
    return pl.pallas_call(
        functools.partial(_expand_kernel, nblk=nblk, tb=tb, W=W, rows=rows),
        grid_spec=pltpu.PrefetchScalarGridSpec(
            num_scalar_prefetch=3,
            grid=(B, D // dw, nblk),
            in_specs=[
                pl.BlockSpec((tb, dw), lambda b, d, k, *_: (b * nblk + k, d)),
                pl.BlockSpec((tb, 2 * N_EXPERTS), lambda b, d, k, *_: (b * nblk + k, 0)),
                pl.BlockSpec((2 * N_EXPERTS, N_EXPERTS * W), lambda b, d, k, *_: (0, 0)),
                pl.BlockSpec((1, N_EXPERTS * W), lambda b, d, k, *_: (0, 0)),
                pl.BlockSpec((1, N_EXPERTS, rows, dw), lambda b, d, k, *_: (b, 0, 0, d),
                             pipeline_mode=pl.Buffered(1)),
            ],
            out_specs=pl.BlockSpec((tb, dw), lambda b, d, k, *_: (b * nblk + k, d)),
        ),
        out_shape=jax.ShapeDtypeStruct((N, D), F32),
        compiler_params=_cp("parallel", "parallel", "arbitrary"),
        name="moe_expand",
    )(aoff, nch, nmax, h, post, jnp.asarray(_spread_table(W), F32).astype(BF16),
      jnp.asarray((np.arange(N_EXPERTS * W) % W + 1).reshape(1, -1), F32), y)


def _moe(h, xn, gsp, aff4, wg, wu, wd, layer, B, T):
    N, D = h.shape
    tb = min(512, T)
    W = 96
    nblk = T // tb
    cap = EC_CAPACITY_FACTOR * T // N_EXPERTS
    live = -(-cap // BF16_SUBLANES) * BF16_SUBLANES
    rows = -(-(live + W) // LANES) * LANES
    dw = 256
    pos4, aoff4, cnt4 = _topk(aff4, T, tb)
    per = tb // LANES
    first = aoff4[:, ::per, :, 0]
    start = first & ~(BF16_SUBLANES - 1)
    aoff = start.reshape(-1)
    nch4 = (first - start + cnt4[:, ::per, :, 0] + (W - 1)) // W
    nch = nch4.reshape(-1)
    nmax = jnp.max(nch4, axis=-1).reshape(-1)
    post = jnp.transpose(pos4, (0, 1, 3, 2)).reshape(N, N_EXPERTS) + 1
    post = jnp.concatenate([post >> 5, post & 31], axis=1).astype(BF16)
    xg, gs = _compress(aoff, nch, nmax, xn, gsp, pos4, B, T, tb, rows, W)
    y = _ffn(xg, gs, wg, wu, wd, layer, live)
    return _expand(aoff, nch, nmax, h, post, y, B, T, tb, 2 * dw, W)


def _final_norm_kernel(h_ref, g_ref, o_ref):
    o_ref[...] = _rms(h_ref[...], g_ref[...])


def _final_norm(h, g, tm):
    N, D = h.shape
    return pl.pallas_call(
        _final_norm_kernel,
        grid=(N // tm,),
        in_specs=[pl.BlockSpec((tm, D), lambda i: (i, 0)), pl.BlockSpec((1, D), lambda i: (0, 0))],
        out_specs=pl.BlockSpec((tm, D), lambda i: (i, 0)),
        out_shape=jax.ShapeDtypeStruct((N, D), F32),
        compiler_params=_cp("parallel"),
        name="final_norm",
    )(h, g)


def kernel(x, mem, mix_norm_g, ffn_norm_g, mem_norm_g, final_norm_g, w_mem_kv, w_out, hy_w_in, hy_short_w, hy_filt_w1, hy_filt_b1, hy_filt_w2, hy_filt_b2, hy_filt_w3, hy_filt_freq, hy_skip, at_w_in, at_q_norm_g, at_k_norm_g, router_w, exp_w_gate, exp_w_up, exp_w_down):
    B, T, D = x.shape
    depth = mix_norm_g.shape[0]
    C = hy_skip.shape[-1]
    tm = min(512, T)
    cb = 16
    n1 = T // LANES
    h = x.reshape(B * T, D)
    row = lambda v: v.reshape(1, -1).astype(F32)
    lanes = lambda v: jnp.broadcast_to(v[..., None, None], v.shape + (1, LANES)).astype(F32)
    max_decay = math.log(HY_DECAY_TARGET) / HY_FAST_PCT
    min_decay = math.log(HY_DECAY_TARGET) / HY_SLOW_PCT
    absdelta = jnp.asarray(np.abs(np.linspace(min_decay, max_decay, C)).astype(np.float32)).reshape(C, 1)

    for i in range(depth):
        j = i // 2
        wo = w_out[i].astype(BF16)
        wkv = w_mem_kv[i].astype(BF16)
        mkt, mv = _memkv(mem, row(mem_norm_g), wkv[:, :D_MEM].T, wkv[:, D_MEM:])
        wr = router_w[i].astype(BF16)
        route_w = (row(ffn_norm_g[i]), jnp.zeros((D, LANES), BF16).at[:, :N_EXPERTS].set(wr), wr.T)
        if i % 2 == 0:
            w_in = hy_w_in[j].astype(BF16)
            ut, cq = _hy_inproj(h, row(mix_norm_g[i]), w_in[:, :3 * C].T, w_in[:, 3 * C:], B, T, tm)
            filt = dict(
                w1t=jnp.zeros((HY_FILT, 40), F32).at[:, :hy_filt_w1.shape[1]].set(hy_filt_w1[j].T),
                b1=hy_filt_b1[j].reshape(-1, 1), w2t=hy_filt_w2[j].T, b2=hy_filt_b2[j].reshape(-1, 1),
                fr=hy_filt_freq[j].reshape(-1, 1),
                w3t=hy_filt_w3[j].T.reshape(2, 2, C, HY_FILT), absdelta=absdelta,
            )
            kfft = _hyena_filters_fft(filt, T, 2 * cb)
            ut4 = ut.reshape(B, 3 * C, n1, LANES)
            if B % 2:
                ut4 = jnp.concatenate([ut4, jnp.zeros_like(ut4[:1])], axis=0)
            zt = _hy_conv(ut4, lanes(hy_short_w[j]), lanes(hy_skip[j]), kfft, B + B % 2, T, C, cb)[:B]
            routed = _outproj(h, zt, cq, mkt, mv, wo[:C], wo[C:], *route_w, B, T, min(1024, T))
        else:
            rep = lambda v, n: jnp.tile(v.astype(F32), n).reshape(1, -1)
            qt, kz, vtz, cq = _at_inproj(h, row(mix_norm_g[i]), at_w_in[j].astype(BF16),
                                         rep(at_q_norm_g[j], GQA_GROUP * N_KV_HEADS),
                                         rep(at_k_norm_g[j], N_KV_HEADS), B, T, tm)
            main_t = _flash(qt, kz, vtz, B, T, min(2048, T), min(1024, T))
            routed = _outproj(h, main_t, cq, mkt, mv, wo[:C], wo[C:], *route_w, B, T, tm)
        h = _moe(*routed, exp_w_gate, exp_w_up, exp_w_down, i, B, T)
    return _final_norm(h, row(final_norm_g), tm).reshape(B, T, D)
```

```python
import functools
import math

import numpy as np
import jax
import jax.numpy as jnp
from jax import lax
from jax.experimental import pallas as pl
from jax.experimental.pallas import tpu as pltpu

F32 = jnp.float32
BF16 = jnp.bfloat16
I32 = jnp.int32

HEAD_DIM = 64
MEM_HEADS = 4
D_MEM = MEM_HEADS * HEAD_DIM
N_KV_HEADS = 4
GQA_GROUP = 3
GRID_W = 64
ROPE_THETA = 10000.0
ROPE_AXIS_DIM = HEAD_DIM // 2
HY_BANDS = 16
HY_FILT = 64
HY_DECAY_TARGET = 1e-2
HY_FAST_PCT = 0.3
HY_SLOW_PCT = 1.5
N_EXPERTS = 16
EC_CAPACITY_FACTOR = 2
NORM_EPS = 1e-6

LANES = 128
BF16_SUBLANES = 16
VMEM_LIMIT = 56 * 1024 * 1024
HI = lax.Precision.HIGHEST
LOG2E = 1.4426950408889634


def _cp(*sem):
    return pltpu.CompilerParams(dimension_semantics=sem, vmem_limit_bytes=VMEM_LIMIT)


def _rms(x, g):
    ms = jnp.mean(x * x, axis=-1, keepdims=True)
    return x * lax.rsqrt(ms + NORM_EPS) * g


def _dot(a, b):
    return jnp.dot(a, b, preferred_element_type=F32)


def _dot3(a, b):
    ah, bh = a.astype(BF16), b.astype(BF16)
    al, bl = (a - ah.astype(F32)).astype(BF16), (b - bh.astype(F32)).astype(BF16)
    return _dot(ah, bh) + _dot(ah, bl) + _dot(al, bh)


def _dot_nt(a, b):
    return lax.dot_general(a, b, (((1,), (1,)), ((), ())), preferred_element_type=F32)


@functools.lru_cache(maxsize=None)
def _dft_tables(T, cb):
    nf = 2 * T
    n1k = nf // LANES
    n1 = T // LANES
    k1 = np.arange(n1k)[:, None]
    a1 = 2 * np.pi * k1 * np.arange(n1)[None, :] / n1k
    a1f = 2 * np.pi * k1 * np.arange(n1k)[None, :] / n1k
    f1full = np.concatenate([np.cos(a1f), -np.sin(a1f)], axis=0)
    tw = 2 * np.pi * k1 * np.arange(LANES)[None, :] / nf
    twr, twi = np.cos(tw), -np.sin(tw)
    a2 = 2 * np.pi * np.arange(LANES)[:, None] * np.arange(LANES)[None, :] / LANES
    cr, ci = np.cos(a2), -np.sin(a2)
    w2 = np.block([[cr, ci], [-ci, cr]])
    minv = np.block([[cr, -ci], [ci, cr]])
    a3 = 2 * np.pi * np.arange(n1)[:, None] * np.arange(n1k)[None, :] / n1k
    g1c = np.block([[np.cos(a3), -np.sin(a3)], [np.sin(a3), np.cos(a3)]]) / nf
    f1c = np.block([[np.cos(a1), np.sin(a1)], [-np.sin(a1), np.cos(a1)]])
    return dict(
        f1c=f1c, f1full=f1full, w2=w2, minv=minv, g1c=g1c,
        twr_l=np.tile(twr, (1, cb)), twi_l=np.tile(twi, (1, cb)),
        twr_r=np.tile(twr, (cb, 1)), twi_r=np.tile(twi, (cb, 1)),
    )


@functools.lru_cache(maxsize=None)
def _filter_feats(T):
    t = np.linspace(0.0, 1.0, T)[None, :]
    w = (2.0 * np.pi) * np.arange(T)[None, :] / T
    bands = np.linspace(1e-4, HY_BANDS - 1, HY_BANDS)[:, None]
    feats = np.concatenate([t, np.cos(bands * w), -np.sin(bands * w)], axis=0)
    pad = np.zeros((40 - feats.shape[0], T))
    feats = np.concatenate([feats, pad], axis=0)
    rev = (T - np.arange(T)) % T
    return feats.astype(np.float32), np.stack([t, t[:, rev]]).astype(np.float32)


@functools.lru_cache(maxsize=None)
def _rope_tables(T):
    rows = T // GRID_W
    pos_row = np.repeat(np.arange(rows), GRID_W).astype(np.float64)
    pos_col = np.tile(np.arange(GRID_W), rows).astype(np.float64)
    inv = 1.0 / (ROPE_THETA ** (np.arange(0, ROPE_AXIS_DIM, 2, dtype=np.float64) / ROPE_AXIS_DIM))
    lane = np.arange(LANES)
    d = lane % HEAD_DIM
    axis = d // ROPE_AXIS_DIM
    half = (d // (ROPE_AXIS_DIM // 2)) % 2
    f = d % (ROPE_AXIS_DIM // 2)
    pos = np.where(axis[None, :] == 0, pos_row[:, None], pos_col[:, None])
    ang = pos * inv[f][None, :]
    cos2 = np.cos(ang)
    sin2 = np.where(half[None, :] == 0, -np.sin(ang), np.sin(ang))
    return cos2.astype(np.float32), sin2.astype(np.float32)


@functools.lru_cache(maxsize=None)
def _spread_table(W):
    owner = np.arange(N_EXPERTS * W) // W
    hit = (np.arange(N_EXPERTS)[:, None] == owner[None, :]).astype(np.float32)
    return np.concatenate([32.0 * hit, hit], axis=0)


@functools.lru_cache(maxsize=None)
def _topk_tables(T, tb):
    nj = T // LANES
    per = tb // LANES
    r = np.arange(nj * N_EXPERTS)
    j, e = r // N_EXPERTS, r % N_EXPERTS
    same_e = e[:, None] == e[None, :]
    blk = j // per
    same_blk = blk[:, None] == blk[None, :]
    m_all = same_e & (j[None, :] < j[:, None])
    m_w = same_e & same_blk & (j[None, :] < j[:, None])
    m_b = same_e & same_blk
    tri = np.triu(np.ones((LANES, LANES)))
    return tuple(np.asarray(m, np.float32) for m in (m_all, m_w, m_b, tri))


def _hy_inproj_kernel(h_ref, g_ref, wmt_ref, wcq_ref, ut_ref, cq_ref):
    xn = _rms(h_ref[...], g_ref[...]).astype(BF16)
    ut_ref[0] = _dot_nt(wmt_ref[...], xn)
    cq_ref[...] = _dot(xn, wcq_ref[...]).astype(BF16)


def _hy_inproj(h, g, wmt, wcq, B, T, tm):
    N, D = h.shape
    c3 = wmt.shape[0]
    nt = T // tm
    return pl.pallas_call(
        _hy_inproj_kernel,
        grid=(B, nt),
        in_specs=[
            pl.BlockSpec((tm, D), lambda b, i: (b * nt + i, 0)),
            pl.BlockSpec((1, D), lambda b, i: (0, 0)),
            pl.BlockSpec((c3, D), lambda b, i: (0, 0)),
            pl.BlockSpec((D, D_MEM), lambda b, i: (0, 0)),
        ],
        out_specs=[
            pl.BlockSpec((1, c3, tm), lambda b, i: (b, 0, i)),
            pl.BlockSpec((tm, D_MEM), lambda b, i: (b * nt + i, 0)),
        ],
        out_shape=[
            jax.ShapeDtypeStruct((B, c3, T), F32),
            jax.ShapeDtypeStruct((N, D_MEM), BF16),
        ],
        compiler_params=_cp("parallel", "parallel"),
        name="hy_inproj",
    )(h, g, wmt, wcq)


def _filt_mlp_kernel(feats_ref, w1t_ref, b1_ref, w2t_ref, b2_ref, fr_ref, h2_ref):
    fr = fr_ref[...]
    a = jnp.dot(w1t_ref[...], feats_ref[...], precision=HI, preferred_element_type=F32)
    h1 = jnp.sin(fr * (a + b1_ref[...]))
    a = jnp.dot(w2t_ref[...], h1, precision=HI, preferred_element_type=F32)
    h2_ref[...] = jnp.sin(fr * (a + b2_ref[...]))


def _filt_taps_kernel(h2_ref, w3t_ref, t_ref, dl_ref, out_ref):
    T = h2_ref.shape[2]
    dl = dl_ref[...]
    hf = _dot3(w3t_ref[0, 0], h2_ref[0]) * jnp.exp(-t_ref[0] * dl)
    hb = _dot3(w3t_ref[0, 1], h2_ref[1]) * jnp.exp(-t_ref[1] * dl)
    nrm = jnp.sum(jnp.abs(hf) + jnp.abs(hb), axis=-1, keepdims=True) + 1e-6
    inv = 1.0 / nrm
    tap0 = lax.broadcasted_iota(I32, hf.shape, 1) == 0
    out_ref[0, :, :T] = ((hf + jnp.where(tap0, hb, 0.0)) * inv).astype(BF16)
    out_ref[0, :, T:] = (jnp.where(tap0, 0.0, hb) * inv).astype(BF16)


def _fwd_fft(x3, f1s, twr, twi, w2, cb, n1k):
    rhs = jnp.concatenate([x3[c].astype(BF16) for c in range(cb)], axis=1)
    a = _dot(f1s, rhs)
    ar, ai = a[:n1k], a[n1k:]
    tr = (ar * twr - ai * twi).astype(BF16)
    ti = (ar * twi + ai * twr).astype(BF16)
    lr = jnp.concatenate([tr[:, c * LANES:(c + 1) * LANES] for c in range(cb)], axis=0)
    li = jnp.concatenate([ti[:, c * LANES:(c + 1) * LANES] for c in range(cb)], axis=0)
    return _dot(jnp.concatenate([lr, li], axis=1), w2)


def _inv_fft(y, minv, twr, twi, g1, cb, n1k):
    b = _dot(y.astype(BF16), minv)
    br, bi = b[:, :LANES], b[:, LANES:]
    pr = (br * twr + bi * twi).astype(BF16)
    pi = (bi * twr - br * twi).astype(BF16)
    top = jnp.concatenate([pr[c * n1k:(c + 1) * n1k] for c in range(cb)], axis=1)
    bot = jnp.concatenate([pi[c * n1k:(c + 1) * n1k] for c in range(cb)], axis=1)
    rhs = jnp.concatenate([top, bot], axis=0)
    return _dot(g1, rhs)


def _filt_fft_kernel(taps_ref, f1_ref, twr_ref, twi_ref, w2_ref, k_ref, *, cb, n1k):
    k_ref[0] = _fwd_fft(taps_ref[0], f1_ref[...], twr_ref[...], twi_ref[...], w2_ref[...], cb, n1k).astype(BF16)


def _hyena_filters_fft(p, T, cb):
    feats, t_rows = _filter_feats(T)
    C = p["w3t"].shape[2]
    tabs = _dft_tables(T, cb)
    n1k = 2 * T // LANES
    h2 = pl.pallas_call(
        _filt_mlp_kernel,
        out_shape=jax.ShapeDtypeStruct((HY_FILT, T), F32),
        compiler_params=pltpu.CompilerParams(vmem_limit_bytes=VMEM_LIMIT),
        name="hy_filt_mlp",
    )(jnp.asarray(feats), p["w1t"], p["b1"], p["w2t"], p["b2"], p["fr"])
    h2 = jnp.stack([h2, jnp.roll(h2[:, ::-1], 1, axis=1)])
    cbt = 64
    taps = pl.pallas_call(
        _filt_taps_kernel,
        grid=(2, C // cbt),
        in_specs=[
            pl.BlockSpec((2, HY_FILT, T), lambda o, c: (0, 0, 0)),
            pl.BlockSpec((1, 2, cbt, HY_FILT), lambda o, c: (o, 0, c, 0)),
            pl.BlockSpec((2, 1, T), lambda o, c: (0, 0, 0)),
            pl.BlockSpec((cbt, 1), lambda o, c: (c, 0)),
        ],
        out_specs=pl.BlockSpec((1, cbt, 2 * T), lambda o, c: (o, c, 0)),
        out_shape=jax.ShapeDtypeStruct((2, C, 2 * T), BF16),
        compiler_params=_cp("parallel", "parallel"),
        name="hy_filt_taps",
    )(h2, p["w3t"], jnp.asarray(t_rows), p["absdelta"])
    taps = taps.reshape(2, C, n1k, LANES)
    const = lambda shape: pl.BlockSpec(shape, lambda o, c: (0,) * len(shape))
    f32 = lambda k: jnp.asarray(tabs[k], F32)
    return pl.pallas_call(
        functools.partial(_filt_fft_kernel, cb=cb, n1k=n1k),
        grid=(2, C // cb),
        in_specs=[
            pl.BlockSpec((1, cb, n1k, LANES), lambda o, c: (o, c, 0, 0)),
            const((2 * n1k, n1k)), const((n1k, cb * LANES)), const((n1k, cb * LANES)),
            const((2 * LANES, 2 * LANES)),
        ],
        out_specs=pl.BlockSpec((1, cb * n1k, 2 * LANES), lambda o, c: (o, c, 0)),
        out_shape=jax.ShapeDtypeStruct((2, C * n1k, 2 * LANES), BF16),
        compiler_params=_cp("parallel", "parallel"),
        name="hy_filt_fft",
    )(taps, f32("f1full").astype(BF16), f32("twr_l"), f32("twi_l"), f32("w2").astype(BF16))


def _time_neighbours(x):
    rows = x.shape[0]
    lane = lax.broadcasted_iota(I32, x.shape, 1)
    r = pltpu.roll(x, 1, 1)
    rr = pltpu.roll(r, 1, 0)
    prev = jnp.where(lane == 0, rr, r)
    r2 = pltpu.roll(x, LANES - 1, 1)
    rr2 = pltpu.roll(r2, rows - 1, 0)
    nxt = jnp.where(lane == LANES - 1, rr2, r2)
    return prev, nxt


def _hy_conv_kernel(x1_ref, x2_ref, v_ref, sw1_ref, sw2_ref, swv_ref, skip_ref, k_ref,
                    f1c_ref, twrl_ref, twil_ref, w2_ref, minv_ref, twrr_ref, twir_ref,
                    g1c_ref, o_ref, *, cb, n1k, n1):
    rows = cb * n1
    shape2 = (rows, LANES)
    row = lax.broadcasted_iota(I32, shape2, 0)
    lane = lax.broadcasted_iota(I32, shape2, 1)
    first = (lane == 0) & (row % n1 == 0)
    last = (lane == LANES - 1) & (row % n1 == n1 - 1)

    def sconv(x, sw_ref):
        x = x.reshape(shape2)
        prev, nxt = _time_neighbours(x)
        prev = jnp.where(first, 0.0, prev)
        nxt = jnp.where(last, 0.0, nxt)
        w = [jnp.broadcast_to(sw_ref[j], (cb, n1, LANES)).reshape(shape2) for j in range(3)]
        return prev * w[0] + x * w[1] + nxt * w[2]

    z = [sconv(v_ref[b], swv_ref) for b in range(2)]
    gates = [[sconv(x_ref[b], sw_ref) for b in range(2)] for x_ref, sw_ref in ((x1_ref, sw1_ref), (x2_ref, sw2_ref))]
    f1c, w2 = f1c_ref[...], w2_ref[...]
    for o in range(2):
        pair = jnp.concatenate([z[0].reshape(cb, n1, LANES), z[1].reshape(cb, n1, LANES)], axis=1)
        zf = _fwd_fft(pair, f1c, twrl_ref[...], twil_ref[...], w2, cb, n1k)
        kk = k_ref[o].astype(F32)
        zr, zi = zf[:, :LANES], zf[:, LANES:]
        kr, ki = kk[:, :LANES], kk[:, LANES:]
        y = jnp.concatenate([zr * kr - zi * ki, zr * ki + zi * kr], axis=1)
        conv = _inv_fft(y, minv_ref[...], twrr_ref[...], twir_ref[...], g1c_ref[...], cb, n1k)
        skip = jnp.broadcast_to(skip_ref[o], (cb, n1, LANES)).reshape(shape2)
        for b in range(2):
            cv = conv[b * n1:(b + 1) * n1]
            cv = jnp.concatenate([cv[:, c * LANES:(c + 1) * LANES] for c in range(cb)], axis=0)
            z[b] = gates[o][b] * (cv + skip * z[b])
    for b in range(2):
        o_ref[b] = z[b].reshape(cb, n1, LANES)


def _hy_conv(ut4, sw, skip, kfft, B, T, C, cb):
    n1k, n1 = 2 * T // LANES, T // LANES
    tabs = _dft_tables(T, cb)
    nct = C // cb
    bf = lambda k: jnp.asarray(tabs[k], F32).astype(BF16)
    f32 = lambda k: jnp.asarray(tabs[k], F32)
    const = lambda shape: pl.BlockSpec(shape, lambda c, b: (0,) * len(shape))
    ublk = lambda s: pl.BlockSpec((2, cb, n1, LANES), lambda c, b, s=s: (b, s * nct + c, 0, 0))
    wblk = lambda s: pl.BlockSpec((3, cb, 1, LANES), lambda c, b, s=s: (0, s * nct + c, 0, 0))
    return pl.pallas_call(
        functools.partial(_hy_conv_kernel, cb=cb, n1k=n1k, n1=n1),
        grid=(nct, B // 2),
        in_specs=[
            ublk(0), ublk(1), ublk(2), wblk(0), wblk(1), wblk(2),
            pl.BlockSpec((2, cb, 1, LANES), lambda c, b: (0, c, 0, 0)),
            pl.BlockSpec((2, cb * n1k, 2 * LANES), lambda c, b: (0, c, 0)),
            const((2 * n1k, 2 * n1)), const((n1k, cb * LANES)), const((n1k, cb * LANES)),
            const((2 * LANES, 2 * LANES)), const((2 * LANES, 2 * LANES)),
            const((cb * n1k, LANES)), const((cb * n1k, LANES)), const((2 * n1, 2 * n1k)),
        ],
        out_specs=pl.BlockSpec((2, cb, n1, LANES), lambda c, b: (b, c, 0, 0)),
        out_shape=jax.ShapeDtypeStruct((B, C, n1, LANES), F32),
        compiler_params=_cp("parallel", "arbitrary"),
        name="hy_conv",
    )(ut4, ut4, ut4, sw, sw, sw, skip, kfft,
      bf("f1c"), f32("twr_l"), f32("twi_l"), bf("w2"), bf("minv"),
      f32("twr_r"), f32("twi_r"), bf("g1c"))


def _head_norm(x, bd, g):
    ss = _dot((x * x).astype(BF16), bd)
    return x * lax.rsqrt(ss * (1.0 / HEAD_DIM) + NORM_EPS) * g


def _rope(x, cos2, sin2):
    lane = lax.broadcasted_iota(I32, cos2.shape, 1)
    low = (lane // (ROPE_AXIS_DIM // 2)) % 2 == 0
    out = []
    for c in range(x.shape[1] // LANES):
        xc = x[:, c * LANES:(c + 1) * LANES]
        up = pltpu.roll(xc, LANES - 16, 1)
        dn = pltpu.roll(xc, 16, 1)
        out.append(xc * cos2 + jnp.where(low, up, dn) * sin2)
    return jnp.concatenate(out, axis=1)


def _at_inproj_kernel(h_ref, g_ref, w_ref, bdq_ref, bdk_ref, gq_ref, gk_ref, cos_ref, sin_ref,
                      qt_ref, kz_ref, vt_ref, cq_ref, *, dq, dk):
    xn = _rms(h_ref[...], g_ref[...]).astype(BF16)
    proj = _dot(xn, w_ref[...])
    q, k = proj[:, :dq], proj[:, dq:dq + dk]
    v, cq = proj[:, dq + dk:dq + 2 * dk], proj[:, dq + 2 * dk:]
    cos2, sin2 = cos_ref[...], sin_ref[...]
    qr = _rope(_head_norm(q, bdq_ref[...], gq_ref[...]), cos2, sin2)
    qt = (qr * (HEAD_DIM ** -0.5 * LOG2E)).T.astype(BF16)
    for p in range(dq // LANES):
        qt_ref[0, p] = qt[p * LANES:(p + 1) * LANES]
    kr = _rope(_head_norm(k, bdk_ref[...], gk_ref[...]), cos2, sin2)
    vt = v.T.astype(BF16)
    zv = jnp.zeros((HEAD_DIM, vt.shape[1]), BF16)
    lane = lax.broadcasted_iota(I32, (k.shape[0], LANES), 1)
    for kv in range(N_KV_HEADS):
        rows = vt[kv * HEAD_DIM:(kv + 1) * HEAD_DIM]
        vt_ref[0, kv, 0] = jnp.concatenate([rows, zv], axis=0)
        vt_ref[0, kv, 1] = jnp.concatenate([zv, rows], axis=0)
        pair = kr[:, (kv // 2) * LANES:(kv // 2 + 1) * LANES]
        own = jnp.where((lane < HEAD_DIM) == (kv % 2 == 0), pair, 0.0)
        other = pltpu.roll(own, HEAD_DIM, 1)
        lo, hi = (own, other) if kv % 2 == 0 else (other, own)
        kz_ref[0, kv, 0] = lo.astype(BF16)
        kz_ref[0, kv, 1] = hi.astype(BF16)
    cq_ref[...] = cq.astype(BF16)


def _at_inproj(h, g, w, gq, gk, B, T, tm):
    N, D = h.shape
    dq, dk = GQA_GROUP * N_KV_HEADS * HEAD_DIM, N_KV_HEADS * HEAD_DIM
    nt = T // tm
    cos2, sin2 = _rope_tables(T)
    bd = lambda n: jnp.asarray(np.kron(np.eye(n // HEAD_DIM), np.ones((HEAD_DIM, HEAD_DIM))), F32).astype(BF16)
    const = lambda shape: pl.BlockSpec(shape, lambda b, i: (0,) * len(shape))
    return pl.pallas_call(
        functools.partial(_at_inproj_kernel, dq=dq, dk=dk),
        grid=(B, nt),
        in_specs=[
            pl.BlockSpec((tm, D), lambda b, i: (b * nt + i, 0)),
            const((1, D)), const(w.shape), const((dq, dq)), const((dk, dk)),
            const((1, dq)), const((1, dk)),
            pl.BlockSpec((tm, LANES), lambda b, i: (i, 0)),
            pl.BlockSpec((tm, LANES), lambda b, i: (i, 0)),
        ],
        out_specs=[
            pl.BlockSpec((1, dq // LANES, LANES, tm), lambda b, i: (b, 0, 0, i)),
            pl.BlockSpec((1, N_KV_HEADS, 2, tm, LANES), lambda b, i: (b, 0, 0, i, 0)),
            pl.BlockSpec((1, N_KV_HEADS, 2, LANES, tm), lambda b, i: (b, 0, 0, 0, i)),
            pl.BlockSpec((tm, D_MEM), lambda b, i: (b * nt + i, 0)),
        ],
        out_shape=[
            jax.ShapeDtypeStruct((B, dq // LANES, LANES, T), BF16),
            jax.ShapeDtypeStruct((B, N_KV_HEADS, 2, T, LANES), BF16),
            jax.ShapeDtypeStruct((B, N_KV_HEADS, 2, LANES, T), BF16),
            jax.ShapeDtypeStruct((N, D_MEM), BF16),
        ],
        compiler_params=_cp("parallel", "parallel"),
        name="at_inproj",
    )(h, g, w, bd(dq), bd(dk), gq, gk, jnp.asarray(cos2), jnp.asarray(sin2))


def _flash_kernel(qt_ref, ka_ref, kb_ref, vta_ref, vtb_ref, o_ref, m_sc, l_sc, acc_sc, s_sc, p_sc):
    ki = pl.program_id(3)

    @pl.when(ki == 0)
    def _():
        m_sc[...] = jnp.full(m_sc.shape, -jnp.inf, F32)
        l_sc[...] = jnp.zeros(l_sc.shape, F32)
        acc_sc[...] = jnp.zeros(acc_sc.shape, F32)

    qt = qt_ref[0, 0]
    tk, tq = s_sc.shape[1], s_sc.shape[2]
    sub = 8
    ck = 2 * sub
    pv, alphas = [], []
    for idx, k_ref in enumerate((ka_ref, kb_ref)):
        s_sc[idx] = _dot(k_ref[0, 0, 0], qt)
    for idx, vt_ref in enumerate((vta_ref, vtb_ref)):
        mx = s_sc[idx, 0:sub, :]
        for c in range(1, tk // sub):
            mx = jnp.maximum(mx, s_sc[idx, c * sub:(c + 1) * sub, :])
        m_prev = m_sc[idx]
        m_new = jnp.maximum(m_prev, jnp.max(mx, axis=0, keepdims=True))
        alpha = jnp.exp2(m_prev - m_new)
        lsum = jnp.zeros((sub, tq), F32)
        for c in range(tk // ck):
            p = jnp.exp2(s_sc[idx, c * ck:(c + 1) * ck, :] - m_new)
            lsum = lsum + p[:sub] + p[sub:]
            p_sc[idx, c * ck:(c + 1) * ck, :] = p.astype(BF16)
        l_sc[idx] = alpha * l_sc[idx] + jnp.sum(lsum, axis=0, keepdims=True)
        m_sc[idx] = m_new
        pv.append(_dot(vt_ref[0, 0, 0], p_sc[idx]))
        alphas.append(alpha)
    row = lax.broadcasted_iota(I32, acc_sc.shape, 0)
    low = row < HEAD_DIM
    acc_sc[...] = acc_sc[...] * jnp.where(low, alphas[0], alphas[1]) + pv[0] + pv[1]

    @pl.when(ki == pl.num_programs(3) - 1)
    def _():
        o_ref[0] = acc_sc[...] / jnp.where(low, l_sc[0], l_sc[1])


def _flash(qt, kz, vtz, B, T, tq, tk):
    npair = qt.shape[1]
    nq, nk = T // tq, T // tk
    kv_of = lambda p, j: (2 * p + j) // GQA_GROUP
    return pl.pallas_call(
        _flash_kernel,
        grid=(B, npair, nq, nk),
        in_specs=[
            pl.BlockSpec((1, 1, LANES, tq), lambda b, p, i, k: (b, p, 0, i)),
            pl.BlockSpec((1, 1, 1, tk, LANES), lambda b, p, i, k: (b, kv_of(p, 0), 0, k, 0)),
            pl.BlockSpec((1, 1, 1, tk, LANES), lambda b, p, i, k: (b, kv_of(p, 1), 1, k, 0)),
            pl.BlockSpec((1, 1, 1, LANES, tk), lambda b, p, i, k: (b, kv_of(p, 0), 0, 0, k)),
            pl.BlockSpec((1, 1, 1, LANES, tk), lambda b, p, i, k: (b, kv_of(p, 1), 1, 0, k)),
        ],
        out_specs=pl.BlockSpec((1, LANES, tq), lambda b, p, i, k: (b, p, i)),
        out_shape=jax.ShapeDtypeStruct((B, npair * LANES, T), F32),
        scratch_shapes=[
            pltpu.VMEM((2, 1, tq), F32), pltpu.VMEM((2, 1, tq), F32), pltpu.VMEM((LANES, tq), F32),
            pltpu.VMEM((2, tk, tq), F32), pltpu.VMEM((2, tk, tq), BF16),
        ],
        compiler_params=_cp("parallel", "parallel", "parallel", "arbitrary"),
        name="flash_gqa",
    )(qt, kz, kz, vtz, vtz)


def _memkv_kernel(mem_ref, g_ref, wkvt_ref, wv_ref, mkt_ref, mv_ref):
    mn = _rms(mem_ref[0], g_ref[...]).astype(BF16)
    kt = _dot_nt(wkvt_ref[...], mn) * (HEAD_DIM ** -0.5)
    v = _dot(mn, wv_ref[...])
    row = lax.broadcasted_iota(I32, kt.shape, 0)
    col = lax.broadcasted_iota(I32, v.shape, 1)
    for hd in range(MEM_HEADS):
        mkt_ref[0, hd] = jnp.where(row // HEAD_DIM == hd, kt, 0.0).astype(BF16)
        mv_ref[0, hd] = jnp.where(col // HEAD_DIM == hd, v, 0.0).astype(BF16)


def _memkv(mem, g, wkt, wv):
    B, M, D = mem.shape
    return pl.pallas_call(
        _memkv_kernel,
        grid=(B,),
        in_specs=[
            pl.BlockSpec((1, M, D), lambda b: (b, 0, 0)),
            pl.BlockSpec((1, D), lambda b: (0, 0)),
            pl.BlockSpec((D_MEM, D), lambda b: (0, 0)),
            pl.BlockSpec((D, D_MEM), lambda b: (0, 0)),
        ],
        out_specs=[
            pl.BlockSpec((1, MEM_HEADS, D_MEM, M), lambda b: (b, 0, 0, 0)),
            pl.BlockSpec((1, MEM_HEADS, M, D_MEM), lambda b: (b, 0, 0, 0)),
        ],
        out_shape=[
            jax.ShapeDtypeStruct((B, MEM_HEADS, D_MEM, M), BF16),
            jax.ShapeDtypeStruct((B, MEM_HEADS, M, D_MEM), BF16),
        ],
        compiler_params=_cp("parallel"),
        name="mem_kv",
    )(mem, g, wkt, wv)


def _cross_attn(cq, mkt_ref, mv_ref):
    acc = jnp.zeros(cq.shape, F32)
    for hd in range(MEM_HEADS):
        s = _dot(cq, mkt_ref[0, hd])
        p = jnp.exp(s - jnp.max(s, axis=-1, keepdims=True))
        p = p / jnp.sum(p, axis=-1, keepdims=True)
        acc = acc + _dot(p.astype(BF16), mv_ref[0, hd])
    return acc


def _route(h, g_ref, wr_ref, wrt_ref, xn_ref, gsp_ref, afft_ref):
    xn = _rms(h, g_ref[...]).astype(BF16)
    xn_ref[...] = xn
    lg = _dot(xn, wr_ref[...])
    lane = lax.broadcasted_iota(I32, lg.shape, 1)
    lg = jnp.where(lane < N_EXPERTS, lg, -jnp.inf)
    p = jnp.exp(lg - jnp.max(lg, axis=-1, keepdims=True))
    aff = p / jnp.sum(p, axis=-1, keepdims=True)
    hi = aff.astype(BF16).astype(F32)
    mid = (aff - hi).astype(BF16).astype(F32)
    lo = (aff - hi - mid).astype(BF16).astype(F32)
    gsp_ref[...] = (hi + pltpu.roll(mid, N_EXPERTS, 1) + pltpu.roll(lo, 2 * N_EXPERTS, 1)).astype(BF16)
    lt = _dot_nt(wrt_ref[...], xn)
    pt = jnp.exp(lt - jnp.max(lt, axis=0, keepdims=True))
    pt = pt / jnp.sum(pt, axis=0, keepdims=True)
    for j in range(lt.shape[1] // LANES):
        afft_ref[0, j] = pt[:, j * LANES:(j + 1) * LANES]


def _outproj_kernel(h_ref, main_ref, cq_ref, mkt_ref, mv_ref, wm_ref, wc_ref, g_ref, wr_ref, wrt_ref,
                    o_ref, xn_ref, gsp_ref, afft_ref):
    if len(main_ref.shape) == 3:
        main = main_ref[0].T
    else:
        main = jnp.concatenate([main_ref[0, :, j, :].T for j in range(main_ref.shape[2])], axis=0)
    main = main.astype(BF16)
    cross = _cross_attn(cq_ref[...], mkt_ref, mv_ref).astype(BF16)
    h_new = h_ref[...] + _dot(main, wm_ref[...]) + _dot(cross, wc_ref[...])
    o_ref[...] = h_new
    _route(h_new, g_ref, wr_ref, wrt_ref, xn_ref, gsp_ref, afft_ref)


def _outproj(h, main, cq, mkt, mv, wm, wc, g, wr, wrt, B, T, tm):
    N, D = h.shape
    C = wm.shape[0]
    M = mkt.shape[-1]
    nt = T // tm
    if main.ndim == 3:
        mspec = pl.BlockSpec((1, C, tm), lambda b, i: (b, 0, i))
    else:
        mspec = pl.BlockSpec((1, C, tm // LANES, LANES), lambda b, i: (b, 0, i, 0))
    return pl.pallas_call(
        _outproj_kernel,
        grid=(B, nt),
        in_specs=[
            pl.BlockSpec((tm, D), lambda b, i: (b * nt + i, 0)),
            mspec,
            pl.BlockSpec((tm, D_MEM), lambda b, i: (b * nt + i, 0)),
            pl.BlockSpec((1, MEM_HEADS, D_MEM, M), lambda b, i: (b, 0, 0, 0)),
            pl.BlockSpec((1, MEM_HEADS, M, D_MEM), lambda b, i: (b, 0, 0, 0)),
            pl.BlockSpec((C, D), lambda b, i: (0, 0)),
            pl.BlockSpec((D_MEM, D), lambda b, i: (0, 0)),
            pl.BlockSpec((1, D), lambda b, i: (0, 0)),
            pl.BlockSpec((D, LANES), lambda b, i: (0, 0)),
            pl.BlockSpec((N_EXPERTS, D), lambda b, i: (0, 0)),
        ],
        out_specs=[
            pl.BlockSpec((tm, D), lambda b, i: (b * nt + i, 0)),
            pl.BlockSpec((tm, D), lambda b, i: (b * nt + i, 0)),
            pl.BlockSpec((tm, LANES), lambda b, i: (b * nt + i, 0)),
            pl.BlockSpec((1, tm // LANES, N_EXPERTS, LANES), lambda b, i: (b, i, 0, 0)),
        ],
        out_shape=[
            jax.ShapeDtypeStruct((N, D), F32),
            jax.ShapeDtypeStruct((N, D), BF16),
            jax.ShapeDtypeStruct((N, LANES), BF16),
            jax.ShapeDtypeStruct((B, T // LANES, N_EXPERTS, LANES), F32),
        ],
        compiler_params=_cp("parallel", "parallel"),
        name="outproj",
    )(h, main, cq, mkt, mv, wm, wc, g, wr, wrt)


def _topk_kernel(aff_ref, mall_ref, mw_ref, mb_ref, tri_ref, pos_ref, aoff_ref, cnt_ref, *, cap, nj):
    E = N_EXPERTS
    aff3 = aff_ref[0]

    def count(mask3):
        per = jnp.sum(mask3.astype(F32), axis=0)
        return jnp.broadcast_to(jnp.sum(per, axis=-1, keepdims=True), (E, LANES))

    def step(i, thr):
        cand = thr | lax.shift_left(jnp.int32(1), 30 - i)
        ok = count(aff3 >= pltpu.bitcast(cand, F32)[None]) >= cap
        return jnp.where(ok, cand, thr)

    thr = lax.fori_loop(0, 31, step, jnp.zeros((E, LANES), I32))
    thr = pltpu.bitcast(thr, F32)
    gt3 = aff3 > thr[None]
    eq3 = aff3 == thr[None]
    need = cap - count(gt3)

    ones = jnp.ones((LANES, LANES), BF16)
    tri = tri_ref[...]

    def prefix(mask2):
        mb = mask2.astype(BF16)
        incl = _dot(mb, tri)
        tot = _dot(mb, ones)
        return incl - mask2, tot

    eq2 = eq3.reshape(nj * E, LANES).astype(F32)
    ex, tot = prefix(eq2)
    eq_rank = ex + _dot(mall_ref[...], tot.astype(BF16))
    need2 = jnp.broadcast_to(need[None], (nj, E, LANES)).reshape(nj * E, LANES)
    sel = jnp.where((gt3.reshape(nj * E, LANES)) | ((eq2 > 0) & (eq_rank < need2)), 1.0, 0.0)

    ex, tot = prefix(sel)
    totb = tot.astype(BF16)
    before = _dot(mall_ref[...], totb)
    within = _dot(mw_ref[...], totb)
    cnt = _dot(mb_ref[...], totb)
    pos_ref[0] = jnp.where(sel > 0, (before + ex).astype(I32), -1).reshape(nj, E, LANES)
    aoff_ref[0] = (before - within).astype(I32).reshape(nj, E, LANES)
    cnt_ref[0] = cnt.astype(I32).reshape(nj, E, LANES)


def _topk(aff4, T, tb):
    B, nj = aff4.shape[0], aff4.shape[1]
    cap = EC_CAPACITY_FACTOR * T // N_EXPERTS
    tabs = [jnp.asarray(m, F32).astype(BF16) for m in _topk_tables(T, tb)]
    R = nj * N_EXPERTS
    blk = pl.BlockSpec((1, nj, N_EXPERTS, LANES), lambda b: (b, 0, 0, 0))
    const = lambda shape: pl.BlockSpec(shape, lambda b: (0,) * len(shape))
    out = jax.ShapeDtypeStruct((B, nj, N_EXPERTS, LANES), I32)
    return pl.pallas_call(
        functools.partial(_topk_kernel, cap=cap, nj=nj),
        grid=(B,),
        in_specs=[blk, const((R, R)), const((R, R)), const((R, R)), const((LANES, LANES))],
        out_specs=[blk, blk, blk],
        out_shape=[out, out, out],
        compiler_params=_cp("parallel"),
        name="moe_topk",
    )(aff4, *tabs)


def _compress_kernel(aoff_ref, nch_ref, nmax_ref, xn_ref, gsp_ref, pos_ref, xg_ref, gs_ref,
                     *, nblk, tb, W, eg, nsub):
    b, grp, blk = pl.program_id(0), pl.program_id(1), pl.program_id(2)

    @pl.when(blk == 0)
    def _():
        xg_ref[...] = jnp.zeros(xg_ref.shape, BF16)
        gs_ref[...] = jnp.zeros(gs_ref.shape, F32)

    iota_s = lax.broadcasted_iota(I32, (W, LANES), 0)
    per = tb // LANES
    for sb in range(nsub):
        tblk = blk * nsub + sb
        base = (b * nblk + tblk) * N_EXPERTS + grp * eg
        tok = slice(sb * tb, (sb + 1) * tb)

        def chunk(c, carry, base=base, tok=tok, sb=sb):
            pieces = []
            for i in range(eg):
                a = aoff_ref[base + i] + c * W
                e = grp * eg + i
                g = [pos_ref[0, sb * per + jj, pl.ds(e, 1), :] - a == iota_s for jj in range(per)]
                pieces.append(jnp.where(jnp.concatenate(g, axis=1), 1.0, 0.0).astype(BF16))
            lhs = jnp.concatenate(pieces, axis=0)
            res = _dot(lhs, xn_ref[tok, :])
            resg = _dot(lhs, gsp_ref[tok, :])
            for i in range(eg):
                @pl.when(c < nch_ref[base + i])
                def _():
                    win = pl.ds(pl.multiple_of(aoff_ref[base + i] + c * W, BF16_SUBLANES), W)
                    xg_ref[0, i, win, :] = (xg_ref[0, i, win, :] + res[i * W:(i + 1) * W]).astype(BF16)
                    gs_ref[0, i, win, :] = gs_ref[0, i, win, :] + resg[i * W:(i + 1) * W]
            return carry

        lax.fori_loop(0, nmax_ref[b * nblk + tblk], chunk, 0)


def _compress(aoff, nch, nmax, xn, gsp, pos4, B, T, tb, rows, W):
    N, D = xn.shape
    nblk = T // tb
    eg = 4
    nsub = 2 if nblk % 2 == 0 else 1
    ns, ts = nblk // nsub, nsub * tb
    return pl.pallas_call(
        functools.partial(_compress_kernel, nblk=nblk, tb=tb, W=W, eg=eg, nsub=nsub),
        grid_spec=pltpu.PrefetchScalarGridSpec(
            num_scalar_prefetch=3,
            grid=(B, N_EXPERTS // eg, ns),
            in_specs=[
                pl.BlockSpec((ts, D), lambda b, g, k, *_: (b * ns + k, 0)),
                pl.BlockSpec((ts, LANES), lambda b, g, k, *_: (b * ns + k, 0)),
                pl.BlockSpec((1, ts // LANES, N_EXPERTS, LANES), lambda b, g, k, *_: (b, k, 0, 0)),
            ],
            out_specs=[
                pl.BlockSpec((1, eg, rows, D), lambda b, g, k, *_: (b, g, 0, 0)),
                pl.BlockSpec((1, eg, rows, LANES), lambda b, g, k, *_: (b, g, 0, 0)),
            ],
        ),
        out_shape=[
            jax.ShapeDtypeStruct((B, N_EXPERTS, rows, D), BF16),
            jax.ShapeDtypeStruct((B, N_EXPERTS, rows, LANES), F32),
        ],
        compiler_params=_cp("parallel", "parallel", "arbitrary"),
        name="moe_compress",
    )(aoff, nch, nmax, xn, gsp, pos4)


def _ffn_kernel(x_ref, gs_ref, wg_ref, wu_ref, wd_ref, y_ref, *, live):
    e = pl.program_id(0)
    x = x_ref[0, 0, :live]
    g = _dot(x, wg_ref[0, 0].astype(BF16))
    u = _dot(x, wu_ref[0, 0].astype(BF16))
    hid = (g * (1.0 / (1.0 + jnp.exp(-g))) * u).astype(BF16)
    gs = gs_ref[0, 0, :live]
    lane = lax.broadcasted_iota(I32, gs.shape, 1)
    mine = (lane == e) | (lane == e + N_EXPERTS) | (lane == e + 2 * N_EXPERTS)
    gate = jnp.sum(jnp.where(mine, gs, 0.0), axis=-1, keepdims=True)
    y_ref[0, 0, :live] = (_dot(hid, wd_ref[0, 0].astype(BF16)) * gate).astype(BF16)
    y_ref[0, 0, live:] = jnp.zeros((y_ref.shape[2] - live, y_ref.shape[3]), BF16)


def _ffn(xg, gs, wg, wu, wd, layer, live):
    B, E, rows, D = xg.shape
    F = wg.shape[-1]
    return pl.pallas_call(
        functools.partial(_ffn_kernel, live=live),
        grid=(E, B),
        in_specs=[
            pl.BlockSpec((1, 1, rows, D), lambda e, b: (b, e, 0, 0)),
            pl.BlockSpec((1, 1, rows, LANES), lambda e, b: (b, e, 0, 0)),
            pl.BlockSpec((1, 1, D, F), lambda e, b: (layer, e, 0, 0)),
            pl.BlockSpec((1, 1, D, F), lambda e, b: (layer, e, 0, 0)),
            pl.BlockSpec((1, 1, F, D), lambda e, b: (layer, e, 0, 0)),
        ],
        out_specs=pl.BlockSpec((1, 1, rows, D), lambda e, b: (b, e, 0, 0)),
        out_shape=jax.ShapeDtypeStruct((B, E, rows, D), BF16),
        compiler_params=_cp("parallel", "arbitrary"),
        name="moe_ffn",
    )(xg, gs, wg, wu, wd)


def _expand_kernel(aoff_ref, nch_ref, nmax_ref, h_ref, post_ref, spread_ref, slot1_ref, y_ref, o_ref,
                   *, nblk, tb, W, rows):
    b, blk = pl.program_id(0), pl.program_id(2)
    base = (b * nblk + blk) * N_EXPERTS
    ngrp = N_EXPERTS * W // LANES
    lane = lax.broadcasted_iota(I32, (1, LANES), 1)
    rel = _dot(post_ref[...], spread_ref[...]) - slot1_ref[...]
    never = jnp.float32(-2.0 ** 20)

    def chunk(c, acc):
        tgt, wins = [], []
        for e in range(N_EXPERTS):
            a = aoff_ref[base + e] + c * W
            tgt.append(jnp.where(c < nch_ref[base + e], a.astype(F32), never))
            a_in = pl.multiple_of(jnp.minimum(a, rows - W), BF16_SUBLANES)
            wins.append(y_ref[0, e, pl.ds(a_in, W), :])
        cols = []
        for g in range(ngrp):
            first, last = (LANES * g) // W, (LANES * g + LANES - 1) // W
            t = jnp.full((1, LANES), tgt[last], F32)
            for e in range(last - 1, first - 1, -1):
                t = jnp.where(lane < (e + 1) * W - LANES * g, tgt[e], t)
            hit = rel[:, g * LANES:(g + 1) * LANES] == t
            cols.append(jnp.where(hit, 1.0, 0.0).astype(BF16))
        return acc + _dot(jnp.concatenate(cols, axis=1), jnp.concatenate(wins, axis=0))

    acc = lax.fori_loop(0, nmax_ref[b * nblk + blk], chunk, jnp.zeros(o_ref.shape, F32))
    o_ref[...] = h_ref[...] + acc


def _expand(aoff, nch, nmax, h, post, y, B, T, tb, dw, W):
    N, D = h.shape
    rows = y.shape[2]
    nblk = T // tb
    return pl.pallas_call(
        functools.partial(_expand_kernel, nblk=nblk, tb=tb, W=W, rows=rows),
        grid_spec=pltpu.PrefetchScalarGridSpec(
            num_scalar_prefetch=3,
            grid=(B, D // dw, nblk),
            in_specs=[
                pl.BlockSpec((tb, dw), lambda b, d, k, *_: (b * nblk + k, d)),
                pl.BlockSpec((tb, 2 * N_EXPERTS), lambda b, d, k, *_: (b * nblk + k, 0)),
                pl.BlockSpec((2 * N_EXPERTS, N_EXPERTS * W), lambda b, d, k, *_: (0, 0)),
                pl.BlockSpec((1, N_EXPERTS * W), lambda b, d, k, *_: (0, 0)),
                pl.BlockSpec((1, N_EXPERTS, rows, dw), lambda b, d, k, *_: (b, 0, 0, d),
                             pipeline_mode=pl.Buffered(1)),
            ],
            out_specs=pl.BlockSpec((tb, dw), lambda b, d, k, *_: (b * nblk + k, d)),
        ),
        out_shape=jax.ShapeDtypeStruct((N, D), F32),
        compiler_params=_cp("parallel", "parallel", "arbitrary"),
        name="moe_expand",
    )(aoff, nch, nmax, h, post, jnp.asarray(_spread_table(W), F32).astype(BF16),
      jnp.asarray((np.arange(N_EXPERTS * W) % W + 1).reshape(1, -1), F32), y)


def _moe(h, xn, gsp, aff4, wg, wu, wd, layer, B, T):
    N, D = h.shape
    tb = min(512, T)
    W = 96
    nblk = T // tb
    cap = EC_CAPACITY_FACTOR * T // N_EXPERTS
    live = -(-cap // BF16_SUBLANES) * BF16_SUBLANES
    rows = -(-(live + W) // LANES) * LANES
    dw = 256
    pos4, aoff4, cnt4 = _topk(aff4, T, tb)
    per = tb // LANES
    first = aoff4[:, ::per, :, 0]
    start = first & ~(BF16_SUBLANES - 1)
    aoff = start.reshape(-1)
    nch4 = (first - start + cnt4[:, ::per, :, 0] + (W - 1)) // W
    nch = nch4.reshape(-1)
    nmax = jnp.max(nch4, axis=-1).reshape(-1)
    post = jnp.transpose(pos4, (0, 1, 3, 2)).reshape(N, N_EXPERTS) + 1
    post = jnp.concatenate([post >> 5, post & 31], axis=1).astype(BF16)
    xg, gs = _compress(aoff, nch, nmax, xn, gsp, pos4, B, T, tb, rows, W)
    y = _ffn(xg, gs, wg, wu, wd, layer, live)
    return _expand(aoff, nch, nmax, h, post, y, B, T, tb, 2 * dw, W)


def _final_norm_kernel(h_ref, g_ref, o_ref):
    o_ref[...] = _rms(h_ref[...], g_ref[...])


def _final_norm(h, g, tm):
    N, D = h.shape
    return pl.pallas_call(
        _final_norm_kernel,
        grid=(N // tm,),
        in_specs=[pl.BlockSpec((tm, D), lambda i: (i, 0)), pl.BlockSpec((1, D), lambda i: (0, 0))],
        out_specs=pl.BlockSpec((tm, D), lambda i: (i, 0)),
        out_shape=jax.ShapeDtypeStruct((N, D), F32),
        compiler_params=_cp("parallel"),
        name="final_norm",
    )(h, g)


def kernel(x, mem, mix_norm_g, ffn_norm_g, mem_norm_g, final_norm_g, w_mem_kv, w_out, hy_w_in, hy_short_w, hy_filt_w1, hy_filt_b1, hy_filt_w2, hy_filt_b2, hy_filt_w3, hy_filt_freq, hy_skip, at_w_in, at_q_norm_g, at_k_norm_g, router_w, exp_w_gate, exp_w_up, exp_w_down):
    B, T, D = x.shape
    depth = mix_norm_g.shape[0]
    C = hy_skip.shape[-1]
    tm = min(512, T)
    cb = 16
    n1 = T // LANES
    h = x.reshape(B * T, D)
    row = lambda v: v.reshape(1, -1).astype(F32)
    lanes = lambda v: jnp.broadcast_to(v[..., None, None], v.shape + (1, LANES)).astype(F32)
    max_decay = math.log(HY_DECAY_TARGET) / HY_FAST_PCT
    min_decay = math.log(HY_DECAY_TARGET) / HY_SLOW_PCT
    absdelta = jnp.asarray(np.abs(np.linspace(min_decay, max_decay, C)).astype(np.float32)).reshape(C, 1)

    for i in range(depth):
        j = i // 2
        wo = w_out[i].astype(BF16)
        wkv = w_mem_kv[i].astype(BF16)
        mkt, mv = _memkv(mem, row(mem_norm_g), wkv[:, :D_MEM].T, wkv[:, D_MEM:])
        wr = router_w[i].astype(BF16)
        route_w = (row(ffn_norm_g[i]), jnp.zeros((D, LANES), BF16).at[:, :N_EXPERTS].set(wr), wr.T)
        if i % 2 == 0:
            w_in = hy_w_in[j].astype(BF16)
            ut, cq = _hy_inproj(h, row(mix_norm_g[i]), w_in[:, :3 * C].T, w_in[:, 3 * C:], B, T, tm)
            filt = dict(
                w1t=jnp.zeros((HY_FILT, 40), F32).at[:, :hy_filt_w1.shape[1]].set(hy_filt_w1[j].T),
                b1=hy_filt_b1[j].reshape(-1, 1), w2t=hy_filt_w2[j].T, b2=hy_filt_b2[j].reshape(-1, 1),
                fr=hy_filt_freq[j].reshape(-1, 1),
                w3t=hy_filt_w3[j].T.reshape(2, 2, C, HY_FILT), absdelta=absdelta,
            )
            kfft = _hyena_filters_fft(filt, T, 2 * cb)
            ut4 = ut.reshape(B, 3 * C, n1, LANES)
            if B % 2:
                ut4 = jnp.concatenate([ut4, jnp.zeros_like(ut4[:1])], axis=0)
            zt = _hy_conv(ut4, lanes(hy_short_w[j]), lanes(hy_skip[j]), kfft, B + B % 2, T, C, cb)[:B]
            routed = _outproj(h, zt, cq, mkt, mv, wo[:C], wo[C:], *route_w, B, T, min(1024, T))
        else:
            rep = lambda v, n: jnp.tile(v.astype(F32), n).reshape(1, -1)
            qt, kz, vtz, cq = _at_inproj(h, row(mix_norm_g[i]), at_w_in[j].astype(BF16),
                                         rep(at_q_norm_g[j], GQA_GROUP * N_KV_HEADS),
                                         rep(at_k_norm_g[j], N_KV_HEADS), B, T, tm)
            main_t = _flash(qt, kz, vtz, B, T, min(2048, T), min(1024, T))
            routed = _outproj(h, main_t, cq, mkt, mv, wo[:C], wo[C:], *route_w, B, T, tm)
        h = _moe(*routed, exp_w_gate, exp_w_up, exp_w_down, i, B, T)
    return _final_norm(h, row(final_norm_g), tm).reshape(B, T, D)
```

```python
import functools
import math

import numpy as np
import jax
import jax.numpy as jnp
from jax import lax
from jax.experimental import pallas as pl
from jax.experimental.pallas import tpu as pltpu

F32 = jnp.float32
BF16 = jnp.bfloat16
I32 = jnp.int32

HEAD_DIM = 64
MEM_HEADS = 4
D_MEM = MEM_HEADS * HEAD_DIM
N_KV_HEADS = 4
GQA_GROUP = 3
GRID_W = 64
ROPE_THETA = 10000.0
ROPE_AXIS_DIM = HEAD_DIM // 2
HY_BANDS = 16
HY_FILT = 64
HY_DECAY_TARGET = 1e-2
HY_FAST_PCT = 0.3
HY_SLOW_PCT = 1.5
N_EXPERTS = 16
EC_CAPACITY_FACTOR = 2
NORM_EPS = 1e-6

LANES = 128
BF16_SUBLANES = 16
VMEM_LIMIT = 56 * 1024 * 1024
HI = lax.Precision.HIGHEST
LOG2E = 1.4426950408889634


def _cp(*sem):
    return pltpu.CompilerParams(dimension_semantics=sem, vmem_limit_bytes=VMEM_LIMIT)


def _rms(x, g):
    ms = jnp.mean(x * x, axis=-1, keepdims=True)
    return x * lax.rsqrt(ms + NORM_EPS) * g


def _dot(a, b):
    return jnp.dot(a, b, preferred_element_type=F32)


def _dot3(a, b):
    ah, bh = a.astype(BF16), b.astype(BF16)
    al, bl = (a - ah.astype(F32)).astype(BF16), (b - bh.astype(F32)).astype(BF16)
    return _dot(ah, bh) + _dot(ah, bl) + _dot(al, bh)


def _dot_nt(a, b):
    return lax.dot_general(a, b, (((1,), (1,)), ((), ())), preferred_element_type=F32)


@functools.lru_cache(maxsize=None)
def _dft_tables(T, cb):
    nf = 2 * T
    n1k = nf // LANES
    n1 = T // LANES
    k1 = np.arange(n1k)[:, None]
    a1 = 2 * np.pi * k1 * np.arange(n1)[None, :] / n1k
    a1f = 2 * np.pi * k1 * np.arange(n1k)[None, :] / n1k
    f1full = np.concatenate([np.cos(a1f), -np.sin(a1f)], axis=0)
    tw = 2 * np.pi * k1 * np.arange(LANES)[None, :] / nf
    twr, twi = np.cos(tw), -np.sin(tw)
    a2 = 2 * np.pi * np.arange(LANES)[:, None] * np.arange(LANES)[None, :] / LANES
    cr, ci = np.cos(a2), -np.sin(a2)
    w2 = np.block([[cr, ci], [-ci, cr]])
    minv = np.block([[cr, -ci], [ci, cr]])
    a3 = 2 * np.pi * np.arange(n1)[:, None] * np.arange(n1k)[None, :] / n1k
    g1c = np.block([[np.cos(a3), -np.sin(a3)], [np.sin(a3), np.cos(a3)]]) / nf
    f1c = np.block([[np.cos(a1), np.sin(a1)], [-np.sin(a1), np.cos(a1)]])
    return dict(
        f1c=f1c, f1full=f1full, w2=w2, minv=minv, g1c=g1c,
        twr_l=np.tile(twr, (1, cb)), twi_l=np.tile(twi, (1, cb)),
        twr_r=np.tile(twr, (cb, 1)), twi_r=np.tile(twi, (cb, 1)),
    )


@functools.lru_cache(maxsize=None)
def _filter_feats(T):
    t = np.linspace(0.0, 1.0, T)[None, :]
    w = (2.0 * np.pi) * np.arange(T)[None, :] / T
    bands = np.linspace(1e-4, HY_BANDS - 1, HY_BANDS)[:, None]
    feats = np.concatenate([t, np.cos(bands * w), -np.sin(bands * w)], axis=0)
    pad = np.zeros((40 - feats.shape[0], T))
    feats = np.concatenate([feats, pad], axis=0)
    rev = (T - np.arange(T)) % T
    return (np.stack([feats, feats[:, rev]]).astype(np.float32),
            np.stack([t, t[:, rev]]).astype(np.float32))


@functools.lru_cache(maxsize=None)
def _rope_tables(T):
    rows = T // GRID_W
    pos_row = np.repeat(np.arange(rows), GRID_W).astype(np.float64)
    pos_col = np.tile(np.arange(GRID_W), rows).astype(np.float64)
    inv = 1.0 / (ROPE_THETA ** (np.arange(0, ROPE_AXIS_DIM, 2, dtype=np.float64) / ROPE_AXIS_DIM))
    lane = np.arange(LANES)
    d = lane % HEAD_DIM
    axis = d // ROPE_AXIS_DIM
    half = (d // (ROPE_AXIS_DIM // 2)) % 2
    f = d % (ROPE_AXIS_DIM // 2)
    pos = np.where(axis[None, :] == 0, pos_row[:, None], pos_col[:, None])
    ang = pos * inv[f][None, :]
    cos2 = np.cos(ang)
    sin2 = np.where(half[None, :] == 0, -np.sin(ang), np.sin(ang))
    return cos2.astype(np.float32), sin2.astype(np.float32)


@functools.lru_cache(maxsize=None)
def _spread_table(W):
    owner = np.arange(N_EXPERTS * W) // W
    hit = (np.arange(N_EXPERTS)[:, None] == owner[None, :]).astype(np.float32)
    return np.concatenate([32.0 * hit, hit], axis=0)


@functools.lru_cache(maxsize=None)
def _topk_tables(T, tb):
    nj = T // LANES
    per = tb // LANES
    r = np.arange(nj * N_EXPERTS)
    j, e = r // N_EXPERTS, r % N_EXPERTS
    same_e = e[:, None] == e[None, :]
    blk = j // per
    same_blk = blk[:, None] == blk[None, :]
    m_all = same_e & (j[None, :] < j[:, None])
    m_w = same_e & same_blk & (j[None, :] < j[:, None])
    m_b = same_e & same_blk
    tri = np.triu(np.ones((LANES, LANES)))
    return tuple(np.asarray(m, np.float32) for m in (m_all, m_w, m_b, tri))


def _hy_inproj_kernel(h_ref, g_ref, wmt_ref, wcq_ref, ut_ref, cq_ref):
    xn = _rms(h_ref[...], g_ref[...]).astype(BF16)
    ut_ref[0] = _dot_nt(wmt_ref[...], xn)
    cq_ref[...] = _dot(xn, wcq_ref[...]).astype(BF16)


def _hy_inproj(h, g, wmt, wcq, B, T, tm):
    N, D = h.shape
    c3 = wmt.shape[0]
    nt = T // tm
    return pl.pallas_call(
        _hy_inproj_kernel,
        grid=(B, nt),
        in_specs=[
            pl.BlockSpec((tm, D), lambda b, i: (b * nt + i, 0)),
            pl.BlockSpec((1, D), lambda b, i: (0, 0)),
            pl.BlockSpec((c3, D), lambda b, i: (0, 0)),
            pl.BlockSpec((D, D_MEM), lambda b, i: (0, 0)),
        ],
        out_specs=[
            pl.BlockSpec((1, c3, tm), lambda b, i: (b, 0, i)),
            pl.BlockSpec((tm, D_MEM), lambda b, i: (b * nt + i, 0)),
        ],
        out_shape=[
            jax.ShapeDtypeStruct((B, c3, T), F32),
            jax.ShapeDtypeStruct((N, D_MEM), BF16),
        ],
        compiler_params=_cp("parallel", "parallel"),
        name="hy_inproj",
    )(h, g, wmt, wcq)


def _filt_mlp_kernel(feats_ref, w1t_ref, b1_ref, w2t_ref, b2_ref, fr_ref, h2_ref):
    fr = fr_ref[...]
    for d in range(2):
        a = jnp.dot(w1t_ref[...], feats_ref[d], precision=HI, preferred_element_type=F32)
        h1 = jnp.sin(fr * (a + b1_ref[...]))
        a = jnp.dot(w2t_ref[...], h1, precision=HI, preferred_element_type=F32)
        h2_ref[d] = jnp.sin(fr * (a + b2_ref[...]))


def _filt_taps_kernel(h2_ref, w3t_ref, t_ref, dl_ref, out_ref):
    T = h2_ref.shape[2]
    dl = dl_ref[...]
    hf = _dot3(w3t_ref[0, 0], h2_ref[0]) * jnp.exp(-t_ref[0] * dl)
    hb = _dot3(w3t_ref[0, 1], h2_ref[1]) * jnp.exp(-t_ref[1] * dl)
    nrm = jnp.sum(jnp.abs(hf) + jnp.abs(hb), axis=-1, keepdims=True) + 1e-6
    inv = 1.0 / nrm
    tap0 = lax.broadcasted_iota(I32, hf.shape, 1) == 0
    out_ref[0, :, :T] = ((hf + jnp.where(tap0, hb, 0.0)) * inv).astype(BF16)
    out_ref[0, :, T:] = (jnp.where(tap0, 0.0, hb) * inv).astype(BF16)


def _fwd_fft(x3, f1s, twr, twi, w2, cb, n1k):
    rhs = jnp.concatenate([x3[c].astype(BF16) for c in range(cb)], axis=1)
    a = _dot(f1s, rhs)
    ar, ai = a[:n1k], a[n1k:]
    tr = (ar * twr - ai * twi).astype(BF16)
    ti = (ar * twi + ai * twr).astype(BF16)
    lr = jnp.concatenate([tr[:, c * LANES:(c + 1) * LANES] for c in range(cb)], axis=0)
    li = jnp.concatenate([ti[:, c * LANES:(c + 1) * LANES] for c in range(cb)], axis=0)
    return _dot(jnp.concatenate([lr, li], axis=1), w2)


def _inv_fft(y, minv, twr, twi, g1, cb, n1k):
    b = _dot(y.astype(BF16), minv)
    br, bi = b[:, :LANES], b[:, LANES:]
    pr = (br * twr + bi * twi).astype(BF16)
    pi = (bi * twr - br * twi).astype(BF16)
    top = jnp.concatenate([pr[c * n1k:(c + 1) * n1k] for c in range(cb)], axis=1)
    bot = jnp.concatenate([pi[c * n1k:(c + 1) * n1k] for c in range(cb)], axis=1)
    rhs = jnp.concatenate([top, bot], axis=0)
    return _dot(g1, rhs)


def _filt_fft_kernel(taps_ref, f1_ref, twr_ref, twi_ref, w2_ref, k_ref, *, cb, n1k):
    k_ref[0] = _fwd_fft(taps_ref[0], f1_ref[...], twr_ref[...], twi_ref[...], w2_ref[...], cb, n1k).astype(BF16)


def _hyena_filters_fft(p, T, cb):
    feats, t_rows = _filter_feats(T)
    C = p["w3t"].shape[2]
    tabs = _dft_tables(T, cb)
    n1k = 2 * T // LANES
    h2 = pl.pallas_call(
        _filt_mlp_kernel,
        out_shape=jax.ShapeDtypeStruct((2, HY_FILT, T), F32),
        compiler_params=pltpu.CompilerParams(vmem_limit_bytes=VMEM_LIMIT),
        name="hy_filt_mlp",
    )(jnp.asarray(feats), p["w1t"], p["b1"], p["w2t"], p["b2"], p["fr"])
    cbt = 64
    taps = pl.pallas_call(
        _filt_taps_kernel,
        grid=(2, C // cbt),
        in_specs=[
            pl.BlockSpec((2, HY_FILT, T), lambda o, c: (0, 0, 0)),
            pl.BlockSpec((1, 2, cbt, HY_FILT), lambda o, c: (o, 0, c, 0)),
            pl.BlockSpec((2, 1, T), lambda o, c: (0, 0, 0)),
            pl.BlockSpec((cbt, 1), lambda o, c: (c, 0)),
        ],
        out_specs=pl.BlockSpec((1, cbt, 2 * T), lambda o, c: (o, c, 0)),
        out_shape=jax.ShapeDtypeStruct((2, C, 2 * T), BF16),
        compiler_params=_cp("parallel", "parallel"),
        name="hy_filt_taps",
    )(h2, p["w3t"], jnp.asarray(t_rows), p["absdelta"])
    taps = taps.reshape(2, C, n1k, LANES)
    const = lambda shape: pl.BlockSpec(shape, lambda o, c: (0,) * len(shape))
    f32 = lambda k: jnp.asarray(tabs[k], F32)
    return pl.pallas_call(
        functools.partial(_filt_fft_kernel, cb=cb, n1k=n1k),
        grid=(2, C // cb),
        in_specs=[
            pl.BlockSpec((1, cb, n1k, LANES), lambda o, c: (o, c, 0, 0)),
            const((2 * n1k, n1k)), const((n1k, cb * LANES)), const((n1k, cb * LANES)),
            const((2 * LANES, 2 * LANES)),
        ],
        out_specs=pl.BlockSpec((1, cb * n1k, 2 * LANES), lambda o, c: (o, c, 0)),
        out_shape=jax.ShapeDtypeStruct((2, C * n1k, 2 * LANES), BF16),
        compiler_params=_cp("parallel", "parallel"),
        name="hy_filt_fft",
    )(taps, f32("f1full").astype(BF16), f32("twr_l"), f32("twi_l"), f32("w2").astype(BF16))


def _time_neighbours(x):
    rows = x.shape[0]
    lane = lax.broadcasted_iota(I32, x.shape, 1)
    r = pltpu.roll(x, 1, 1)
    rr = pltpu.roll(r, 1, 0)
    prev = jnp.where(lane == 0, rr, r)
    r2 = pltpu.roll(x, LANES - 1, 1)
    rr2 = pltpu.roll(r2, rows - 1, 0)
    nxt = jnp.where(lane == LANES - 1, rr2, r2)
    return prev, nxt


def _hy_conv_kernel(x1_ref, x2_ref, v_ref, sw1_ref, sw2_ref, swv_ref, skip_ref, k_ref,
                    f1c_ref, twrl_ref, twil_ref, w2_ref, minv_ref, twrr_ref, twir_ref,
                    g1c_ref, o_ref, *, cb, n1k, n1):
    rows = cb * n1
    shape2 = (rows, LANES)
    row = lax.broadcasted_iota(I32, shape2, 0)
    lane = lax.broadcasted_iota(I32, shape2, 1)
    first = (lane == 0) & (row % n1 == 0)
    last = (lane == LANES - 1) & (row % n1 == n1 - 1)

    def sconv(x, sw_ref):
        x = x.reshape(shape2)
        prev, nxt = _time_neighbours(x)
        prev = jnp.where(first, 0.0, prev)
        nxt = jnp.where(last, 0.0, nxt)
        w = [jnp.broadcast_to(sw_ref[j], (cb, n1, LANES)).reshape(shape2) for j in range(3)]
        return prev * w[0] + x * w[1] + nxt * w[2]

    z = [sconv(v_ref[b], swv_ref) for b in range(2)]
    gates = [[sconv(x_ref[b], sw_ref) for b in range(2)] for x_ref, sw_ref in ((x1_ref, sw1_ref), (x2_ref, sw2_ref))]
    f1c, w2 = f1c_ref[...], w2_ref[...]
    for o in range(2):
        pair = jnp.concatenate([z[0].reshape(cb, n1, LANES), z[1].reshape(cb, n1, LANES)], axis=1)
        zf = _fwd_fft(pair, f1c, twrl_ref[...], twil_ref[...], w2, cb, n1k)
        kk = k_ref[o].astype(F32)
        zr, zi = zf[:, :LANES], zf[:, LANES:]
        kr, ki = kk[:, :LANES], kk[:, LANES:]
        y = jnp.concatenate([zr * kr - zi * ki, zr * ki + zi * kr], axis=1)
        conv = _inv_fft(y, minv_ref[...], twrr_ref[...], twir_ref[...], g1c_ref[...], cb, n1k)
        skip = jnp.broadcast_to(skip_ref[o], (cb, n1, LANES)).reshape(shape2)
        for b in range(2):
            cv = conv[b * n1:(b + 1) * n1]
            cv = jnp.concatenate([cv[:, c * LANES:(c + 1) * LANES] for c in range(cb)], axis=0)
            z[b] = gates[o][b] * (cv + skip * z[b])
    for b in range(2):
        o_ref[b] = z[b].reshape(cb, n1, LANES)


def _hy_conv(ut4, sw, skip, kfft, B, T, C, cb):
    n1k, n1 = 2 * T // LANES, T // LANES
    tabs = _dft_tables(T, cb)
    nct = C // cb
    bf = lambda k: jnp.asarray(tabs[k], F32).astype(BF16)
    f32 = lambda k: jnp.asarray(tabs[k], F32)
    const = lambda shape: pl.BlockSpec(shape, lambda c, b: (0,) * len(shape))
    ublk = lambda s: pl.BlockSpec((2, cb, n1, LANES), lambda c, b, s=s: (b, s * nct + c, 0, 0))
    wblk = lambda s: pl.BlockSpec((3, cb, 1, LANES), lambda c, b, s=s: (0, s * nct + c, 0, 0))
    return pl.pallas_call(
        functools.partial(_hy_conv_kernel, cb=cb, n1k=n1k, n1=n1),
        grid=(nct, B // 2),
        in_specs=[
            ublk(0), ublk(1), ublk(2), wblk(0), wblk(1), wblk(2),
            pl.BlockSpec((2, cb, 1, LANES), lambda c, b: (0, c, 0, 0)),
            pl.BlockSpec((2, cb * n1k, 2 * LANES), lambda c, b: (0, c, 0)),
            const((2 * n1k, 2 * n1)), const((n1k, cb * LANES)), const((n1k, cb * LANES)),
            const((2 * LANES, 2 * LANES)), const((2 * LANES, 2 * LANES)),
            const((cb * n1k, LANES)), const((cb * n1k, LANES)), const((2 * n1, 2 * n1k)),
        ],
        out_specs=pl.BlockSpec((2, cb, n1, LANES), lambda c, b: (b, c, 0, 0)),
        out_shape=jax.ShapeDtypeStruct((B, C, n1, LANES), F32),
        compiler_params=_cp("parallel", "arbitrary"),
        name="hy_conv",
    )(ut4, ut4, ut4, sw, sw, sw, skip, kfft,
      bf("f1c"), f32("twr_l"), f32("twi_l"), bf("w2"), bf("minv"),
      f32("twr_r"), f32("twi_r"), bf("g1c"))


def _head_norm(x, bd, g):
    ss = _dot((x * x).astype(BF16), bd)
    return x * lax.rsqrt(ss * (1.0 / HEAD_DIM) + NORM_EPS) * g


def _rope(x, cos2, sin2):
    lane = lax.broadcasted_iota(I32, cos2.shape, 1)
    low = (lane // (ROPE_AXIS_DIM // 2)) % 2 == 0
    out = []
    for c in range(x.shape[1] // LANES):
        xc = x[:, c * LANES:(c + 1) * LANES]
        up = pltpu.roll(xc, LANES - 16, 1)
        dn = pltpu.roll(xc, 16, 1)
        out.append(xc * cos2 + jnp.where(low, up, dn) * sin2)
    return jnp.concatenate(out, axis=1)


def _at_inproj_kernel(h_ref, g_ref, w_ref, bdq_ref, bdk_ref, gq_ref, gk_ref, cos_ref, sin_ref,
                      qt_ref, kz_ref, vt_ref, cq_ref, *, dq, dk):
    xn = _rms(h_ref[...], g_ref[...]).astype(BF16)
    proj = _dot(xn, w_ref[...])
    q, k = proj[:, :dq], proj[:, dq:dq + dk]
    v, cq = proj[:, dq + dk:dq + 2 * dk], proj[:, dq + 2 * dk:]
    cos2, sin2 = cos_ref[...], sin_ref[...]
    qr = _rope(_head_norm(q, bdq_ref[...], gq_ref[...]), cos2, sin2)
    qt = (qr * (HEAD_DIM ** -0.5 * LOG2E)).T.astype(BF16)
    for p in range(dq // LANES):
        qt_ref[0, p] = qt[p * LANES:(p + 1) * LANES]
    kr = _rope(_head_norm(k, bdk_ref[...], gk_ref[...]), cos2, sin2)
    vt = v.T.astype(BF16)
    zv = jnp.zeros((HEAD_DIM, vt.shape[1]), BF16)
    lane = lax.broadcasted_iota(I32, (k.shape[0], LANES), 1)
    for kv in range(N_KV_HEADS):
        rows = vt[kv * HEAD_DIM:(kv + 1) * HEAD_DIM]
        vt_ref[0, kv, 0] = jnp.concatenate([rows, zv], axis=0)
        vt_ref[0, kv, 1] = jnp.concatenate([zv, rows], axis=0)
        pair = kr[:, (kv // 2) * LANES:(kv // 2 + 1) * LANES]
        own = jnp.where((lane < HEAD_DIM) == (kv % 2 == 0), pair, 0.0)
        other = pltpu.roll(own, HEAD_DIM, 1)
        lo, hi = (own, other) if kv % 2 == 0 else (other, own)
        kz_ref[0, kv, 0] = lo.astype(BF16)
        kz_ref[0, kv, 1] = hi.astype(BF16)
    cq_ref[...] = cq.astype(BF16)


def _at_inproj(h, g, w, gq, gk, B, T, tm):
    N, D = h.shape
    dq, dk = GQA_GROUP * N_KV_HEADS * HEAD_DIM, N_KV_HEADS * HEAD_DIM
    nt = T // tm
    cos2, sin2 = _rope_tables(T)
    bd = lambda n: jnp.asarray(np.kron(np.eye(n // HEAD_DIM), np.ones((HEAD_DIM, HEAD_DIM))), F32).astype(BF16)
    const = lambda shape: pl.BlockSpec(shape, lambda b, i: (0,) * len(shape))
    return pl.pallas_call(
        functools.partial(_at_inproj_kernel, dq=dq, dk=dk),
        grid=(B, nt),
        in_specs=[
            pl.BlockSpec((tm, D), lambda b, i: (b * nt + i, 0)),
            const((1, D)), const(w.shape), const((dq, dq)), const((dk, dk)),
            const((1, dq)), const((1, dk)),
            pl.BlockSpec((tm, LANES), lambda b, i: (i, 0)),
            pl.BlockSpec((tm, LANES), lambda b, i: (i, 0)),
        ],
        out_specs=[
            pl.BlockSpec((1, dq // LANES, LANES, tm), lambda b, i: (b, 0, 0, i)),
            pl.BlockSpec((1, N_KV_HEADS, 2, tm, LANES), lambda b, i: (b, 0, 0, i, 0)),
            pl.BlockSpec((1, N_KV_HEADS, 2, LANES, tm), lambda b, i: (b, 0, 0, 0, i)),
            pl.BlockSpec((tm, D_MEM), lambda b, i: (b * nt + i, 0)),
        ],
        out_shape=[
            jax.ShapeDtypeStruct((B, dq // LANES, LANES, T), BF16),
            jax.ShapeDtypeStruct((B, N_KV_HEADS, 2, T, LANES), BF16),
            jax.ShapeDtypeStruct((B, N_KV_HEADS, 2, LANES, T), BF16),
            jax.ShapeDtypeStruct((N, D_MEM), BF16),
        ],
        compiler_params=_cp("parallel", "parallel"),
        name="at_inproj",
    )(h, g, w, bd(dq), bd(dk), gq, gk, jnp.asarray(cos2), jnp.asarray(sin2))


def _flash_kernel(qt_ref, ka_ref, kb_ref, vta_ref, vtb_ref, o_ref, m_sc, l_sc, acc_sc, s_sc, p_sc):
    ki = pl.program_id(3)

    @pl.when(ki == 0)
    def _():
        m_sc[...] = jnp.full(m_sc.shape, -jnp.inf, F32)
        l_sc[...] = jnp.zeros(l_sc.shape, F32)
        acc_sc[...] = jnp.zeros(acc_sc.shape, F32)

    qt = qt_ref[0, 0]
    tk, tq = s_sc.shape[1], s_sc.shape[2]
    sub = 8
    ck = 2 * sub
    pv, alphas = [], []
    for idx, k_ref in enumerate((ka_ref, kb_ref)):
        s_sc[idx] = _dot(k_ref[0, 0, 0], qt)
    for idx, vt_ref in enumerate((vta_ref, vtb_ref)):
        mx = s_sc[idx, 0:sub, :]
        for c in range(1, tk // sub):
            mx = jnp.maximum(mx, s_sc[idx, c * sub:(c + 1) * sub, :])
        m_prev = m_sc[idx]
        m_new = jnp.maximum(m_prev, jnp.max(mx, axis=0, keepdims=True))
        alpha = jnp.exp2(m_prev - m_new)
        lsum = jnp.zeros((sub, tq), F32)
        for c in range(tk // ck):
            p = jnp.exp2(s_sc[idx, c * ck:(c + 1) * ck, :] - m_new)
            lsum = lsum + p[:sub] + p[sub:]
            p_sc[idx, c * ck:(c + 1) * ck, :] = p.astype(BF16)
        l_sc[idx] = alpha * l_sc[idx] + jnp.sum(lsum, axis=0, keepdims=True)
        m_sc[idx] = m_new
        pv.append(_dot(vt_ref[0, 0, 0], p_sc[idx]))
        alphas.append(alpha)
    row = lax.broadcasted_iota(I32, acc_sc.shape, 0)
    low = row < HEAD_DIM
    acc_sc[...] = acc_sc[...] * jnp.where(low, alphas[0], alphas[1]) + pv[0] + pv[1]

    @pl.when(ki == pl.num_programs(3) - 1)
    def _():
        o_ref[0] = acc_sc[...] / jnp.where(low, l_sc[0], l_sc[1])


def _flash(qt, kz, vtz, B, T, tq, tk):
    npair = qt.shape[1]
    nq, nk = T // tq, T // tk
    kv_of = lambda p, j: (2 * p + j) // GQA_GROUP
    return pl.pallas_call(
        _flash_kernel,
        grid=(B, npair, nq, nk),
        in_specs=[
            pl.BlockSpec((1, 1, LANES, tq), lambda b, p, i, k: (b, p, 0, i)),
            pl.BlockSpec((1, 1, 1, tk, LANES), lambda b, p, i, k: (b, kv_of(p, 0), 0, k, 0)),
            pl.BlockSpec((1, 1, 1, tk, LANES), lambda b, p, i, k: (b, kv_of(p, 1), 1, k, 0)),
            pl.BlockSpec((1, 1, 1, LANES, tk), lambda b, p, i, k: (b, kv_of(p, 0), 0, 0, k)),
            pl.BlockSpec((1, 1, 1, LANES, tk), lambda b, p, i, k: (b, kv_of(p, 1), 1, 0, k)),
        ],
        out_specs=pl.BlockSpec((1, LANES, tq), lambda b, p, i, k: (b, p, i)),
        out_shape=jax.ShapeDtypeStruct((B, npair * LANES, T), F32),
        scratch_shapes=[
            pltpu.VMEM((2, 1, tq), F32), pltpu.VMEM((2, 1, tq), F32), pltpu.VMEM((LANES, tq), F32),
            pltpu.VMEM((2, tk, tq), F32), pltpu.VMEM((2, tk, tq), BF16),
        ],
        compiler_params=_cp("parallel", "parallel", "parallel", "arbitrary"),
        name="flash_gqa",
    )(qt, kz, kz, vtz, vtz)


def _memkv_kernel(mem_ref, g_ref, wkvt_ref, wv_ref, mkt_ref, mv_ref):
    mn = _rms(mem_ref[0], g_ref[...]).astype(BF16)
    kt = _dot_nt(wkvt_ref[...], mn) * (HEAD_DIM ** -0.5)
    v = _dot(mn, wv_ref[...])
    row = lax.broadcasted_iota(I32, kt.shape, 0)
    col = lax.broadcasted_iota(I32, v.shape, 1)
    for hd in range(MEM_HEADS):
        mkt_ref[0, hd] = jnp.where(row // HEAD_DIM == hd, kt, 0.0).astype(BF16)
        mv_ref[0, hd] = jnp.where(col // HEAD_DIM == hd, v, 0.0).astype(BF16)


def _memkv(mem, g, wkt, wv):
    B, M, D = mem.shape
    return pl.pallas_call(
        _memkv_kernel,
        grid=(B,),
        in_specs=[
            pl.BlockSpec((1, M, D), lambda b: (b, 0, 0)),
            pl.BlockSpec((1, D), lambda b: (0, 0)),
            pl.BlockSpec((D_MEM, D), lambda b: (0, 0)),
            pl.BlockSpec((D, D_MEM), lambda b: (0, 0)),
        ],
        out_specs=[
            pl.BlockSpec((1, MEM_HEADS, D_MEM, M), lambda b: (b, 0, 0, 0)),
            pl.BlockSpec((1, MEM_HEADS, M, D_MEM), lambda b: (b, 0, 0, 0)),
        ],
        out_shape=[
            jax.ShapeDtypeStruct((B, MEM_HEADS, D_MEM, M), BF16),
            jax.ShapeDtypeStruct((B, MEM_HEADS, M, D_MEM), BF16),
        ],
        compiler_params=_cp("parallel"),
        name="mem_kv",
    )(mem, g, wkt, wv)


def _cross_attn(cq, mkt_ref, mv_ref):
    acc = jnp.zeros(cq.shape, F32)
    for hd in range(MEM_HEADS):
        s = _dot(cq, mkt_ref[0, hd])
        p = jnp.exp(s - jnp.max(s, axis=-1, keepdims=True))
        p = p / jnp.sum(p, axis=-1, keepdims=True)
        acc = acc + _dot(p.astype(BF16), mv_ref[0, hd])
    return acc


def _route(h, g_ref, wr_ref, wrt_ref, xn_ref, gsp_ref, afft_ref):
    xn = _rms(h, g_ref[...]).astype(BF16)
    xn_ref[...] = xn
    lg = _dot(xn, wr_ref[...])
    lane = lax.broadcasted_iota(I32, lg.shape, 1)
    lg = jnp.where(lane < N_EXPERTS, lg, -jnp.inf)
    p = jnp.exp(lg - jnp.max(lg, axis=-1, keepdims=True))
    aff = p / jnp.sum(p, axis=-1, keepdims=True)
    hi = aff.astype(BF16).astype(F32)
    mid = (aff - hi).astype(BF16).astype(F32)
    lo = (aff - hi - mid).astype(BF16).astype(F32)
    gsp_ref[...] = (hi + pltpu.roll(mid, N_EXPERTS, 1) + pltpu.roll(lo, 2 * N_EXPERTS, 1)).astype(BF16)
    lt = _dot_nt(wrt_ref[...], xn)
    pt = jnp.exp(lt - jnp.max(lt, axis=0, keepdims=True))
    pt = pt / jnp.sum(pt, axis=0, keepdims=True)
    for j in range(lt.shape[1] // LANES):
        afft_ref[0, j] = pt[:, j * LANES:(j + 1) * LANES]


def _outproj_kernel(h_ref, main_ref, cq_ref, mkt_ref, mv_ref, wm_ref, wc_ref, g_ref, wr_ref, wrt_ref,
                    o_ref, xn_ref, gsp_ref, afft_ref):
    if len(main_ref.shape) == 3:
        main = main_ref[0].T
    else:
        main = jnp.concatenate([main_ref[0, :, j, :].T for j in range(main_ref.shape[2])], axis=0)
    main = main.astype(BF16)
    cross = _cross_attn(cq_ref[...], mkt_ref, mv_ref).astype(BF16)
    h_new = h_ref[...] + _dot(main, wm_ref[...]) + _dot(cross, wc_ref[...])
    o_ref[...] = h_new
    _route(h_new, g_ref, wr_ref, wrt_ref, xn_ref, gsp_ref, afft_ref)


def _outproj(h, main, cq, mkt, mv, wm, wc, g, wr, wrt, B, T, tm):
    N, D = h.shape
    C = wm.shape[0]
    M = mkt.shape[-1]
    nt = T // tm
    if main.ndim == 3:
        mspec = pl.BlockSpec((1, C, tm), lambda b, i: (b, 0, i))
    else:
        mspec = pl.BlockSpec((1, C, tm // LANES, LANES), lambda b, i: (b, 0, i, 0))
    return pl.pallas_call(
        _outproj_kernel,
        grid=(B, nt),
        in_specs=[
            pl.BlockSpec((tm, D), lambda b, i: (b * nt + i, 0)),
            mspec,
            pl.BlockSpec((tm, D_MEM), lambda b, i: (b * nt + i, 0)),
            pl.BlockSpec((1, MEM_HEADS, D_MEM, M), lambda b, i: (b, 0, 0, 0)),
            pl.BlockSpec((1, MEM_HEADS, M, D_MEM), lambda b, i: (b, 0, 0, 0)),
            pl.BlockSpec((C, D), lambda b, i: (0, 0)),
            pl.BlockSpec((D_MEM, D), lambda b, i: (0, 0)),
            pl.BlockSpec((1, D), lambda b, i: (0, 0)),
            pl.BlockSpec((D, LANES), lambda b, i: (0, 0)),
            pl.BlockSpec((N_EXPERTS, D), lambda b, i: (0, 0)),
        ],
        out_specs=[
            pl.BlockSpec((tm, D), lambda b, i: (b * nt + i, 0)),
            pl.BlockSpec((tm, D), lambda b, i: (b * nt + i, 0)),
            pl.BlockSpec((tm, LANES), lambda b, i: (b * nt + i, 0)),
            pl.BlockSpec((1, tm // LANES, N_EXPERTS, LANES), lambda b, i: (b, i, 0, 0)),
        ],
        out_shape=[
            jax.ShapeDtypeStruct((N, D), F32),
            jax.ShapeDtypeStruct((N, D), BF16),
            jax.ShapeDtypeStruct((N, LANES), BF16),
            jax.ShapeDtypeStruct((B, T // LANES, N_EXPERTS, LANES), F32),
        ],
        compiler_params=_cp("parallel", "parallel"),
        name="outproj",
    )(h, main, cq, mkt, mv, wm, wc, g, wr, wrt)


def _topk_kernel(aff_ref, mall_ref, mw_ref, mb_ref, tri_ref, pos_ref, aoff_ref, cnt_ref, *, cap, nj):
    E = N_EXPERTS
    aff3 = aff_ref[0]

    def count(mask3):
        per = jnp.sum(mask3.astype(F32), axis=0)
        return jnp.broadcast_to(jnp.sum(per, axis=-1, keepdims=True), (E, LANES))

    def step(i, thr):
        cand = thr | lax.shift_left(jnp.int32(1), 30 - i)
        ok = count(aff3 >= pltpu.bitcast(cand, F32)[None]) >= cap
        return jnp.where(ok, cand, thr)

    thr = lax.fori_loop(0, 31, step, jnp.zeros((E, LANES), I32))
    thr = pltpu.bitcast(thr, F32)
    gt3 = aff3 > thr[None]
    eq3 = aff3 == thr[None]
    need = cap - count(gt3)

    ones = jnp.ones((LANES, LANES), BF16)
    tri = tri_ref[...]

    def prefix(mask2):
        mb = mask2.astype(BF16)
        incl = _dot(mb, tri)
        tot = _dot(mb, ones)
        return incl - mask2, tot

    eq2 = eq3.reshape(nj * E, LANES).astype(F32)
    ex, tot = prefix(eq2)
    eq_rank = ex + _dot(mall_ref[...], tot.astype(BF16))
    need2 = jnp.broadcast_to(need[None], (nj, E, LANES)).reshape(nj * E, LANES)
    sel = jnp.where((gt3.reshape(nj * E, LANES)) | ((eq2 > 0) & (eq_rank < need2)), 1.0, 0.0)

    ex, tot = prefix(sel)
    totb = tot.astype(BF16)
    before = _dot(mall_ref[...], totb)
    within = _dot(mw_ref[...], totb)
    cnt = _dot(mb_ref[...], totb)
    pos_ref[0] = jnp.where(sel > 0, (before + ex).astype(I32), -1).reshape(nj, E, LANES)
    aoff_ref[0] = (before - within).astype(I32).reshape(nj, E, LANES)
    cnt_ref[0] = cnt.astype(I32).reshape(nj, E, LANES)


def _topk(aff4, T, tb):
    B, nj = aff4.shape[0], aff4.shape[1]
    cap = EC_CAPACITY_FACTOR * T // N_EXPERTS
    tabs = [jnp.asarray(m, F32).astype(BF16) for m in _topk_tables(T, tb)]
    R = nj * N_EXPERTS
    blk = pl.BlockSpec((1, nj, N_EXPERTS, LANES), lambda b: (b, 0, 0, 0))
    const = lambda shape: pl.BlockSpec(shape, lambda b: (0,) * len(shape))
    out = jax.ShapeDtypeStruct((B, nj, N_EXPERTS, LANES), I32)
    return pl.pallas_call(
        functools.partial(_topk_kernel, cap=cap, nj=nj),
        grid=(B,),
        in_specs=[blk, const((R, R)), const((R, R)), const((R, R)), const((LANES, LANES))],
        out_specs=[blk, blk, blk],
        out_shape=[out, out, out],
        compiler_params=_cp("parallel"),
        name="moe_topk",
    )(aff4, *tabs)


def _compress_kernel(aoff_ref, nch_ref, nmax_ref, xn_ref, gsp_ref, pos_ref, xg_ref, gs_ref,
                     *, nblk, tb, W, eg, nsub):
    b, grp, blk = pl.program_id(0), pl.program_id(1), pl.program_id(2)

    @pl.when(blk == 0)
    def _():
        xg_ref[...] = jnp.zeros(xg_ref.shape, BF16)
        gs_ref[...] = jnp.zeros(gs_ref.shape, F32)

    iota_s = lax.broadcasted_iota(I32, (W, LANES), 0)
    per = tb // LANES
    for sb in range(nsub):
        tblk = blk * nsub + sb
        base = (b * nblk + tblk) * N_EXPERTS + grp * eg
        tok = slice(sb * tb, (sb + 1) * tb)

        def chunk(c, carry, base=base, tok=tok, sb=sb):
            pieces = []
            for i in range(eg):
                a = aoff_ref[base + i] + c * W
                e = grp * eg + i
                g = [pos_ref[0, sb * per + jj, pl.ds(e, 1), :] - a == iota_s for jj in range(per)]
                pieces.append(jnp.where(jnp.concatenate(g, axis=1), 1.0, 0.0).astype(BF16))
            lhs = jnp.concatenate(pieces, axis=0)
            res = _dot(lhs, xn_ref[tok, :])
            resg = _dot(lhs, gsp_ref[tok, :])
            for i in range(eg):
                @pl.when(c < nch_ref[base + i])
                def _():
                    win = pl.ds(pl.multiple_of(aoff_ref[base + i] + c * W, BF16_SUBLANES), W)
                    xg_ref[0, i, win, :] = (xg_ref[0, i, win, :] + res[i * W:(i + 1) * W]).astype(BF16)
                    gs_ref[0, i, win, :] = gs_ref[0, i, win, :] + resg[i * W:(i + 1) * W]
            return carry

        lax.fori_loop(0, nmax_ref[b * nblk + tblk], chunk, 0)


def _compress(aoff, nch, nmax, xn, gsp, pos4, B, T, tb, rows, W):
    N, D = xn.shape
    nblk = T // tb
    eg = 4
    nsub = 2 if nblk % 2 == 0 else 1
    ns, ts = nblk // nsub, nsub * tb
    return pl.pallas_call(
        functools.partial(_compress_kernel, nblk=nblk, tb=tb, W=W, eg=eg, nsub=nsub),
        grid_spec=pltpu.PrefetchScalarGridSpec(
            num_scalar_prefetch=3,
            grid=(B, N_EXPERTS // eg, ns),
            in_specs=[
                pl.BlockSpec((ts, D), lambda b, g, k, *_: (b * ns + k, 0)),
                pl.BlockSpec((ts, LANES), lambda b, g, k, *_: (b * ns + k, 0)),
                pl.BlockSpec((1, ts // LANES, N_EXPERTS, LANES), lambda b, g, k, *_: (b, k, 0, 0)),
            ],
            out_specs=[
                pl.BlockSpec((1, eg, rows, D), lambda b, g, k, *_: (b, g, 0, 0)),
                pl.BlockSpec((1, eg, rows, LANES), lambda b, g, k, *_: (b, g, 0, 0)),
            ],
        ),
        out_shape=[
            jax.ShapeDtypeStruct((B, N_EXPERTS, rows, D), BF16),
            jax.ShapeDtypeStruct((B, N_EXPERTS, rows, LANES), F32),
        ],
        compiler_params=_cp("parallel", "parallel", "arbitrary"),
        name="moe_compress",
    )(aoff, nch, nmax, xn, gsp, pos4)


def _ffn_kernel(x_ref, gs_ref, wg_ref, wu_ref, wd_ref, y_ref, *, live):
    e = pl.program_id(0)
    x = x_ref[0, 0, :live]
    g = _dot(x, wg_ref[0, 0].astype(BF16))
    u = _dot(x, wu_ref[0, 0].astype(BF16))
    hid = (g * (1.0 / (1.0 + jnp.exp(-g))) * u).astype(BF16)
    gs = gs_ref[0, 0, :live]
    lane = lax.broadcasted_iota(I32, gs.shape, 1)
    mine = (lane == e) | (lane == e + N_EXPERTS) | (lane == e + 2 * N_EXPERTS)
    gate = jnp.sum(jnp.where(mine, gs, 0.0), axis=-1, keepdims=True)
    y_ref[0, 0, :live] = (_dot(hid, wd_ref[0, 0].astype(BF16)) * gate).astype(BF16)
    y_ref[0, 0, live:] = jnp.zeros((y_ref.shape[2] - live, y_ref.shape[3]), BF16)


def _ffn(xg, gs, wg, wu, wd, layer, live):
    B, E, rows, D = xg.shape
    F = wg.shape[-1]
    return pl.pallas_call(
        functools.partial(_ffn_kernel, live=live),
        grid=(E, B),
        in_specs=[
            pl.BlockSpec((1, 1, rows, D), lambda e, b: (b, e, 0, 0)),
            pl.BlockSpec((1, 1, rows, LANES), lambda e, b: (b, e, 0, 0)),
            pl.BlockSpec((1, 1, D, F), lambda e, b: (layer, e, 0, 0)),
            pl.BlockSpec((1, 1, D, F), lambda e, b: (layer, e, 0, 0)),
            pl.BlockSpec((1, 1, F, D), lambda e, b: (layer, e, 0, 0)),
        ],
        out_specs=pl.BlockSpec((1, 1, rows, D), lambda e, b: (b, e, 0, 0)),
        out_shape=jax.ShapeDtypeStruct((B, E, rows, D), BF16),
        compiler_params=_cp("parallel", "arbitrary"),
        name="moe_ffn",
    )(xg, gs, wg, wu, wd)


def _expand_kernel(aoff_ref, nch_ref, nmax_ref, h_ref, post_ref, spread_ref, slot1_ref, y_ref, o_ref,
                   *, nblk, tb, W, rows):
    b, blk = pl.program_id(0), pl.program_id(2)
    base = (b * nblk + blk) * N_EXPERTS
    ngrp = N_EXPERTS * W // LANES
    lane = lax.broadcasted_iota(I32, (1, LANES), 1)
    rel = _dot(post_ref[...], spread_ref[...]) - slot1_ref[...]
    never = jnp.float32(-2.0 ** 20)

    def chunk(c, acc):
        tgt, wins = [], []
        for e in range(N_EXPERTS):
            a = aoff_ref[base + e] + c * W
            tgt.append(jnp.where(c < nch_ref[base + e], a.astype(F32), never))
            a_in = pl.multiple_of(jnp.minimum(a, rows - W), BF16_SUBLANES)
            wins.append(y_ref[0, e, pl.ds(a_in, W), :])
        cols = []
        for g in range(ngrp):
            first, last = (LANES * g) // W, (LANES * g + LANES - 1) // W
            t = jnp.full((1, LANES), tgt[last], F32)
            for e in range(last - 1, first - 1, -1):
                t = jnp.where(lane < (e + 1) * W - LANES * g, tgt[e], t)
            hit = rel[:, g * LANES:(g + 1) * LANES] == t
            cols.append(jnp.where(hit, 1.0, 0.0).astype(BF16))
        return acc + _dot(jnp.concatenate(cols, axis=1), jnp.concatenate(wins, axis=0))

    acc = lax.fori_loop(0, nmax_ref[b * nblk + blk], chunk, jnp.zeros(o_ref.shape, F32))
    o_ref[...] = h_ref[...] + acc


def _expand(aoff, nch, nmax, h, post, y, B, T, tb, dw, W):
    N, D = h.shape
    rows = y.shape[2]
    nblk = T // tb
    return pl.pallas_call(
        functools.partial(_expand_kernel, nblk=nblk, tb=tb, W=W, rows=rows),
        grid_spec=pltpu.PrefetchScalarGridSpec(
            num_scalar_prefetch=3,
            grid=(B, D // dw, nblk),
            in_specs=[
                pl.BlockSpec((tb, dw), lambda b, d, k, *_: (b * nblk + k, d)),
                pl.BlockSpec((tb, 2 * N_EXPERTS), lambda b, d, k, *_: (b * nblk + k, 0)),
                pl.BlockSpec((2 * N_EXPERTS, N_EXPERTS * W), lambda b, d, k, *_: (0, 0)),
                pl.BlockSpec((1, N_EXPERTS * W), lambda b, d, k, *_: (0, 0)),
                pl.BlockSpec((1, N_EXPERTS, rows, dw), lambda b, d, k, *_: (b, 0, 0, d),
                             pipeline_mode=pl.Buffered(1)),
            ],
            out_specs=pl.BlockSpec((tb, dw), lambda b, d, k, *_: (b * nblk + k, d)),
        ),
        out_shape=jax.ShapeDtypeStruct((N, D), F32),
        compiler_params=_cp("parallel", "parallel", "arbitrary"),
        name="moe_expand",
    )(aoff, nch, nmax, h, post, jnp.asarray(_spread_table(W), F32).astype(BF16),
      jnp.asarray((np.arange(N_EXPERTS * W) % W + 1).reshape(1, -1), F32), y)


def _moe(h, xn, gsp, aff4, wg, wu, wd, layer, B, T):
    N, D = h.shape
    tb = min(512, T)
    W = 96
    nblk = T // tb
    cap = EC_CAPACITY_FACTOR * T // N_EXPERTS
    live = -(-cap // BF16_SUBLANES) * BF16_SUBLANES
    rows = -(-(live + W) // LANES) * LANES
    dw = 256
    pos4, aoff4, cnt4 = _topk(aff4, T, tb)
    per = tb // LANES
    first = aoff4[:, ::per, :, 0]
    start = first & ~(BF16_SUBLANES - 1)
    aoff = start.reshape(-1)
    nch4 = (first - start + cnt4[:, ::per, :, 0] + (W - 1)) // W
    nch = nch4.reshape(-1)
    nmax = jnp.max(nch4, axis=-1).reshape(-1)
    post = jnp.transpose(pos4, (0, 1, 3, 2)).reshape(N, N_EXPERTS) + 1
    post = jnp.concatenate([post >> 5, post & 31], axis=1).astype(BF16)
    xg, gs = _compress(aoff, nch, nmax, xn, gsp, pos4, B, T, tb, rows, W)
    y = _ffn(xg, gs, wg, wu, wd, layer, live)
    return _expand(aoff, nch, nmax, h, post, y, B, T, tb, 2 * dw, W)


def _final_norm_kernel(h_ref, g_ref, o_ref):
    o_ref[...] = _rms(h_ref[...], g_ref[...])


def _final_norm(h, g, tm):
    N, D = h.shape
    return pl.pallas_call(
        _final_norm_kernel,
        grid=(N // tm,),
        in_specs=[pl.BlockSpec((tm, D), lambda i: (i, 0)), pl.BlockSpec((1, D), lambda i: (0, 0))],
        out_specs=pl.BlockSpec((tm, D), lambda i: (i, 0)),
        out_shape=jax.ShapeDtypeStruct((N, D), F32),
        compiler_params=_cp("parallel"),
        name="final_norm",
    )(h, g)


def kernel(x, mem, mix_norm_g, ffn_norm_g, mem_norm_g, final_norm_g, w_mem_kv, w_out, hy_w_in, hy_short_w, hy_filt_w1, hy_filt_b1, hy_filt_w2, hy_filt_b2, hy_filt_w3, hy_filt_freq, hy_skip, at_w_in, at_q_norm_g, at_k_norm_g, router_w, exp_w_gate, exp_w_up, exp_w_down):
    B, T, D = x.shape
    depth = mix_norm_g.shape[0]
    C = hy_skip.shape[-1]
    tm = min(512, T)
    cb = 16
    n1 = T // LANES
    h = x.reshape(B * T, D)
    row = lambda v: v.reshape(1, -1).astype(F32)
    lanes = lambda v: jnp.broadcast_to(v[..., None, None], v.shape + (1, LANES)).astype(F32)
    max_decay = math.log(HY_DECAY_TARGET) / HY_FAST_PCT
    min_decay = math.log(HY_DECAY_TARGET) / HY_SLOW_PCT
    absdelta = jnp.asarray(np.abs(np.linspace(min_decay, max_decay, C)).astype(np.float32)).reshape(C, 1)

    for i in range(depth):
        j = i // 2
        wo = w_out[i].astype(BF16)
        wkv = w_mem_kv[i].astype(BF16)
        mkt, mv = _memkv(mem, row(mem_norm_g), wkv[:, :D_MEM].T, wkv[:, D_MEM:])
        wr = router_w[i].astype(BF16)
        route_w = (row(ffn_norm_g[i]), jnp.zeros((D, LANES), BF16).at[:, :N_EXPERTS].set(wr), wr.T)
        if i % 2 == 0:
            w_in = hy_w_in[j].astype(BF16)
            ut, cq = _hy_inproj(h, row(mix_norm_g[i]), w_in[:, :3 * C].T, w_in[:, 3 * C:], B, T, tm)
            filt = dict(
                w1t=jnp.zeros((HY_FILT, 40), F32).at[:, :hy_filt_w1.shape[1]].set(hy_filt_w1[j].T),
                b1=hy_filt_b1[j].reshape(-1, 1), w2t=hy_filt_w2[j].T, b2=hy_filt_b2[j].reshape(-1, 1),
                fr=hy_filt_freq[j].reshape(-1, 1),
                w3t=hy_filt_w3[j].T.reshape(2, 2, C, HY_FILT), absdelta=absdelta,
            )
            kfft = _hyena_filters_fft(filt, T, 2 * cb)
            ut4 = ut.reshape(B, 3 * C, n1, LANES)
            if B % 2:
                ut4 = jnp.concatenate([ut4, jnp.zeros_like(ut4[:1])], axis=0)
            zt = _hy_conv(ut4, lanes(hy_short_w[j]), lanes(hy_skip[j]), kfft, B + B % 2, T, C, cb)[:B]
            routed = _outproj(h, zt, cq, mkt, mv, wo[:C], wo[C:], *route_w, B, T, min(1024, T))
        else:
            rep = lambda v, n: jnp.tile(v.astype(F32), n).reshape(1, -1)
            qt, kz, vtz, cq = _at_inproj(h, row(mix_norm_g[i]), at_w_in[j].astype(BF16),
                                         rep(at_q_norm_g[j], GQA_GROUP * N_KV_HEADS),
                                         rep(at_k_norm_g[j], N_KV_HEADS), B, T, tm)
            main_t = _flash(qt, kz, vtz, B, T, min(2048, T), min(1024, T))
            routed = _outproj(h, main_t, cq, mkt, mv, wo[:C], wo[C:], *route_w, B, T, tm)
        h = _moe(*routed, exp_w_gate, exp_w_up, exp_w_down, i, B, T)
    return _final_norm(h, row(final_norm_g), tm).reshape(B, T, D)
```

```python
import functools
import math

import numpy as np
import jax
import jax.numpy as jnp
from jax import lax
from jax.experimental import pallas as pl
from jax.experimental.pallas import tpu as pltpu

F32 = jnp.float32
BF16 = jnp.bfloat16
I32 = jnp.int32

HEAD_DIM = 64
MEM_HEADS = 4
D_MEM = MEM_HEADS * HEAD_DIM
N_KV_HEADS = 4
GQA_GROUP = 3
GRID_W = 64
ROPE_THETA = 10000.0
ROPE_AXIS_DIM = HEAD_DIM // 2
HY_BANDS = 16
HY_FILT = 64
HY_DECAY_TARGET = 1e-2
HY_FAST_PCT = 0.3
HY_SLOW_PCT = 1.5
N_EXPERTS = 16
EC_CAPACITY_FACTOR = 2
NORM_EPS = 1e-6

LANES = 128
BF16_SUBLANES = 16
VMEM_LIMIT = 56 * 1024 * 1024
HI = lax.Precision.HIGHEST
LOG2E = 1.4426950408889634


def _cp(*sem):
    return pltpu.CompilerParams(dimension_semantics=sem, vmem_limit_bytes=VMEM_LIMIT)


def _rms(x, g):
    ms = jnp.mean(x * x, axis=-1, keepdims=True)
    return x * lax.rsqrt(ms + NORM_EPS) * g


def _dot(a, b):
    return jnp.dot(a, b, preferred_element_type=F32)


def _dot3(a, b):
    ah, bh = a.astype(BF16), b.astype(BF16)
    al, bl = (a - ah.astype(F32)).astype(BF16), (b - bh.astype(F32)).astype(BF16)
    return _dot(ah, bh) + _dot(ah, bl) + _dot(al, bh)


def _dot_nt(a, b):
    return lax.dot_general(a, b, (((1,), (1,)), ((), ())), preferred_element_type=F32)


@functools.lru_cache(maxsize=None)
def _dft_tables(T, cb):
    nf = 2 * T
    n1k = nf // LANES
    n1 = T // LANES
    k1 = np.arange(n1k)[:, None]
    a1 = 2 * np.pi * k1 * np.arange(n1)[None, :] / n1k
    a1f = 2 * np.pi * k1 * np.arange(n1k)[None, :] / n1k
    f1full = np.concatenate([np.cos(a1f), -np.sin(a1f)], axis=0)
    tw = 2 * np.pi * k1 * np.arange(LANES)[None, :] / nf
    twr, twi = np.cos(tw), -np.sin(tw)
    a2 = 2 * np.pi * np.arange(LANES)[:, None] * np.arange(LANES)[None, :] / LANES
    cr, ci = np.cos(a2), -np.sin(a2)
    w2 = np.block([[cr, ci], [-ci, cr]])
    minv = np.block([[cr, -ci], [ci, cr]])
    a3 = 2 * np.pi * np.arange(n1)[:, None] * np.arange(n1k)[None, :] / n1k
    g1c = np.block([[np.cos(a3), -np.sin(a3)], [np.sin(a3), np.cos(a3)]]) / nf
    f1c = np.block([[np.cos(a1), np.sin(a1)], [-np.sin(a1), np.cos(a1)]])
    return dict(
        f1c=f1c, f1full=f1full, w2=w2, minv=minv, g1c=g1c,
        twr_l=np.tile(twr, (1, cb)), twi_l=np.tile(twi, (1, cb)),
        twr_r=np.tile(twr, (cb, 1)), twi_r=np.tile(twi, (cb, 1)),
    )


@functools.lru_cache(maxsize=None)
def _filter_feats(T):
    t = np.linspace(0.0, 1.0, T)[None, :]
    w = (2.0 * np.pi) * np.arange(T)[None, :] / T
    bands = np.linspace(1e-4, HY_BANDS - 1, HY_BANDS)[:, None]
    feats = np.concatenate([t, np.cos(bands * w), -np.sin(bands * w)], axis=0)
    pad = np.zeros((40 - feats.shape[0], T))
    feats = np.concatenate([feats, pad], axis=0)
    rev = (T - np.arange(T)) % T
    return (np.stack([feats, feats[:, rev]]).astype(np.float32),
            np.stack([t, t[:, rev]]).astype(np.float32))


@functools.lru_cache(maxsize=None)
def _rope_tables(T):
    rows = T // GRID_W
    pos_row = np.repeat(np.arange(rows), GRID_W).astype(np.float64)
    pos_col = np.tile(np.arange(GRID_W), rows).astype(np.float64)
    inv = 1.0 / (ROPE_THETA ** (np.arange(0, ROPE_AXIS_DIM, 2, dtype=np.float64) / ROPE_AXIS_DIM))
    lane = np.arange(LANES)
    d = lane % HEAD_DIM
    axis = d // ROPE_AXIS_DIM
    half = (d // (ROPE_AXIS_DIM // 2)) % 2
    f = d % (ROPE_AXIS_DIM // 2)
    pos = np.where(axis[None, :] == 0, pos_row[:, None], pos_col[:, None])
    ang = pos * inv[f][None, :]
    cos2 = np.cos(ang)
    sin2 = np.where(half[None, :] == 0, -np.sin(ang), np.sin(ang))
    return cos2.astype(np.float32), sin2.astype(np.float32)


@functools.lru_cache(maxsize=None)
def _spread_table(W):
    owner = np.arange(N_EXPERTS * W) // W
    hit = (np.arange(N_EXPERTS)[:, None] == owner[None, :]).astype(np.float32)
    return np.concatenate([32.0 * hit, hit], axis=0)


@functools.lru_cache(maxsize=None)
def _topk_tables(T, tb):
    nj = T // LANES
    per = tb // LANES
    r = np.arange(nj * N_EXPERTS)
    j, e = r // N_EXPERTS, r % N_EXPERTS
    same_e = e[:, None] == e[None, :]
    blk = j // per
    same_blk = blk[:, None] == blk[None, :]
    m_all = same_e & (j[None, :] < j[:, None])
    m_w = same_e & same_blk & (j[None, :] < j[:, None])
    m_b = same_e & same_blk
    tri = np.triu(np.ones((LANES, LANES)))
    return tuple(np.asarray(m, np.float32) for m in (m_all, m_w, m_b, tri))


def _hy_inproj_kernel(h_ref, g_ref, wmt_ref, wcq_ref, ut_ref, cq_ref):
    xn = _rms(h_ref[...], g_ref[...]).astype(BF16)
    ut_ref[0] = _dot_nt(wmt_ref[...], xn)
    cq_ref[...] = _dot(xn, wcq_ref[...]).astype(BF16)


def _hy_inproj(h, g, wmt, wcq, B, T, tm):
    N, D = h.shape
    c3 = wmt.shape[0]
    nt = T // tm
    return pl.pallas_call(
        _hy_inproj_kernel,
        grid=(B, nt),
        in_specs=[
            pl.BlockSpec((tm, D), lambda b, i: (b * nt + i, 0)),
            pl.BlockSpec((1, D), lambda b, i: (0, 0)),
            pl.BlockSpec((c3, D), lambda b, i: (0, 0)),
            pl.BlockSpec((D, D_MEM), lambda b, i: (0, 0)),
        ],
        out_specs=[
            pl.BlockSpec((1, c3, tm), lambda b, i: (b, 0, i)),
            pl.BlockSpec((tm, D_MEM), lambda b, i: (b * nt + i, 0)),
        ],
        out_shape=[
            jax.ShapeDtypeStruct((B, c3, T), F32),
            jax.ShapeDtypeStruct((N, D_MEM), BF16),
        ],
        compiler_params=_cp("parallel", "parallel"),
        name="hy_inproj",
    )(h, g, wmt, wcq)


def _filt_mlp_kernel(feats_ref, w1t_ref, b1_ref, w2t_ref, b2_ref, fr_ref, h2_ref):
    fr = fr_ref[...]
    for d in range(2):
        a = jnp.dot(w1t_ref[...], feats_ref[d], precision=HI, preferred_element_type=F32)
        h1 = jnp.sin(fr * (a + b1_ref[...]))
        a = jnp.dot(w2t_ref[...], h1, precision=HI, preferred_element_type=F32)
        h2_ref[d] = jnp.sin(fr * (a + b2_ref[...]))


def _filt_taps_kernel(h2_ref, w3t_ref, t_ref, dl_ref, out_ref):
    T = h2_ref.shape[2]
    dl = dl_ref[...]
    hf = _dot3(w3t_ref[0, 0], h2_ref[0]) * jnp.exp(-t_ref[0] * dl)
    hb = _dot3(w3t_ref[0, 1], h2_ref[1]) * jnp.exp(-t_ref[1] * dl)
    nrm = jnp.sum(jnp.abs(hf) + jnp.abs(hb), axis=-1, keepdims=True) + 1e-6
    inv = 1.0 / nrm
    tap0 = lax.broadcasted_iota(I32, hf.shape, 1) == 0
    out_ref[0, :, :T] = ((hf + jnp.where(tap0, hb, 0.0)) * inv).astype(BF16)
    out_ref[0, :, T:] = (jnp.where(tap0, 0.0, hb) * inv).astype(BF16)


def _fwd_fft(x3, f1s, twr, twi, w2, cb, n1k):
    rhs = jnp.concatenate([x3[c].astype(BF16) for c in range(cb)], axis=1)
    a = _dot(f1s, rhs)
    ar, ai = a[:n1k], a[n1k:]
    tr = (ar * twr - ai * twi).astype(BF16)
    ti = (ar * twi + ai * twr).astype(BF16)
    lr = jnp.concatenate([tr[:, c * LANES:(c + 1) * LANES] for c in range(cb)], axis=0)
    li = jnp.concatenate([ti[:, c * LANES:(c + 1) * LANES] for c in range(cb)], axis=0)
    return _dot(jnp.concatenate([lr, li], axis=1), w2)


def _inv_fft(y, minv, twr, twi, g1, cb, n1k):
    b = _dot(y.astype(BF16), minv)
    br, bi = b[:, :LANES], b[:, LANES:]
    pr = (br * twr + bi * twi).astype(BF16)
    pi = (bi * twr - br * twi).astype(BF16)
    top = jnp.concatenate([pr[c * n1k:(c + 1) * n1k] for c in range(cb)], axis=1)
    bot = jnp.concatenate([pi[c * n1k:(c + 1) * n1k] for c in range(cb)], axis=1)
    rhs = jnp.concatenate([top, bot], axis=0)
    return _dot(g1, rhs)


def _filt_fft_kernel(taps_ref, f1_ref, twr_ref, twi_ref, w2_ref, k_ref, *, cb, n1k):
    k_ref[0] = _fwd_fft(taps_ref[0], f1_ref[...], twr_ref[...], twi_ref[...], w2_ref[...], cb, n1k).astype(BF16)


def _hyena_filters_fft(p, T, cb):
    feats, t_rows = _filter_feats(T)
    C = p["w3t"].shape[2]
    tabs = _dft_tables(T, cb)
    n1k = 2 * T // LANES
    h2 = pl.pallas_call(
        _filt_mlp_kernel,
        out_shape=jax.ShapeDtypeStruct((2, HY_FILT, T), F32),
        compiler_params=pltpu.CompilerParams(vmem_limit_bytes=VMEM_LIMIT),
        name="hy_filt_mlp",
    )(jnp.asarray(feats), p["w1t"], p["b1"], p["w2t"], p["b2"], p["fr"])
    cbt = 64
    taps = pl.pallas_call(
        _filt_taps_kernel,
        grid=(2, C // cbt),
        in_specs=[
            pl.BlockSpec((2, HY_FILT, T), lambda o, c: (0, 0, 0)),
            pl.BlockSpec((1, 2, cbt, HY_FILT), lambda o, c: (o, 0, c, 0)),
            pl.BlockSpec((2, 1, T), lambda o, c: (0, 0, 0)),
            pl.BlockSpec((cbt, 1), lambda o, c: (c, 0)),
        ],
        out_specs=pl.BlockSpec((1, cbt, 2 * T), lambda o, c: (o, c, 0)),
        out_shape=jax.ShapeDtypeStruct((2, C, 2 * T), BF16),
        compiler_params=_cp("parallel", "parallel"),
        name="hy_filt_taps",
    )(h2, p["w3t"], jnp.asarray(t_rows), p["absdelta"])
    taps = taps.reshape(2, C, n1k, LANES)
    const = lambda shape: pl.BlockSpec(shape, lambda o, c: (0,) * len(shape))
    f32 = lambda k: jnp.asarray(tabs[k], F32)
    return pl.pallas_call(
        functools.partial(_filt_fft_kernel, cb=cb, n1k=n1k),
        grid=(2, C // cb),
        in_specs=[
            pl.BlockSpec((1, cb, n1k, LANES), lambda o, c: (o, c, 0, 0)),
            const((2 * n1k, n1k)), const((n1k, cb * LANES)), const((n1k, cb * LANES)),
            const((2 * LANES, 2 * LANES)),
        ],
        out_specs=pl.BlockSpec((1, cb * n1k, 2 * LANES), lambda o, c: (o, c, 0)),
        out_shape=jax.ShapeDtypeStruct((2, C * n1k, 2 * LANES), BF16),
        compiler_params=_cp("parallel", "parallel"),
        name="hy_filt_fft",
    )(taps, f32("f1full").astype(BF16), f32("twr_l"), f32("twi_l"), f32("w2").astype(BF16))


def _time_neighbours(x):
    rows = x.shape[0]
    lane = lax.broadcasted_iota(I32, x.shape, 1)
    r = pltpu.roll(x, 1, 1)
    rr = pltpu.roll(r, 1, 0)
    prev = jnp.where(lane == 0, rr, r)
    r2 = pltpu.roll(x, LANES - 1, 1)
    rr2 = pltpu.roll(r2, rows - 1, 0)
    nxt = jnp.where(lane == LANES - 1, rr2, r2)
    return prev, nxt


def _hy_conv_kernel(x1_ref, x2_ref, v_ref, sw1_ref, sw2_ref, swv_ref, skip_ref, k_ref,
                    f1c_ref, twrl_ref, twil_ref, w2_ref, minv_ref, twrr_ref, twir_ref,
                    g1c_ref, o_ref, *, cb, n1k, n1):
    rows = cb * n1
    shape2 = (rows, LANES)
    row = lax.broadcasted_iota(I32, shape2, 0)
    lane = lax.broadcasted_iota(I32, shape2, 1)
    first = (lane == 0) & (row % n1 == 0)
    last = (lane == LANES - 1) & (row % n1 == n1 - 1)

    def sconv(x, sw_ref):
        x = x.reshape(shape2)
        prev, nxt = _time_neighbours(x)
        prev = jnp.where(first, 0.0, prev)
        nxt = jnp.where(last, 0.0, nxt)
        w = [jnp.broadcast_to(sw_ref[j], (cb, n1, LANES)).reshape(shape2) for j in range(3)]
        return prev * w[0] + x * w[1] + nxt * w[2]

    z = [sconv(v_ref[b], swv_ref) for b in range(2)]
    gates = [[sconv(x_ref[b], sw_ref) for b in range(2)] for x_ref, sw_ref in ((x1_ref, sw1_ref), (x2_ref, sw2_ref))]
    f1c, w2 = f1c_ref[...], w2_ref[...]
    for o in range(2):
        pair = jnp.concatenate([z[0].reshape(cb, n1, LANES), z[1].reshape(cb, n1, LANES)], axis=1)
        zf = _fwd_fft(pair, f1c, twrl_ref[...], twil_ref[...], w2, cb, n1k)
        kk = k_ref[o].astype(F32)
        zr, zi = zf[:, :LANES], zf[:, LANES:]
        kr, ki = kk[:, :LANES], kk[:, LANES:]
        y = jnp.concatenate([zr * kr - zi * ki, zr * ki + zi * kr], axis=1)
        conv = _inv_fft(y, minv_ref[...], twrr_ref[...], twir_ref[...], g1c_ref[...], cb, n1k)
        skip = jnp.broadcast_to(skip_ref[o], (cb, n1, LANES)).reshape(shape2)
        for b in range(2):
            cv = conv[b * n1:(b + 1) * n1]
            cv = jnp.concatenate([cv[:, c * LANES:(c + 1) * LANES] for c in range(cb)], axis=0)
            z[b] = gates[o][b] * (cv + skip * z[b])
    for b in range(2):
        o_ref[b] = z[b].reshape(cb, n1, LANES)


def _hy_conv(ut4, sw, skip, kfft, B, T, C, cb):
    n1k, n1 = 2 * T // LANES, T // LANES
    tabs = _dft_tables(T, cb)
    nct = C // cb
    bf = lambda k: jnp.asarray(tabs[k], F32).astype(BF16)
    f32 = lambda k: jnp.asarray(tabs[k], F32)
    const = lambda shape: pl.BlockSpec(shape, lambda c, b: (0,) * len(shape))
    ublk = lambda s: pl.BlockSpec((2, cb, n1, LANES), lambda c, b, s=s: (b, s * nct + c, 0, 0))
    wblk = lambda s: pl.BlockSpec((3, cb, 1, LANES), lambda c, b, s=s: (0, s * nct + c, 0, 0))
    return pl.pallas_call(
        functools.partial(_hy_conv_kernel, cb=cb, n1k=n1k, n1=n1),
        grid=(nct, B // 2),
        in_specs=[
            ublk(0), ublk(1), ublk(2), wblk(0), wblk(1), wblk(2),
            pl.BlockSpec((2, cb, 1, LANES), lambda c, b: (0, c, 0, 0)),
            pl.BlockSpec((2, cb * n1k, 2 * LANES), lambda c, b: (0, c, 0)),
            const((2 * n1k, 2 * n1)), const((n1k, cb * LANES)), const((n1k, cb * LANES)),
            const((2 * LANES, 2 * LANES)), const((2 * LANES, 2 * LANES)),
            const((cb * n1k, LANES)), const((cb * n1k, LANES)), const((2 * n1, 2 * n1k)),
        ],
        out_specs=pl.BlockSpec((2, cb, n1, LANES), lambda c, b: (b, c, 0, 0)),
        out_shape=jax.ShapeDtypeStruct((B, C, n1, LANES), F32),
        compiler_params=_cp("parallel", "arbitrary"),
        name="hy_conv",
    )(ut4, ut4, ut4, sw, sw, sw, skip, kfft,
      bf("f1c"), f32("twr_l"), f32("twi_l"), bf("w2"), bf("minv"),
      f32("twr_r"), f32("twi_r"), bf("g1c"))


def _head_norm(x, bd, g):
    ss = _dot((x * x).astype(BF16), bd)
    return x * lax.rsqrt(ss * (1.0 / HEAD_DIM) + NORM_EPS) * g


def _rope(x, cos2, sin2):
    lane = lax.broadcasted_iota(I32, cos2.shape, 1)
    low = (lane // (ROPE_AXIS_DIM // 2)) % 2 == 0
    out = []
    for c in range(x.shape[1] // LANES):
        xc = x[:, c * LANES:(c + 1) * LANES]
        up = pltpu.roll(xc, LANES - 16, 1)
        dn = pltpu.roll(xc, 16, 1)
        out.append(xc * cos2 + jnp.where(low, up, dn) * sin2)
    return jnp.concatenate(out, axis=1)


def _at_inproj_kernel(h_ref, g_ref, w_ref, bdq_ref, bdk_ref, gq_ref, gk_ref, cos_ref, sin_ref,
                      qt_ref, kz_ref, vt_ref, cq_ref, *, dq, dk):
    xn = _rms(h_ref[...], g_ref[...]).astype(BF16)
    proj = _dot(xn, w_ref[...])
    q, k = proj[:, :dq], proj[:, dq:dq + dk]
    v, cq = proj[:, dq + dk:dq + 2 * dk], proj[:, dq + 2 * dk:]
    cos2, sin2 = cos_ref[...], sin_ref[...]
    qr = _rope(_head_norm(q, bdq_ref[...], gq_ref[...]), cos2, sin2)
    qt = (qr * (HEAD_DIM ** -0.5 * LOG2E)).T.astype(BF16)
    for p in range(dq // LANES):
        qt_ref[0, p] = qt[p * LANES:(p + 1) * LANES]
    kr = _rope(_head_norm(k, bdk_ref[...], gk_ref[...]), cos2, sin2)
    vt = v.T.astype(BF16)
    zv = jnp.zeros((HEAD_DIM, vt.shape[1]), BF16)
    lane = lax.broadcasted_iota(I32, (k.shape[0], LANES), 1)
    for kv in range(N_KV_HEADS):
        rows = vt[kv * HEAD_DIM:(kv + 1) * HEAD_DIM]
        vt_ref[0, kv, 0] = jnp.concatenate([rows, zv], axis=0)
        vt_ref[0, kv, 1] = jnp.concatenate([zv, rows], axis=0)
        pair = kr[:, (kv // 2) * LANES:(kv // 2 + 1) * LANES]
        own = jnp.where((lane < HEAD_DIM) == (kv % 2 == 0), pair, 0.0)
        other = pltpu.roll(own, HEAD_DIM, 1)
        lo, hi = (own, other) if kv % 2 == 0 else (other, own)
        kz_ref[0, kv, 0] = lo.astype(BF16)
        kz_ref[0, kv, 1] = hi.astype(BF16)
    cq_ref[...] = cq.astype(BF16)


def _at_inproj(h, g, w, gq, gk, B, T, tm):
    N, D = h.shape
    dq, dk = GQA_GROUP * N_KV_HEADS * HEAD_DIM, N_KV_HEADS * HEAD_DIM
    nt = T // tm
    cos2, sin2 = _rope_tables(T)
    bd = lambda n: jnp.asarray(np.kron(np.eye(n // HEAD_DIM), np.ones((HEAD_DIM, HEAD_DIM))), F32).astype(BF16)
    const = lambda shape: pl.BlockSpec(shape, lambda b, i: (0,) * len(shape))
    return pl.pallas_call(
        functools.partial(_at_inproj_kernel, dq=dq, dk=dk),
        grid=(B, nt),
        in_specs=[
            pl.BlockSpec((tm, D), lambda b, i: (b * nt + i, 0)),
            const((1, D)), const(w.shape), const((dq, dq)), const((dk, dk)),
            const((1, dq)), const((1, dk)),
            pl.BlockSpec((tm, LANES), lambda b, i: (i, 0)),
            pl.BlockSpec((tm, LANES), lambda b, i: (i, 0)),
        ],
        out_specs=[
            pl.BlockSpec((1, dq // LANES, LANES, tm), lambda b, i: (b, 0, 0, i)),
            pl.BlockSpec((1, N_KV_HEADS, 2, tm, LANES), lambda b, i: (b, 0, 0, i, 0)),
            pl.BlockSpec((1, N_KV_HEADS, 2, LANES, tm), lambda b, i: (b, 0, 0, 0, i)),
            pl.BlockSpec((tm, D_MEM), lambda b, i: (b * nt + i, 0)),
        ],
        out_shape=[
            jax.ShapeDtypeStruct((B, dq // LANES, LANES, T), BF16),
            jax.ShapeDtypeStruct((B, N_KV_HEADS, 2, T, LANES), BF16),
            jax.ShapeDtypeStruct((B, N_KV_HEADS, 2, LANES, T), BF16),
            jax.ShapeDtypeStruct((N, D_MEM), BF16),
        ],
        compiler_params=_cp("parallel", "parallel"),
        name="at_inproj",
    )(h, g, w, bd(dq), bd(dk), gq, gk, jnp.asarray(cos2), jnp.asarray(sin2))


def _flash_kernel(qt_ref, ka_ref, kb_ref, vta_ref, vtb_ref, o_ref, m_sc, l_sc, acc_sc, s_sc, p_sc):
    ki = pl.program_id(3)

    @pl.when(ki == 0)
    def _():
        m_sc[...] = jnp.full(m_sc.shape, -jnp.inf, F32)
        l_sc[...] = jnp.zeros(l_sc.shape, F32)
        acc_sc[...] = jnp.zeros(acc_sc.shape, F32)

    qt = qt_ref[0, 0]
    tk, tq = s_sc.shape[1], s_sc.shape[2]
    sub = 8
    ck = 2 * sub
    pv, alphas = [], []
    for idx, k_ref in enumerate((ka_ref, kb_ref)):
        s_sc[idx] = _dot(k_ref[0, 0, 0], qt)
    for idx, vt_ref in enumerate((vta_ref, vtb_ref)):
        mx = s_sc[idx, 0:sub, :]
        for c in range(1, tk // sub):
            mx = jnp.maximum(mx, s_sc[idx, c * sub:(c + 1) * sub, :])
        m_prev = m_sc[idx]
        m_new = jnp.maximum(m_prev, jnp.max(mx, axis=0, keepdims=True))
        alpha = jnp.exp2(m_prev - m_new)
        lsum = jnp.zeros((sub, tq), F32)
        for c in range(tk // ck):
            p = jnp.exp2(s_sc[idx, c * ck:(c + 1) * ck, :] - m_new)
            lsum = lsum + p[:sub] + p[sub:]
            p_sc[idx, c * ck:(c + 1) * ck, :] = p.astype(BF16)
        l_sc[idx] = alpha * l_sc[idx] + jnp.sum(lsum, axis=0, keepdims=True)
        m_sc[idx] = m_new
        pv.append(_dot(vt_ref[0, 0, 0], p_sc[idx]))
        alphas.append(alpha)
    row = lax.broadcasted_iota(I32, acc_sc.shape, 0)
    low = row < HEAD_DIM
    acc_sc[...] = acc_sc[...] * jnp.where(low, alphas[0], alphas[1]) + pv[0] + pv[1]

    @pl.when(ki == pl.num_programs(3) - 1)
    def _():
        o_ref[0] = acc_sc[...] / jnp.where(low, l_sc[0], l_sc[1])


def _flash(qt, kz, vtz, B, T, tq, tk):
    npair = qt.shape[1]
    nq, nk = T // tq, T // tk
    kv_of = lambda p, j: (2 * p + j) // GQA_GROUP
    return pl.pallas_call(
        _flash_kernel,
        grid=(B, npair, nq, nk),
        in_specs=[
            pl.BlockSpec((1, 1, LANES, tq), lambda b, p, i, k: (b, p, 0, i)),
            pl.BlockSpec((1, 1, 1, tk, LANES), lambda b, p, i, k: (b, kv_of(p, 0), 0, k, 0)),
            pl.BlockSpec((1, 1, 1, tk, LANES), lambda b, p, i, k: (b, kv_of(p, 1), 1, k, 0)),
            pl.BlockSpec((1, 1, 1, LANES, tk), lambda b, p, i, k: (b, kv_of(p, 0), 0, 0, k)),
            pl.BlockSpec((1, 1, 1, LANES, tk), lambda b, p, i, k: (b, kv_of(p, 1), 1, 0, k)),
        ],
        out_specs=pl.BlockSpec((1, LANES, tq), lambda b, p, i, k: (b, p, i)),
        out_shape=jax.ShapeDtypeStruct((B, npair * LANES, T), F32),
        scratch_shapes=[
            pltpu.VMEM((2, 1, tq), F32), pltpu.VMEM((2, 1, tq), F32), pltpu.VMEM((LANES, tq), F32),
            pltpu.VMEM((2, tk, tq), F32), pltpu.VMEM((2, tk, tq), BF16),
        ],
        compiler_params=_cp("parallel", "parallel", "parallel", "arbitrary"),
        name="flash_gqa",
    )(qt, kz, kz, vtz, vtz)


def _memkv_kernel(mem_ref, g_ref, wkvt_ref, wv_ref, mkt_ref, mv_ref):
    mn = _rms(mem_ref[0], g_ref[...]).astype(BF16)
    kt = _dot_nt(wkvt_ref[...], mn) * (HEAD_DIM ** -0.5)
    v = _dot(mn, wv_ref[...])
    row = lax.broadcasted_iota(I32, kt.shape, 0)
    col = lax.broadcasted_iota(I32, v.shape, 1)
    for hd in range(MEM_HEADS):
        mkt_ref[0, hd] = jnp.where(row // HEAD_DIM == hd, kt, 0.0).astype(BF16)
        mv_ref[0, hd] = jnp.where(col // HEAD_DIM == hd, v, 0.0).astype(BF16)


def _memkv(mem, g, wkt, wv):
    B, M, D = mem.shape
    return pl.pallas_call(
        _memkv_kernel,
        grid=(B,),
        in_specs=[
            pl.BlockSpec((1, M, D), lambda b: (b, 0, 0)),
            pl.BlockSpec((1, D), lambda b: (0, 0)),
            pl.BlockSpec((D_MEM, D), lambda b: (0, 0)),
            pl.BlockSpec((D, D_MEM), lambda b: (0, 0)),
        ],
        out_specs=[
            pl.BlockSpec((1, MEM_HEADS, D_MEM, M), lambda b: (b, 0, 0, 0)),
            pl.BlockSpec((1, MEM_HEADS, M, D_MEM), lambda b: (b, 0, 0, 0)),
        ],
        out_shape=[
            jax.ShapeDtypeStruct((B, MEM_HEADS, D_MEM, M), BF16),
            jax.ShapeDtypeStruct((B, MEM_HEADS, M, D_MEM), BF16),
        ],
        compiler_params=_cp("parallel"),
        name="mem_kv",
    )(mem, g, wkt, wv)


def _cross_attn(cq, mkt_ref, mv_ref):
    acc = jnp.zeros(cq.shape, F32)
    for hd in range(MEM_HEADS):
        s = _dot(cq, mkt_ref[0, hd])
        p = jnp.exp(s - jnp.max(s, axis=-1, keepdims=True))
        p = p / jnp.sum(p, axis=-1, keepdims=True)
        acc = acc + _dot(p.astype(BF16), mv_ref[0, hd])
    return acc


def _route(h, g_ref, wr_ref, wrt_ref, xn_ref, gsp_ref, afft_ref):
    xn = _rms(h, g_ref[...]).astype(BF16)
    xn_ref[...] = xn
    lg = _dot(xn, wr_ref[...])
    lane = lax.broadcasted_iota(I32, lg.shape, 1)
    lg = jnp.where(lane < N_EXPERTS, lg, -jnp.inf)
    p = jnp.exp(lg - jnp.max(lg, axis=-1, keepdims=True))
    aff = p / jnp.sum(p, axis=-1, keepdims=True)
    hi = aff.astype(BF16).astype(F32)
    mid = (aff - hi).astype(BF16).astype(F32)
    lo = (aff - hi - mid).astype(BF16).astype(F32)
    gsp_ref[...] = (hi + pltpu.roll(mid, N_EXPERTS, 1) + pltpu.roll(lo, 2 * N_EXPERTS, 1)).astype(BF16)
    lt = _dot_nt(wrt_ref[...], xn)
    pt = jnp.exp(lt - jnp.max(lt, axis=0, keepdims=True))
    pt = pt / jnp.sum(pt, axis=0, keepdims=True)
    for j in range(lt.shape[1] // LANES):
        afft_ref[0, j] = pt[:, j * LANES:(j + 1) * LANES]


def _outproj_kernel(h_ref, main_ref, cq_ref, mkt_ref, mv_ref, wm_ref, wc_ref, g_ref, wr_ref, wrt_ref,
                    o_ref, xn_ref, gsp_ref, afft_ref):
    if len(main_ref.shape) == 3:
        main = main_ref[0].T
    else:
        main = jnp.concatenate([main_ref[0, :, j, :].T for j in range(main_ref.shape[2])], axis=0)
    main = main.astype(BF16)
    cross = _cross_attn(cq_ref[...], mkt_ref, mv_ref).astype(BF16)
    h_new = h_ref[...] + _dot(main, wm_ref[...]) + _dot(cross, wc_ref[...])
    o_ref[...] = h_new
    _route(h_new, g_ref, wr_ref, wrt_ref, xn_ref, gsp_ref, afft_ref)


def _outproj(h, main, cq, mkt, mv, wm, wc, g, wr, wrt, B, T, tm):
    N, D = h.shape
    C = wm.shape[0]
    M = mkt.shape[-1]
    nt = T // tm
    if main.ndim == 3:
        mspec = pl.BlockSpec((1, C, tm), lambda b, i: (b, 0, i))
    else:
        mspec = pl.BlockSpec((1, C, tm // LANES, LANES), lambda b, i: (b, 0, i, 0))
    return pl.pallas_call(
        _outproj_kernel,
        grid=(B, nt),
        in_specs=[
            pl.BlockSpec((tm, D), lambda b, i: (b * nt + i, 0)),
            mspec,
            pl.BlockSpec((tm, D_MEM), lambda b, i: (b * nt + i, 0)),
            pl.BlockSpec((1, MEM_HEADS, D_MEM, M), lambda b, i: (b, 0, 0, 0)),
            pl.BlockSpec((1, MEM_HEADS, M, D_MEM), lambda b, i: (b, 0, 0, 0)),
            pl.BlockSpec((C, D), lambda b, i: (0, 0)),
            pl.BlockSpec((D_MEM, D), lambda b, i: (0, 0)),
            pl.BlockSpec((1, D), lambda b, i: (0, 0)),
            pl.BlockSpec((D, LANES), lambda b, i: (0, 0)),
            pl.BlockSpec((N_EXPERTS, D), lambda b, i: (0, 0)),
        ],
        out_specs=[
            pl.BlockSpec((tm, D), lambda b, i: (b * nt + i, 0)),
            pl.BlockSpec((tm, D), lambda b, i: (b * nt + i, 0)),
            pl.BlockSpec((tm, LANES), lambda b, i: (b * nt + i, 0)),
            pl.BlockSpec((1, tm // LANES, N_EXPERTS, LANES), lambda b, i: (b, i, 0, 0)),
        ],
        out_shape=[
            jax.ShapeDtypeStruct((N, D), F32),
            jax.ShapeDtypeStruct((N, D), BF16),
            jax.ShapeDtypeStruct((N, LANES), BF16),
            jax.ShapeDtypeStruct((B, T // LANES, N_EXPERTS, LANES), F32),
        ],
        compiler_params=_cp("parallel", "parallel"),
        name="outproj",
    )(h, main, cq, mkt, mv, wm, wc, g, wr, wrt)


def _topk_kernel(aff_ref, mall_ref, mw_ref, mb_ref, tri_ref, pos_ref, aoff_ref, cnt_ref, *, cap, nj):
    E = N_EXPERTS
    aff3 = aff_ref[0]

    def count(mask3):
        per = jnp.sum(mask3.astype(F32), axis=0)
        return jnp.broadcast_to(jnp.sum(per, axis=-1, keepdims=True), (E, LANES))

    def step(i, thr):
        cand = thr | lax.shift_left(jnp.int32(1), 30 - i)
        ok = count(aff3 >= pltpu.bitcast(cand, F32)[None]) >= cap
        return jnp.where(ok, cand, thr)

    thr = lax.fori_loop(0, 31, step, jnp.zeros((E, LANES), I32))
    thr = pltpu.bitcast(thr, F32)
    gt3 = aff3 > thr[None]
    eq3 = aff3 == thr[None]
    need = cap - count(gt3)

    ones = jnp.ones((LANES, LANES), BF16)
    tri = tri_ref[...]

    def prefix(mask2):
        mb = mask2.astype(BF16)
        incl = _dot(mb, tri)
        tot = _dot(mb, ones)
        return incl - mask2, tot

    eq2 = eq3.reshape(nj * E, LANES).astype(F32)
    ex, tot = prefix(eq2)
    eq_rank = ex + _dot(mall_ref[...], tot.astype(BF16))
    need2 = jnp.broadcast_to(need[None], (nj, E, LANES)).reshape(nj * E, LANES)
    sel = jnp.where((gt3.reshape(nj * E, LANES)) | ((eq2 > 0) & (eq_rank < need2)), 1.0, 0.0)

    ex, tot = prefix(sel)
    totb = tot.astype(BF16)
    before = _dot(mall_ref[...], totb)
    within = _dot(mw_ref[...], totb)
    cnt = _dot(mb_ref[...], totb)
    pos_ref[0] = jnp.where(sel > 0, (before + ex).astype(I32), -1).reshape(nj, E, LANES)
    aoff_ref[0] = (before - within).astype(I32).reshape(nj, E, LANES)
    cnt_ref[0] = cnt.astype(I32).reshape(nj, E, LANES)


def _topk(aff4, T, tb):
    B, nj = aff4.shape[0], aff4.shape[1]
    cap = EC_CAPACITY_FACTOR * T // N_EXPERTS
    tabs = [jnp.asarray(m, F32).astype(BF16) for m in _topk_tables(T, tb)]
    R = nj * N_EXPERTS
    blk = pl.BlockSpec((1, nj, N_EXPERTS, LANES), lambda b: (b, 0, 0, 0))
    const = lambda shape: pl.BlockSpec(shape, lambda b: (0,) * len(shape))
    out = jax.ShapeDtypeStruct((B, nj, N_EXPERTS, LANES), I32)
    return pl.pallas_call(
        functools.partial(_topk_kernel, cap=cap, nj=nj),
        grid=(B,),
        in_specs=[blk, const((R, R)), const((R, R)), const((R, R)), const((LANES, LANES))],
        out_specs=[blk, blk, blk],
        out_shape=[out, out, out],
        compiler_params=_cp("parallel"),
        name="moe_topk",
    )(aff4, *tabs)


def _compress_kernel(aoff_ref, nch_ref, nmax_ref, xn_ref, gsp_ref, pos_ref, xg_ref, gs_ref,
                     *, nblk, tb, W, eg, nsub):
    b, grp, blk = pl.program_id(0), pl.program_id(1), pl.program_id(2)

    @pl.when(blk == 0)
    def _():
        xg_ref[...] = jnp.zeros(xg_ref.shape, BF16)
        gs_ref[...] = jnp.zeros(gs_ref.shape, F32)

    iota_s = lax.broadcasted_iota(I32, (W, LANES), 0)
    per = tb // LANES
    for sb in range(nsub):
        tblk = blk * nsub + sb
        base = (b * nblk + tblk) * N_EXPERTS + grp * eg
        tok = slice(sb * tb, (sb + 1) * tb)

        def chunk(c, carry, base=base, tok=tok, sb=sb):
            pieces = []
            for i in range(eg):
                a = aoff_ref[base + i] + c * W
                e = grp * eg + i
                g = [pos_ref[0, sb * per + jj, pl.ds(e, 1), :] - a == iota_s for jj in range(per)]
                pieces.append(jnp.where(jnp.concatenate(g, axis=1), 1.0, 0.0).astype(BF16))
            lhs = jnp.concatenate(pieces, axis=0)
            res = _dot(lhs, xn_ref[tok, :])
            resg = _dot(lhs, gsp_ref[tok, :])
            for i in range(eg):
                @pl.when(c < nch_ref[base + i])
                def _():
                    win = pl.ds(pl.multiple_of(aoff_ref[base + i] + c * W, BF16_SUBLANES), W)
                    xg_ref[0, i, win, :] = (xg_ref[0, i, win, :] + res[i * W:(i + 1) * W]).astype(BF16)
                    gs_ref[0, i, win, :] = gs_ref[0, i, win, :] + resg[i * W:(i + 1) * W]
            return carry

        lax.fori_loop(0, nmax_ref[b * nblk + tblk], chunk, 0)


def _compress(aoff, nch, nmax, xn, gsp, pos4, B, T, tb, rows, W):
    N, D = xn.shape
    nblk = T // tb
    eg = 4
    nsub = 2 if nblk % 2 == 0 else 1
    ns, ts = nblk // nsub, nsub * tb
    return pl.pallas_call(
        functools.partial(_compress_kernel, nblk=nblk, tb=tb, W=W, eg=eg, nsub=nsub),
        grid_spec=pltpu.PrefetchScalarGridSpec(
            num_scalar_prefetch=3,
            grid=(B, N_EXPERTS // eg, ns),
            in_specs=[
                pl.BlockSpec((ts, D), lambda b, g, k, *_: (b * ns + k, 0)),
                pl.BlockSpec((ts, LANES), lambda b, g, k, *_: (b * ns + k, 0)),
                pl.BlockSpec((1, ts // LANES, N_EXPERTS, LANES), lambda b, g, k, *_: (b, k, 0, 0)),
            ],
            out_specs=[
                pl.BlockSpec((1, eg, rows, D), lambda b, g, k, *_: (b, g, 0, 0)),
                pl.BlockSpec((1, eg, rows, LANES), lambda b, g, k, *_: (b, g, 0, 0)),
            ],
        ),
        out_shape=[
            jax.ShapeDtypeStruct((B, N_EXPERTS, rows, D), BF16),
            jax.ShapeDtypeStruct((B, N_EXPERTS, rows, LANES), F32),
        ],
        compiler_params=_cp("parallel", "parallel", "arbitrary"),
        name="moe_compress",
    )(aoff, nch, nmax, xn, gsp, pos4)


def _ffn_kernel(x_ref, gs_ref, wg_ref, wu_ref, wd_ref, y_ref, *, live):
    e = pl.program_id(0)
    x = x_ref[0, 0, :live]
    g = _dot(x, wg_ref[0, 0].astype(BF16))
    u = _dot(x, wu_ref[0, 0].astype(BF16))
    hid = (g * (1.0 / (1.0 + jnp.exp(-g))) * u).astype(BF16)
    gs = gs_ref[0, 0, :live]
    lane = lax.broadcasted_iota(I32, gs.shape, 1)
    mine = (lane == e) | (lane == e + N_EXPERTS) | (lane == e + 2 * N_EXPERTS)
    gate = jnp.sum(jnp.where(mine, gs, 0.0), axis=-1, keepdims=True)
    y_ref[0, 0, :live] = (_dot(hid, wd_ref[0, 0].astype(BF16)) * gate).astype(BF16)
    y_ref[0, 0, live:] = jnp.zeros((y_ref.shape[2] - live, y_ref.shape[3]), BF16)


def _ffn(xg, gs, wg, wu, wd, layer, live):
    B, E, rows, D = xg.shape
    F = wg.shape[-1]
    return pl.pallas_call(
        functools.partial(_ffn_kernel, live=live),
        grid=(E, B),
        in_specs=[
            pl.BlockSpec((1, 1, rows, D), lambda e, b: (b, e, 0, 0)),
            pl.BlockSpec((1, 1, rows, LANES), lambda e, b: (b, e, 0, 0)),
            pl.BlockSpec((1, 1, D, F), lambda e, b: (layer, e, 0, 0)),
            pl.BlockSpec((1, 1, D, F), lambda e, b: (layer, e, 0, 0)),
            pl.BlockSpec((1, 1, F, D), lambda e, b: (layer, e, 0, 0)),
        ],
        out_specs=pl.BlockSpec((1, 1, rows, D), lambda e, b: (b, e, 0, 0)),
        out_shape=jax.ShapeDtypeStruct((B, E, rows, D), BF16),
        compiler_params=_cp("parallel", "arbitrary"),
        name="moe_ffn",
    )(xg, gs, wg, wu, wd)


def _expand_kernel(aoff_ref, nch_ref, nmax_ref, h_ref, post_ref, spread_ref, slot1_ref, y_ref, o_ref,
                   *, nblk, tb, W, rows, nsub):
    b, blk = pl.program_id(0), pl.program_id(2)
    ngrp = N_EXPERTS * W // LANES
    lane = lax.broadcasted_iota(I32, (1, LANES), 1)
    never = jnp.float32(-2.0 ** 20)
    for sb in range(nsub):
        tblk = blk * nsub + sb
        base = (b * nblk + tblk) * N_EXPERTS
        tok = slice(sb * tb, (sb + 1) * tb)
        rel = _dot(post_ref[tok, :], spread_ref[...]) - slot1_ref[...]

        def chunk(c, acc, base=base, rel=rel):
            tgt, wins = [], []
            for e in range(N_EXPERTS):
                a = aoff_ref[base + e] + c * W
                tgt.append(jnp.where(c < nch_ref[base + e], a.astype(F32), never))
                a_in = pl.multiple_of(jnp.minimum(a, rows - W), BF16_SUBLANES)
                wins.append(y_ref[0, e, pl.ds(a_in, W), :])
            cols = []
            for g in range(ngrp):
                first, last = (LANES * g) // W, (LANES * g + LANES - 1) // W
                t = jnp.full((1, LANES), tgt[last], F32)
                for e in range(last - 1, first - 1, -1):
                    t = jnp.where(lane < (e + 1) * W - LANES * g, tgt[e], t)
                hit = rel[:, g * LANES:(g + 1) * LANES] == t
                cols.append(jnp.where(hit, 1.0, 0.0).astype(BF16))
            return acc + _dot(jnp.concatenate(cols, axis=1), jnp.concatenate(wins, axis=0))

        acc = lax.fori_loop(0, nmax_ref[b * nblk + tblk], chunk, jnp.zeros((tb, o_ref.shape[1]), F32))
        o_ref[tok, :] = h_ref[tok, :] + acc


def _expand(aoff, nch, nmax, h, post, y, B, T, tb, dw, W):
    N, D = h.shape
    rows = y.shape[2]
    nblk = T // tb
    nsub = 2 if nblk % 2 == 0 else 1
    ns, ts = nblk // nsub, nsub * tb
    return pl.pallas_call(
        functools.partial(_expand_kernel, nblk=nblk, tb=tb, W=W, rows=rows, nsub=nsub),
        grid_spec=pltpu.PrefetchScalarGridSpec(
            num_scalar_prefetch=3,
            grid=(B, D // dw, ns),
            in_specs=[
                pl.BlockSpec((ts, dw), lambda b, d, k, *_: (b * ns + k, d)),
                pl.BlockSpec((ts, 2 * N_EXPERTS), lambda b, d, k, *_: (b * ns + k, 0)),
                pl.BlockSpec((2 * N_EXPERTS, N_EXPERTS * W), lambda b, d, k, *_: (0, 0)),
                pl.BlockSpec((1, N_EXPERTS * W), lambda b, d, k, *_: (0, 0)),
                pl.BlockSpec((1, N_EXPERTS, rows, dw), lambda b, d, k, *_: (b, 0, 0, d),
                             pipeline_mode=pl.Buffered(1)),
            ],
            out_specs=pl.BlockSpec((ts, dw), lambda b, d, k, *_: (b * ns + k, d)),
        ),
        out_shape=jax.ShapeDtypeStruct((N, D), F32),
        compiler_params=_cp("parallel", "parallel", "arbitrary"),
        name="moe_expand",
    )(aoff, nch, nmax, h, post, jnp.asarray(_spread_table(W), F32).astype(BF16),
      jnp.asarray((np.arange(N_EXPERTS * W) % W + 1).reshape(1, -1), F32), y)


def _moe(h, xn, gsp, aff4, wg, wu, wd, layer, B, T):
    N, D = h.shape
    tb = min(512, T)
    W = 96
    nblk = T // tb
    cap = EC_CAPACITY_FACTOR * T // N_EXPERTS
    live = -(-cap // BF16_SUBLANES) * BF16_SUBLANES
    rows = -(-(live + W) // LANES) * LANES
    dw = 256
    pos4, aoff4, cnt4 = _topk(aff4, T, tb)
    per = tb // LANES
    first = aoff4[:, ::per, :, 0]
    start = first & ~(BF16_SUBLANES - 1)
    aoff = start.reshape(-1)
    nch4 = (first - start + cnt4[:, ::per, :, 0] + (W - 1)) // W
    nch = nch4.reshape(-1)
    nmax = jnp.max(nch4, axis=-1).reshape(-1)
    post = jnp.transpose(pos4, (0, 1, 3, 2)).reshape(N, N_EXPERTS) + 1
    post = jnp.concatenate([post >> 5, post & 31], axis=1).astype(BF16)
    xg, gs = _compress(aoff, nch, nmax, xn, gsp, pos4, B, T, tb, rows, W)
    y = _ffn(xg, gs, wg, wu, wd, layer, live)
    return _expand(aoff, nch, nmax, h, post, y, B, T, tb, 2 * dw, W)


def _final_norm_kernel(h_ref, g_ref, o_ref):
    o_ref[...] = _rms(h_ref[...], g_ref[...])


def _final_norm(h, g, tm):
    N, D = h.shape
    return pl.pallas_call(
        _final_norm_kernel,
        grid=(N // tm,),
        in_specs=[pl.BlockSpec((tm, D), lambda i: (i, 0)), pl.BlockSpec((1, D), lambda i: (0, 0))],
        out_specs=pl.BlockSpec((tm, D), lambda i: (i, 0)),
        out_shape=jax.ShapeDtypeStruct((N, D), F32),
        compiler_params=_cp("parallel"),
        name="final_norm",
    )(h, g)


def kernel(x, mem, mix_norm_g, ffn_norm_g, mem_norm_g, final_norm_g, w_mem_kv, w_out, hy_w_in, hy_short_w, hy_filt_w1, hy_filt_b1, hy_filt_w2, hy_filt_b2, hy_filt_w3, hy_filt_freq, hy_skip, at_w_in, at_q_norm_g, at_k_norm_g, router_w, exp_w_gate, exp_w_up, exp_w_down):
    B, T, D = x.shape
    depth = mix_norm_g.shape[0]
    C = hy_skip.shape[-1]
    tm = min(512, T)
    cb = 16
    n1 = T // LANES
    h = x.reshape(B * T, D)
    row = lambda v: v.reshape(1, -1).astype(F32)
    lanes = lambda v: jnp.broadcast_to(v[..., None, None], v.shape + (1, LANES)).astype(F32)
    max_decay = math.log(HY_DECAY_TARGET) / HY_FAST_PCT
    min_decay = math.log(HY_DECAY_TARGET) / HY_SLOW_PCT
    absdelta = jnp.asarray(np.abs(np.linspace(min_decay, max_decay, C)).astype(np.float32)).reshape(C, 1)

    for i in range(depth):
        j = i // 2
        wo = w_out[i].astype(BF16)
        wkv = w_mem_kv[i].astype(BF16)
        mkt, mv = _memkv(mem, row(mem_norm_g), wkv[:, :D_MEM].T, wkv[:, D_MEM:])
        wr = router_w[i].astype(BF16)
        route_w = (row(ffn_norm_g[i]), jnp.zeros((D, LANES), BF16).at[:, :N_EXPERTS].set(wr), wr.T)
        if i % 2 == 0:
            w_in = hy_w_in[j].astype(BF16)
            ut, cq = _hy_inproj(h, row(mix_norm_g[i]), w_in[:, :3 * C].T, w_in[:, 3 * C:], B, T, tm)
            filt = dict(
                w1t=jnp.zeros((HY_FILT, 40), F32).at[:, :hy_filt_w1.shape[1]].set(hy_filt_w1[j].T),
                b1=hy_filt_b1[j].reshape(-1, 1), w2t=hy_filt_w2[j].T, b2=hy_filt_b2[j].reshape(-1, 1),
                fr=hy_filt_freq[j].reshape(-1, 1),
                w3t=hy_filt_w3[j].T.reshape(2, 2, C, HY_FILT), absdelta=absdelta,
            )
            kfft = _hyena_filters_fft(filt, T, 2 * cb)
            ut4 = ut.reshape(B, 3 * C, n1, LANES)
            if B % 2:
                ut4 = jnp.concatenate([ut4, jnp.zeros_like(ut4[:1])], axis=0)
            zt = _hy_conv(ut4, lanes(hy_short_w[j]), lanes(hy_skip[j]), kfft, B + B % 2, T, C, cb)[:B]
            routed = _outproj(h, zt, cq, mkt, mv, wo[:C], wo[C:], *route_w, B, T, min(1024, T))
        else:
            rep = lambda v, n: jnp.tile(v.astype(F32), n).reshape(1, -1)
            qt, kz, vtz, cq = _at_inproj(h, row(mix_norm_g[i]), at_w_in[j].astype(BF16),
                                         rep(at_q_norm_g[j], GQA_GROUP * N_KV_HEADS),
                                         rep(at_k_norm_g[j], N_KV_HEADS), B, T, tm)
            main_t = _flash(qt, kz, vtz, B, T, min(2048, T), min(1024, T))
            routed = _outproj(h, main_t, cq, mkt, mv, wo[:C], wo[C:], *route_w, B, T, tm)
        h = _moe(*routed, exp_w_gate, exp_w_up, exp_w_down, i, B, T)
    return _final_norm(h, row(final_norm_g), tm).reshape(B, T, D)
```
